```python
import math
import jax
import jax.numpy as jnp
from jax import lax
import numpy as np

D_MODEL = 1024
BATCH = 16
SEQ = 2048
DEPTH = 4

HEAD_DIM = 64
ROPE_THETA = 500000.0
ROT_DIM = HEAD_DIM // 4
NORM_EPS = 1e-6
BLOCK = 128
NEG_INF = -1e30

A_HEADS = 8
A_WIDTH = A_HEADS * HEAD_DIM
A_PATTERNS = ((128, 1), (512, 4), (2048, 16))

B_HEADS = 4
B_QK_DIM = 64
B_V_DIM = 128
B_CHUNK = 128
RET_THETA = 10000.0

MEM_TOKENS = 256
MEM_HEADS = 4
MEM_WIDTH = MEM_HEADS * HEAD_DIM

C_HEADS = 8
C_Q_RANK = 256
C_KV_RANK = 128
C_NOPE = 64
C_ROPE = 32
C_V = 64

D_HEADS = 8
D_HEAD = 64
D_WIDTH = D_HEADS * D_HEAD
D_DECAY_LORA = 64
D_AAA_LORA = 64
D_MV_LORA = 32
RWKV_LN_EPS = 64e-5

EVEN_MIX = A_WIDTH + B_HEADS * B_V_DIM + MEM_WIDTH
ODD_MIX = C_HEADS * C_V + D_WIDTH + MEM_WIDTH
EVEN_COLS = (A_WIDTH, A_WIDTH, A_WIDTH, B_HEADS * B_QK_DIM, B_HEADS * B_QK_DIM,
             B_HEADS * B_V_DIM, MEM_WIDTH, EVEN_MIX)

kernel_name = 'hybrid_dilated_retention_mla_rwkv7_trunk'


def split_cols(t, sizes):
    return jnp.split(t, np.cumsum(sizes)[:-1].tolist(), axis=-1)


def rms_norm(x, g):
    xf = x.astype(jnp.float32)
    y = xf * lax.rsqrt(jnp.mean(xf * xf, -1, keepdims=True) + NORM_EPS)
    return (y * g.astype(jnp.float32)).astype(x.dtype)


def rope_angles(positions, dim, theta):
    inv = jnp.exp(-math.log(theta) * jnp.arange(0, dim, 2, dtype=jnp.float32) / dim)
    ang = positions.astype(jnp.float32)[..., None] * inv
    return jnp.cos(ang), jnp.sin(ang)


def apply_rotary(x, cos, sin):
    half = x.shape[-1] // 2
    c = cos[:, :, None, :].astype(x.dtype)
    s = sin[:, :, None, :].astype(x.dtype)
    x1, x2 = x[..., :half], x[..., half:]
    return jnp.concatenate([x1 * c - x2 * s, x1 * s + x2 * c], -1)


def partial_rotary(x, cos, sin):
    return jnp.concatenate([apply_rotary(x[..., :ROT_DIM], cos, sin), x[..., ROT_DIM:]], -1)


def banded_window_attention(q, k, v, window):
    n, L, h, hd = q.shape
    blk = min(BLOCK, L)
    nb = -(-L // blk)
    pad = nb * blk - L
    if pad:
        q, k, v = (jnp.pad(t, ((0, 0), (0, pad), (0, 0), (0, 0))) for t in (q, k, v))
    qb, kb, vb = (t.reshape(n, nb, blk, h, hd) for t in (q, k, v))
    kk = jnp.concatenate([jnp.concatenate([jnp.zeros_like(kb[:, :1]), kb[:, :-1]], 1), kb], 2)
    vv = jnp.concatenate([jnp.concatenate([jnp.zeros_like(vb[:, :1]), vb[:, :-1]], 1), vb], 2)
    s = jnp.einsum('nbqhd,nbkhd->nbhqk', qb, kk).astype(jnp.float32) * hd ** -0.5
    qi = jnp.arange(blk)[:, None]
    kj = jnp.arange(2 * blk)[None, :] - blk
    dist = qi - kj
    kpos = (jnp.arange(nb) * blk)[:, None, None] + kj[None]
    mask = (dist >= 0)[None] & (dist <= window)[None] & (kpos >= 0)
    s = jnp.where(mask[None, :, None], s, NEG_INF)
    m = jnp.max(s, -1, keepdims=True)
    e = jnp.exp(s - m)
    den = jnp.sum(e, -1, keepdims=True)
    p = e / den
    lse = (m + jnp.log(den))[..., 0]
    o = jnp.einsum('nbhqk,nbkhd->nbqhd', p.astype(v.dtype), vv).reshape(n, nb * blk, h, hd)[:, :L]
    lse = lse.transpose(0, 1, 3, 2).reshape(n, nb * blk, h)[:, :L]
    return o, lse


def dilated_pattern(q, k, v, window, dilation):
    b, s, h, hd = q.shape
    L = s // dilation
    def to_sub(t):
        return t.reshape(b, L, dilation, h, hd).transpose(0, 2, 1, 3, 4).reshape(b * dilation, L, h, hd)
    o, lse = banded_window_attention(to_sub(q), to_sub(k), to_sub(v), window // dilation)
    o = o.reshape(b, dilation, L, h, hd).transpose(0, 2, 1, 3, 4).reshape(b, s, h, hd)
    lse = lse.reshape(b, dilation, L, h).transpose(0, 2, 1, 3).reshape(b, s, h)
    return o, lse


def dilated_mixture_attention(q, k, v):
    res = [dilated_pattern(q, k, v, w, d) for (w, d) in A_PATTERNS]
    outs = jnp.stack([r[0] for r in res], 0)
    wts = jax.nn.softmax(jnp.stack([r[1] for r in res], 0), axis=0)
    return jnp.einsum('pbsh,pbshd->bshd', wts.astype(q.dtype), outs)


def retention(q, k, v, rot):
    b, s, h, dk = q.shape
    dv = v.shape[-1]
    cos, sin = rot
    qf = apply_rotary(q, cos, sin).astype(jnp.float32)
    kf = apply_rotary(k, cos, sin).astype(jnp.float32) * dk ** -0.5
    vf = v.astype(jnp.float32)
    log_g = jnp.log1p(-jnp.exp2(-5.0 - jnp.arange(h, dtype=jnp.float32)))
    c = B_CHUNK
    n = s // c
    qc = qf.reshape(b, n, c, h, dk)
    kc = kf.reshape(b, n, c, h, dk)
    vc = vf.reshape(b, n, c, h, dv)
    idx = jnp.arange(c, dtype=jnp.float32)
    diff = idx[:, None] - idx[None, :]
    d_in = jnp.where(diff[None] >= 0, jnp.exp(jnp.maximum(diff, 0.0)[None] * log_g[:, None, None]), 0.0)
    scores = jnp.einsum('bnqhd,bnkhd->bnhqk', qc, kc) * d_in
    o_in = jnp.einsum('bnhqk,bnkhe->bnqhe', scores, vc)
    to_end = jnp.exp((c - 1.0 - idx)[:, None] * log_g[None])
    chunk_kv = jnp.einsum('bnkhd,kh,bnkhe->nbhde', kc, to_end, vc)
    g_chunk = jnp.exp(c * log_g)[None, :, None, None]
    def step(state, kv):
        return state * g_chunk + kv, state
    _, prev = lax.scan(step, jnp.zeros((b, h, dk, dv), jnp.float32), chunk_kv)
    from_start = jnp.exp((idx + 1.0)[:, None] * log_g[None])
    o_cross = jnp.einsum('bnqhd,nbhde->bnqhe', qc, prev) * from_start[None, None, :, :, None]
    o = (o_in + o_cross).reshape(b, s, h, dv)
    o = o * lax.rsqrt(jnp.mean(o * o, -1, keepdims=True) + NORM_EPS)
    return o.astype(v.dtype)


def causal_attention(q, k, v, scale):
    b, s, h, dq = q.shape
    nq = s // BLOCK
    qb = q.reshape(b, nq, BLOCK, h, dq).transpose(1, 0, 2, 3, 4)
    kpos = jnp.arange(s)
    def one(args):
        qblk, bi = args
        sc = jnp.einsum('bqhd,bkhd->bhqk', qblk, k).astype(jnp.float32) * scale
        qpos = bi * BLOCK + jnp.arange(BLOCK)
        sc = jnp.where((kpos[None, :] <= qpos[:, None])[None, None], sc, NEG_INF)
        p = jax.nn.softmax(sc, axis=-1)
        return jnp.einsum('bhqk,bkhd->bqhd', p.astype(v.dtype), v)
    o = lax.map(one, (qb, jnp.arange(nq)))
    return o.transpose(1, 0, 2, 3, 4).reshape(b, s, h, v.shape[-1])


def mla_attention(cq, ckv, k_rope, rot, q_norm, w_qb, kv_norm, w_kvb):
    b, s, _ = cq.shape
    cos, sin = rot
    q = (rms_norm(cq, q_norm) @ w_qb).reshape(b, s, C_HEADS, C_NOPE + C_ROPE)
    q = jnp.concatenate([q[..., :C_NOPE], apply_rotary(q[..., C_NOPE:], cos, sin)], -1)
    kv = (rms_norm(ckv, kv_norm) @ w_kvb).reshape(b, s, C_HEADS, C_NOPE + C_V)
    k_pe = apply_rotary(k_rope[:, :, None, :], cos, sin)
    k = jnp.concatenate([kv[..., :C_NOPE], jnp.broadcast_to(k_pe, (b, s, C_HEADS, C_ROPE))], -1)
    o = causal_attention(q, k, kv[..., C_NOPE:], (C_NOPE + C_ROPE) ** -0.5)
    return o.reshape(b, s, C_HEADS * C_V)


def memory_attention(qm, mem_n, w_mem_kv):
    b, s, _ = qm.shape
    kv = mem_n @ w_mem_kv
    km = kv[..., :MEM_WIDTH].reshape(b, MEM_TOKENS, MEM_HEADS, HEAD_DIM)
    vm = kv[..., MEM_WIDTH:].reshape(b, MEM_TOKENS, MEM_HEADS, HEAD_DIM)
    q = qm.reshape(b, s, MEM_HEADS, HEAD_DIM)
    sc = jnp.einsum('bshd,bmhd->bhsm', q, km).astype(jnp.float32) * HEAD_DIM ** -0.5
    p = jax.nn.softmax(sc, axis=-1)
    return jnp.einsum('bhsm,bmhd->bshd', p.astype(vm.dtype), vm).reshape(b, s, MEM_WIDTH)


def token_shift(h, mu):
    prev = jnp.pad(h, ((0, 0), (1, 0), (0, 0)))[:, :-1]
    return h + (prev - h) * mu


def wkv7_scan(r, w, k, v, a, bb):
    b, s, h, n = r.shape
    def step(state, inp):
        rt, wt, kt, vt, at, bt = inp
        sa = jnp.einsum('bhvk,bhk->bhv', state, at)
        state = state * wt[:, :, None, :] + sa[..., None] * bt[:, :, None, :] + vt[..., None] * kt[:, :, None, :]
        return state, jnp.einsum('bhvk,bhk->bhv', state, rt)
    xs = tuple(t.astype(jnp.float32).transpose(1, 0, 2, 3) for t in (r, w, k, v, a, bb))
    _, y = lax.scan(step, jnp.zeros((b, h, n, n), jnp.float32), xs)
    return y.transpose(1, 0, 2, 3)


def rwkv7_time_mix(shifted, v_first, w0, w2, a0, a2, v0, v2, k_k, k_a, r_k, lnx_w, lnx_b):
    b, s, _ = shifted.shape
    sizes = [D_WIDTH] * 3 + [D_DECAY_LORA, D_AAA_LORA] + ([D_MV_LORA] if v_first is not None else [])
    parts = split_cols(shifted, sizes)
    r, k, v, w_dn, a_dn = parts[:5]
    logw = -jax.nn.softplus(-(w0 + jnp.tanh(w_dn) @ w2).astype(jnp.float32)) - 0.5
    decay = jnp.exp(-jnp.exp(logw))
    a = jax.nn.sigmoid(a0 + a_dn @ a2)
    if v_first is None:
        v_first = v
    else:
        v = v + (v_first - v) * jax.nn.sigmoid(v0 + parts[5] @ v2)
    hd = lambda t: t.reshape(b, s, D_HEADS, D_HEAD)
    kk = hd(k * k_k).astype(jnp.float32)
    kk = kk / jnp.maximum(jnp.sqrt(jnp.sum(kk * kk, -1, keepdims=True)), 1e-12)
    k = k * (1.0 + (a - 1.0) * k_a)
    ah = hd(a).astype(jnp.float32)
    y = wkv7_scan(hd(r), hd(decay), hd(k), hd(v), -kk, kk * ah)
    mu = jnp.mean(y, -1, keepdims=True)
    var = jnp.mean((y - mu) ** 2, -1, keepdims=True)
    y = ((y - mu) * lax.rsqrt(var + RWKV_LN_EPS)).reshape(b, s, D_WIDTH) * lnx_w + lnx_b
    bonus = jnp.sum(hd(r) * hd(k) * r_k, -1, keepdims=True) * hd(v)
    return (y + bonus.reshape(b, s, D_WIDTH)).astype(shifted.dtype), v_first


def even_layer(x, mem_n, rot, ret_rot, norm, w_in, w_mem_kv, w_out):
    b, s, _ = x.shape
    h = rms_norm(x, norm)
    qa, ka, va, qr, kr, vr, qm, gate = split_cols(h @ w_in, EVEN_COLS)
    cos, sin = rot
    heads = lambda t, nh: t.reshape(b, s, nh, -1)
    a_out = dilated_mixture_attention(partial_rotary(heads(qa, A_HEADS), cos, sin),
                                      partial_rotary(heads(ka, A_HEADS), cos, sin), heads(va, A_HEADS))
    r_out = retention(heads(qr, B_HEADS), heads(kr, B_HEADS), heads(vr, B_HEADS), ret_rot)
    m_out = memory_attention(qm, mem_n, w_mem_kv)
    y = jnp.concatenate([a_out.reshape(b, s, A_WIDTH), r_out.reshape(b, s, B_HEADS * B_V_DIM), m_out], -1)
    return x + (y * jax.nn.silu(gate)) @ w_out


def odd_layer(x, mem_n, mla_rot, v_first, norm, w_in, q_norm, w_qb, kv_norm, w_kvb, mu_shift,
              w0, w2, a0, a2, v0, v2, k_k, k_a, r_k, lnx_w, lnx_b, w_mem_kv, w_out):
    h = rms_norm(x, norm)
    d_cols = mu_shift.shape[0]
    cq, ckv, krope, dproj, qm, gate = split_cols(
        h @ w_in, [C_Q_RANK, C_KV_RANK, C_ROPE, d_cols, MEM_WIDTH, ODD_MIX])
    c_out = mla_attention(cq, ckv, krope, mla_rot, q_norm, w_qb, kv_norm, w_kvb)
    d_out, v_first = rwkv7_time_mix(token_shift(dproj, mu_shift), v_first, w0, w2, a0, a2, v0, v2,
                                    k_k, k_a, r_k, lnx_w, lnx_b)
    m_out = memory_attention(qm, mem_n, w_mem_kv)
    y = jnp.concatenate([c_out, d_out, m_out], -1)
    return x + (y * jax.nn.silu(gate)) @ w_out, v_first


def setup_inputs(seed: int = 0):
    key = jax.random.key(seed)
    keys = iter(jax.random.split(key, 64 + 32 * DEPTH))
    f32 = jnp.float32
    def normal(shape, scale):
        return jax.random.normal(next(keys), shape, f32) * scale
    def gain(n):
        return 1.0 + 0.02 * jax.random.normal(next(keys), (n,), f32)
    def uniform(shape, lo, hi):
        return jax.random.uniform(next(keys), shape, f32, lo, hi)
    out_scale = (2.0 * DEPTH) ** -0.5
    p = {}
    p['x'] = normal((BATCH, SEQ, D_MODEL), 1.0)
    p['mem'] = normal((BATCH, MEM_TOKENS, D_MODEL), 1.0)
    offsets = jax.random.randint(next(keys), (BATCH, 1), 0, 4096, dtype=jnp.int32)
    p['positions'] = jnp.arange(SEQ, dtype=jnp.int32)[None, :] + offsets
    p['mem_norm'] = gain(D_MODEL)
    p['final_norm'] = gain(D_MODEL)
    for layer in range(DEPTH):
        pre = 'l%d_' % layer
        p[pre + 'norm'] = gain(D_MODEL)
        if layer % 2 == 0:
            p[pre + 'w_in'] = normal((D_MODEL, sum(EVEN_COLS)), D_MODEL ** -0.5)
            p[pre + 'w_mem_kv'] = normal((D_MODEL, 2 * MEM_WIDTH), D_MODEL ** -0.5)
            p[pre + 'w_out'] = normal((EVEN_MIX, D_MODEL), EVEN_MIX ** -0.5 * out_scale)
        else:
            vres = layer > 1
            d_cols = 3 * D_WIDTH + D_DECAY_LORA + D_AAA_LORA + (D_MV_LORA if vres else 0)
            odd_cols = C_Q_RANK + C_KV_RANK + C_ROPE + d_cols + MEM_WIDTH + ODD_MIX
            p[pre + 'w_in'] = normal((D_MODEL, odd_cols), D_MODEL ** -0.5)
            p[pre + 'q_norm'] = gain(C_Q_RANK)
            p[pre + 'w_qb'] = normal((C_Q_RANK, C_HEADS * (C_NOPE + C_ROPE)), C_Q_RANK ** -0.5)
            p[pre + 'kv_norm'] = gain(C_KV_RANK)
            p[pre + 'w_kvb'] = normal((C_KV_RANK, C_HEADS * (C_NOPE + C_V)), C_KV_RANK ** -0.5)
            p[pre + 'mu_shift'] = uniform((d_cols,), 0.0, 1.0)
            p[pre + 'w0'] = uniform((D_WIDTH,), -3.0, 1.0)
            p[pre + 'w2'] = normal((D_DECAY_LORA, D_WIDTH), 0.1 * D_DECAY_LORA ** -0.5)
            p[pre + 'a0'] = normal((D_WIDTH,), 0.1)
            p[pre + 'a2'] = normal((D_AAA_LORA, D_WIDTH), 0.1 * D_AAA_LORA ** -0.5)
            if vres:
                p[pre + 'v0'] = 1.0 + normal((D_WIDTH,), 0.1)
                p[pre + 'v2'] = normal((D_MV_LORA, D_WIDTH), 0.1 * D_MV_LORA ** -0.5)
            p[pre + 'k_k'] = 0.85 + normal((D_WIDTH,), 0.02)
            p[pre + 'k_a'] = 1.0 + normal((D_WIDTH,), 0.02)
            p[pre + 'r_k'] = normal((D_HEADS, D_HEAD), 0.1)
            p[pre + 'lnx_w'] = gain(D_WIDTH)
            p[pre + 'lnx_b'] = normal((D_WIDTH,), 0.02)
            p[pre + 'w_mem_kv'] = normal((D_MODEL, 2 * MEM_WIDTH), D_MODEL ** -0.5)
            p[pre + 'w_out'] = normal((ODD_MIX, D_MODEL), ODD_MIX ** -0.5 * out_scale)
    return p


def reference(x, mem, positions, mem_norm, final_norm,
              l0_norm, l0_w_in, l0_w_mem_kv, l0_w_out,
              l1_norm, l1_w_in, l1_q_norm, l1_w_qb, l1_kv_norm, l1_w_kvb, l1_mu_shift,
              l1_w0, l1_w2, l1_a0, l1_a2, l1_k_k, l1_k_a, l1_r_k, l1_lnx_w, l1_lnx_b,
              l1_w_mem_kv, l1_w_out,
              l2_norm, l2_w_in, l2_w_mem_kv, l2_w_out,
              l3_norm, l3_w_in, l3_q_norm, l3_w_qb, l3_kv_norm, l3_w_kvb, l3_mu_shift,
              l3_w0, l3_w2, l3_a0, l3_a2, l3_v0, l3_v2, l3_k_k, l3_k_a, l3_r_k, l3_lnx_w, l3_lnx_b,
              l3_w_mem_kv, l3_w_out):
    rot = rope_angles(positions, ROT_DIM, ROPE_THETA)
    ret_rot = rope_angles(positions, B_QK_DIM, RET_THETA)
    mla_rot = rope_angles(positions, C_ROPE, ROPE_THETA)
    mem_n = rms_norm(mem, mem_norm)
    layers = (
        dict(norm=l0_norm, w_in=l0_w_in, w_mem_kv=l0_w_mem_kv, w_out=l0_w_out),
        dict(norm=l1_norm, w_in=l1_w_in, q_norm=l1_q_norm, w_qb=l1_w_qb, kv_norm=l1_kv_norm,
             w_kvb=l1_w_kvb, mu_shift=l1_mu_shift, w0=l1_w0, w2=l1_w2, a0=l1_a0, a2=l1_a2,
             v0=None, v2=None, k_k=l1_k_k, k_a=l1_k_a, r_k=l1_r_k, lnx_w=l1_lnx_w, lnx_b=l1_lnx_b,
             w_mem_kv=l1_w_mem_kv, w_out=l1_w_out),
        dict(norm=l2_norm, w_in=l2_w_in, w_mem_kv=l2_w_mem_kv, w_out=l2_w_out),
        dict(norm=l3_norm, w_in=l3_w_in, q_norm=l3_q_norm, w_qb=l3_w_qb, kv_norm=l3_kv_norm,
             w_kvb=l3_w_kvb, mu_shift=l3_mu_shift, w0=l3_w0, w2=l3_w2, a0=l3_a0, a2=l3_a2,
             v0=l3_v0, v2=l3_v2, k_k=l3_k_k, k_a=l3_k_a, r_k=l3_r_k, lnx_w=l3_lnx_w, lnx_b=l3_lnx_b,
             w_mem_kv=l3_w_mem_kv, w_out=l3_w_out),
    )
    v_first = None
    for i in range(DEPTH):
        if i % 2 == 0:
            x = even_layer(x, mem_n, rot, ret_rot, **layers[i])
        else:
            x, v_first = odd_layer(x, mem_n, mla_rot, v_first, **layers[i])
    return rms_norm(x, final_norm)
```

```python
import functools
import math

import numpy as np
import jax
import jax.numpy as jnp
from jax import lax
from jax.experimental import pallas as pl
from jax.experimental.pallas import tpu as pltpu

F32 = jnp.float32
BF16 = jnp.bfloat16

LANES = 128
VMEM_LIMIT = 48 * 1024 * 1024

D_MODEL = 1024
HEAD_DIM = 64
ROPE_THETA = 500000.0
ROT_DIM = HEAD_DIM // 4
NORM_EPS = 1e-6
NEG_INF = -1e30

A_HEADS = 8
A_WIDTH = A_HEADS * HEAD_DIM
A_PATTERNS = ((128, 1), (512, 4), (2048, 16))
A_BLOCK = 128

B_HEADS = 4
B_QK_DIM = 64
B_V_DIM = 128
B_CHUNK = 128
RET_THETA = 10000.0

MEM_TOKENS = 256
MEM_HEADS = 4
MEM_WIDTH = MEM_HEADS * HEAD_DIM

C_HEADS = 8
C_Q_RANK = 256
C_KV_RANK = 128
C_NOPE = 64
C_ROPE = 32
C_V = 64

D_HEADS = 8
D_HEAD = 64
D_WIDTH = D_HEADS * D_HEAD
D_DECAY_LORA = 64
D_AAA_LORA = 64
D_MV_LORA = 32
RWKV_LN_EPS = 64e-5
D_CHUNK = 64
D_PROJ_PAD = 3 * D_WIDTH + 2 * LANES

MIX_WIDTH = 1280


def _cparams(*sem):
    return pltpu.CompilerParams(dimension_semantics=sem, vmem_limit_bytes=VMEM_LIMIT)


def _dot(a, b):
    return jnp.dot(a, b, preferred_element_type=F32)


def _dot_nt(a, b):
    return lax.dot_general(a, b, (((1,), (1,)), ((), ())), preferred_element_type=F32)


def _dot_tn(a, b):
    return lax.dot_general(a, b, (((0,), (0,)), ((), ())), preferred_element_type=F32)


def _rot_slab(x, c, s1, s2, half):
    return x * c + pltpu.roll(x, half, 1) * s1 + pltpu.roll(x, LANES - half, 1) * s2


def _rotary(x, c, s1, s2, half):
    slabs = [_rot_slab(x[:, i:i + LANES], c, s1, s2, half) for i in range(0, x.shape[1], LANES)]
    return slabs[0] if len(slabs) == 1 else jnp.concatenate(slabs, axis=1)


def _norm_proj_kernel(*refs, plan, n_tab, n_out):
    x_ref, g_ref, w_ref = refs[:3]
    tabs = refs[3:3 + 3 * n_tab]
    outs = refs[3 + 3 * n_tab:3 + 3 * n_tab + n_out]
    x = x_ref[...].astype(F32)
    ms = jnp.mean(x * x, axis=-1, keepdims=True)
    xn = (x * lax.rsqrt(ms + NORM_EPS) * g_ref[...]).astype(BF16)
    for (wc, width, oi, oc, rot) in plan:
        acc = _dot(xn, w_ref[:, wc:wc + width])
        if rot is not None:
            t, half = rot
            acc = _rotary(acc, tabs[3 * t][...], tabs[3 * t + 1][...], tabs[3 * t + 2][...], half)
        outs[oi][:, oc:oc + width] = acc.astype(outs[oi].dtype)


def _norm_proj(x2d, g, w, plan, out_widths, tables=(), tm=256):
    t_rows, k = x2d.shape
    n = w.shape[1]
    assert t_rows % tm == 0
    flat_tabs = [t for tab in tables for t in tab]
    in_specs = [pl.BlockSpec((tm, k), lambda i: (i, 0)),
                pl.BlockSpec((1, k), lambda i: (0, 0)),
                pl.BlockSpec((k, n), lambda i: (0, 0))]
    in_specs += [pl.BlockSpec((tm, LANES), lambda i: (i, 0)) for _ in flat_tabs]
    out_specs = [pl.BlockSpec((tm, ow), lambda i: (i, 0)) for ow in out_widths]
    out_shape = [jax.ShapeDtypeStruct((t_rows, ow), BF16) for ow in out_widths]
    kern = functools.partial(_norm_proj_kernel, plan=tuple(plan), n_tab=len(tables), n_out=len(out_widths))
    return pl.pallas_call(
        kern, grid=(t_rows // tm,), in_specs=in_specs, out_specs=out_specs, out_shape=out_shape,
        compiler_params=_cparams("parallel"), name="norm_proj",
    )(x2d, g.reshape(1, k).astype(F32), w, *flat_tabs)


def _band_attn_kernel(q_ref, kp_ref, kc_ref, vp_ref, vc_ref, o_ref, lse_ref):
    i = pl.program_id(2)
    qi = lax.broadcasted_iota(jnp.int32, (A_BLOCK, A_BLOCK), 0)
    kj = lax.broadcasted_iota(jnp.int32, (A_BLOCK, A_BLOCK), 1)
    prev_ok = (kj >= qi) & (i > 0)
    cur_ok = kj <= qi
    scale = HEAD_DIM ** -0.5
    for h in range(A_HEADS):
        sl = slice(h * HEAD_DIM, (h + 1) * HEAD_DIM)
        q = q_ref[:, sl]
        sp = jnp.where(prev_ok, _dot_nt(q, kp_ref[:, sl]) * scale, NEG_INF)
        sc = jnp.where(cur_ok, _dot_nt(q, kc_ref[:, sl]) * scale, NEG_INF)
        m = jnp.maximum(jnp.max(sp, -1, keepdims=True), jnp.max(sc, -1, keepdims=True))
        ep = jnp.exp(sp - m)
        ec = jnp.exp(sc - m)
        den = jnp.sum(ep, -1, keepdims=True) + jnp.sum(ec, -1, keepdims=True)
        o = _dot(ep.astype(BF16), vp_ref[:, sl]) + _dot(ec.astype(BF16), vc_ref[:, sl])
        o_ref[:, sl] = (o / den).astype(o_ref.dtype)
        lse_ref[:, sl] = jnp.broadcast_to(m + jnp.log(den), (A_BLOCK, HEAD_DIM))


def _band_attn(main, dilation):
    b, s, w = main.shape
    sub_len = s // dilation
    nblk = sub_len // A_BLOCK
    cb = w // A_WIDTH
    view = main.reshape(b, sub_len, dilation * w)
    blk = (None, A_BLOCK, A_WIDTH)

    def spec(col, prev):
        if prev:
            return pl.BlockSpec(blk, lambda bi, r, i: (bi, jnp.maximum(i - 1, 0), r * cb + col))
        return pl.BlockSpec(blk, lambda bi, r, i: (bi, i, r * cb + col))

    out_spec = pl.BlockSpec(blk, lambda bi, r, i: (bi, i, r))
    o, lse = pl.pallas_call(
        _band_attn_kernel, grid=(b, dilation, nblk),
        in_specs=[spec(0, False), spec(1, True), spec(1, False), spec(2, True), spec(2, False)],
        out_specs=[out_spec, out_spec],
        out_shape=[jax.ShapeDtypeStruct((b, sub_len, dilation * A_WIDTH), BF16),
                   jax.ShapeDtypeStruct((b, sub_len, dilation * A_WIDTH), F32)],
        compiler_params=_cparams("parallel", "parallel", "parallel"), name="band_attn_d%d" % dilation,
    )(view, view, view, view, view)
    return o.reshape(b * s, A_WIDTH), lse.reshape(b * s, A_WIDTH)


def _retention_kernel(qk_ref, v_ref, o_ref, state_ref):
    c = B_CHUNK
    @pl.when(pl.program_id(1) == 0)
    def _():
        state_ref[...] = jnp.zeros_like(state_ref)

    qi = lax.broadcasted_iota(jnp.int32, (c, c), 0)
    kj = lax.broadcasted_iota(jnp.int32, (c, c), 1)
    diff = (qi - kj).astype(F32)
    row = lax.broadcasted_iota(jnp.int32, (c, 1), 0).astype(F32)
    kt_all = qk_ref[:, B_HEADS * B_QK_DIM:].astype(F32).T
    for h in range(B_HEADS):
        log_g = math.log1p(-2.0 ** (-5.0 - h))
        q = qk_ref[:, h * B_QK_DIM:(h + 1) * B_QK_DIM]
        k = qk_ref[:, (B_HEADS + h) * B_QK_DIM:(B_HEADS + h + 1) * B_QK_DIM]
        v = v_ref[:, h * B_V_DIM:(h + 1) * B_V_DIM]
        d_in = jnp.where(diff >= 0, jnp.exp(jnp.maximum(diff, 0.0) * log_g), 0.0)
        scores = _dot_nt(q, k) * (B_QK_DIM ** -0.5) * d_in
        o = _dot(scores.astype(BF16), v)
        state = state_ref[h]
        from_start = jnp.exp((row + 1.0) * log_g)
        o = o + _dot(q, state.astype(BF16)) * from_start
        to_end = jnp.exp((c - 1.0 - lax.broadcasted_iota(jnp.int32, (1, c), 1).astype(F32)) * log_g)
        kt = (kt_all[h * B_QK_DIM:(h + 1) * B_QK_DIM, :] * (to_end * B_QK_DIM ** -0.5)).astype(BF16)
        state_ref[h] = state * math.exp(c * log_g) + _dot(kt, v)
        o = o * lax.rsqrt(jnp.mean(o * o, -1, keepdims=True) + NORM_EPS)
        o_ref[:, h * B_V_DIM:(h + 1) * B_V_DIM] = o.astype(o_ref.dtype)


def _retention(main):
    b, s, _ = main.shape
    blk = (None, B_CHUNK, 512)
    o = pl.pallas_call(
        _retention_kernel, grid=(b, s // B_CHUNK),
        in_specs=[pl.BlockSpec(blk, lambda bi, i: (bi, i, 3)), pl.BlockSpec(blk, lambda bi, i: (bi, i, 4))],
        out_specs=pl.BlockSpec(blk, lambda bi, i: (bi, i, 0)),
        out_shape=jax.ShapeDtypeStruct((b, s, B_HEADS * B_V_DIM), BF16),
        scratch_shapes=[pltpu.VMEM((B_HEADS, B_QK_DIM, B_V_DIM), F32)],
        compiler_params=_cparams("parallel", "arbitrary"), name="retention",
    )(main, main)
    return o.reshape(b * s, B_HEADS * B_V_DIM)


def _mem_attn_kernel(q_ref, kv_ref, o_ref):
    scale = HEAD_DIM ** -0.5
    for h in range(MEM_HEADS):
        sl = slice(h * HEAD_DIM, (h + 1) * HEAD_DIM)
        sv = slice(MEM_WIDTH + h * HEAD_DIM, MEM_WIDTH + (h + 1) * HEAD_DIM)
        s = _dot_nt(q_ref[:, sl], kv_ref[:, sl]) * scale
        m = jnp.max(s, -1, keepdims=True)
        e = jnp.exp(s - m)
        den = jnp.sum(e, -1, keepdims=True)
        o = _dot(e.astype(BF16), kv_ref[:, sv])
        o_ref[:, sl] = (o / den).astype(o_ref.dtype)


def _mem_attn(qm, kv, b, s, tq=512):
    o = pl.pallas_call(
        _mem_attn_kernel, grid=(b, s // tq),
        in_specs=[pl.BlockSpec((None, tq, MEM_WIDTH), lambda bi, i: (bi, i, 0)),
                  pl.BlockSpec((None, MEM_TOKENS, 2 * MEM_WIDTH), lambda bi, i: (bi, 0, 0))],
        out_specs=pl.BlockSpec((None, tq, MEM_WIDTH), lambda bi, i: (bi, i, 0)),
        out_shape=jax.ShapeDtypeStruct((b, s, MEM_WIDTH), BF16),
        compiler_params=_cparams("parallel", "parallel"), name="mem_attn",
    )(qm.reshape(b, s, MEM_WIDTH), kv.reshape(b, MEM_TOKENS, 2 * MEM_WIDTH))
    return o.reshape(b * s, MEM_WIDTH)


def _out_proj_kernel(*refs, mix, final):
    if mix:
        o1, o2, o3, l1, l2, l3 = refs[:6]
        refs = refs[6:]
        m = jnp.maximum(jnp.maximum(l1[...], l2[...]), l3[...])
        e1 = jnp.exp(l1[...] - m)
        e2 = jnp.exp(l2[...] - m)
        e3 = jnp.exp(l3[...] - m)
        first = (e1 * o1[...].astype(F32) + e2 * o2[...].astype(F32) + e3 * o3[...].astype(F32)) / (e1 + e2 + e3)
    else:
        first = refs[0][...].astype(F32)
        refs = refs[1:]
    second, third, gate_ref, x_ref, w_ref = refs[:5]
    refs = refs[5:]
    if final:
        g_ref, out_ref = refs
    else:
        (out_ref,) = refs
    g = gate_ref[...].astype(F32)
    sg = g * (1.0 / (1.0 + jnp.exp(-g)))
    w1 = first.shape[1]
    w2 = w1 + second.shape[1]
    y1 = (first * sg[:, :w1]).astype(BF16)
    y2 = (second[...].astype(F32) * sg[:, w1:w2]).astype(BF16)
    y3 = (third[...].astype(F32) * sg[:, w2:]).astype(BF16)
    xn = x_ref[...] + _dot(y1, w_ref[:w1, :]) + _dot(y2, w_ref[w1:w2, :]) + _dot(y3, w_ref[w2:, :])
    if final:
        xn = xn * lax.rsqrt(jnp.mean(xn * xn, -1, keepdims=True) + NORM_EPS) * g_ref[...]
    out_ref[...] = xn


def _out_proj(firsts, second, third, gate, x2d, w_out, final_gain=None, tm=256):
    t_rows, d = x2d.shape
    mix = len(firsts) == 6
    ins = list(firsts) + [second, third, gate, x2d, w_out]
    in_specs = [pl.BlockSpec((tm, a.shape[1]), lambda i: (i, 0)) for a in ins[:-1]]
    in_specs.append(pl.BlockSpec(w_out.shape, lambda i: (0, 0)))
    if final_gain is not None:
        ins.append(final_gain.reshape(1, d).astype(F32))
        in_specs.append(pl.BlockSpec((1, d), lambda i: (0, 0)))
    kern = functools.partial(_out_proj_kernel, mix=mix, final=final_gain is not None)
    return pl.pallas_call(
        kern, grid=(t_rows // tm,), in_specs=in_specs,
        out_specs=pl.BlockSpec((tm, d), lambda i: (i, 0)),
        out_shape=jax.ShapeDtypeStruct((t_rows, d), F32),
        compiler_params=_cparams("parallel"), name="out_proj",
    )(*ins)


def _mla_prep_kernel(in_ref, qn_ref, kvn_ref, wq_ref, wk_ref, wv_ref, c_ref, s1_ref, s2_ref,
                     q_out, k_out, v_out):
    half = C_ROPE // 2
    c, s1, s2 = c_ref[...], s1_ref[...], s2_ref[...]
    cq = in_ref[:, :C_Q_RANK].astype(F32)
    ckv = in_ref[:, C_Q_RANK:C_Q_RANK + C_KV_RANK].astype(F32)
    kr = in_ref[:, C_Q_RANK + C_KV_RANK:].astype(F32)
    cqn = (cq * lax.rsqrt(jnp.mean(cq * cq, -1, keepdims=True) + NORM_EPS) * qn_ref[...]).astype(BF16)
    ckvn = (ckv * lax.rsqrt(jnp.mean(ckv * ckv, -1, keepdims=True) + NORM_EPS) * kvn_ref[...]).astype(BF16)
    q_out[...] = _rotary(_dot(cqn, wq_ref[...]), c, s1, s2, half).astype(q_out.dtype)
    kpe = _rot_slab(kr, c, s1, s2, half)
    kn = _dot(ckvn, wk_ref[...])
    for h in range(C_HEADS):
        k_out[:, h * LANES:(h + 1) * LANES] = (kn[:, h * LANES:(h + 1) * LANES] + kpe).astype(k_out.dtype)
    v_out[...] = _dot(ckvn, wv_ref[...]).astype(v_out.dtype)


def _mla_prep(mla_in, q_norm, kv_norm, wq, wk, wv, tabs, tm=256):
    t_rows = mla_in.shape[0]
    row = lambda w: pl.BlockSpec((tm, w), lambda i: (i, 0))
    full = lambda a: pl.BlockSpec(a.shape, lambda i: (0, 0))
    qn = q_norm.reshape(1, -1).astype(F32)
    kvn = kv_norm.reshape(1, -1).astype(F32)
    return pl.pallas_call(
        _mla_prep_kernel, grid=(t_rows // tm,),
        in_specs=[row(512), full(qn), full(kvn), full(wq), full(wk), full(wv), row(LANES), row(LANES), row(LANES)],
        out_specs=[row(C_HEADS * LANES), row(C_HEADS * LANES), row(C_HEADS * C_V)],
        out_shape=[jax.ShapeDtypeStruct((t_rows, C_HEADS * LANES), BF16),
                   jax.ShapeDtypeStruct((t_rows, C_HEADS * LANES), BF16),
                   jax.ShapeDtypeStruct((t_rows, C_HEADS * C_V), BF16)],
        compiler_params=_cparams("parallel"), name="mla_prep",
    )(mla_in, qn, kvn, wq, wk, wv, *tabs)


def _mla_attn_kernel(q_ref, k_ref, v_ref, o_ref, *, tq):
    i = pl.program_id(1)
    scale = (C_NOPE + C_ROPE) ** -0.5
    qpos = i * tq + lax.broadcasted_iota(jnp.int32, (tq, tq), 0)
    kofs = lax.broadcasted_iota(jnp.int32, (tq, tq), 1)
    for h in range(C_HEADS):
        q = q_ref[:, h * LANES:(h + 1) * LANES]

        def body(j, carry, q=q, h=h):
            m, l, acc = carry
            rows = pl.ds(pl.multiple_of(j * tq, tq), tq)
            s = _dot_nt(q, k_ref[rows, h * LANES:(h + 1) * LANES]) * scale
            s = jnp.where(j * tq + kofs <= qpos, s, NEG_INF)
            m_new = jnp.maximum(m, jnp.max(s, -1, keepdims=True))
            alpha = jnp.exp(m - m_new)
            p = jnp.exp(s - m_new)
            l = alpha * l + jnp.sum(p, -1, keepdims=True)
            acc = alpha * acc + _dot(p.astype(BF16), v_ref[rows, h * C_V:(h + 1) * C_V])
            return m_new, l, acc

        init = (jnp.full((tq, 1), NEG_INF, F32), jnp.zeros((tq, 1), F32), jnp.zeros((tq, C_V), F32))
        m, l, acc = lax.fori_loop(0, i + 1, body, init)
        o_ref[:, h * C_V:(h + 1) * C_V] = (acc / l).astype(o_ref.dtype)


def _mla_attn(q, k, v, b, s, tq=256):
    qw, vw = C_HEADS * LANES, C_HEADS * C_V
    o = pl.pallas_call(
        functools.partial(_mla_attn_kernel, tq=tq), grid=(b, s // tq),
        in_specs=[pl.BlockSpec((None, tq, qw), lambda bi, i: (bi, i, 0)),
                  pl.BlockSpec((None, s, qw), lambda bi, i: (bi, 0, 0)),
                  pl.BlockSpec((None, s, vw), lambda bi, i: (bi, 0, 0))],
        out_specs=pl.BlockSpec((None, tq, vw), lambda bi, i: (bi, i, 0)),
        out_shape=jax.ShapeDtypeStruct((b, s, vw), BF16),
        compiler_params=_cparams("parallel", "parallel"), name="mla_attn",
    )(q.reshape(b, s, qw), k.reshape(b, s, qw), v.reshape(b, s, vw))
    return o.reshape(b * s, vw)


def _split3(x):
    hi = x.astype(BF16)
    r1 = x - hi.astype(F32)
    mid = r1.astype(BF16)
    lo = (r1 - mid.astype(F32)).astype(BF16)
    return hi, mid, lo


def _rwkv_kernel(*refs, vres):
    c = D_CHUNK
    if vres:
        (dp_ref, vf_ref, mu_ref, w0_ref, w2_ref, a0_ref, a2_ref, v0_ref, v2_ref,
         kk_ref, ka_ref, rk_ref, lw_ref, lb_ref, out_ref, state_ref, carry_ref) = refs
    else:
        (dp_ref, mu_ref, w0_ref, w2_ref, a0_ref, a2_ref,
         kk_ref, ka_ref, rk_ref, lw_ref, lb_ref, out_ref, vf_out, state_ref, carry_ref) = refs

    @pl.when(pl.program_id(1) == 0)
    def _():
        state_ref[...] = jnp.zeros_like(state_ref)
        carry_ref[...] = jnp.zeros_like(carry_ref)

    hcur = dp_ref[...].astype(F32)
    row = lax.broadcasted_iota(jnp.int32, (c, 1), 0)
    prev = jnp.where(row == 0, carry_ref[...], pltpu.roll(hcur, 1, 0))
    carry_ref[...] = hcur[c - 1:c, :]
    sh = hcur + (prev - hcur) * mu_ref[...]

    w = D_WIDTH
    r = sh[:, 0:w]
    k = sh[:, w:2 * w]
    v = sh[:, 2 * w:3 * w]
    wa = sh[:, 3 * w:3 * w + LANES]
    pre_w = w0_ref[...] + _dot(jnp.tanh(wa).astype(BF16), w2_ref[...])
    logw = -(jnp.maximum(-pre_w, 0.0) + jnp.log1p(jnp.exp(-jnp.abs(pre_w)))) - 0.5
    logd = -jnp.exp(logw)
    pre_a = a0_ref[...] + _dot(wa.astype(BF16), a2_ref[...])
    a = 1.0 / (1.0 + jnp.exp(-pre_a))
    if vres:
        vd = sh[:, 3 * w + LANES:3 * w + 2 * LANES]
        pre_v = v0_ref[...] + _dot(vd.astype(BF16), v2_ref[...])
        v = v + (vf_ref[...].astype(F32) - v) * (1.0 / (1.0 + jnp.exp(-pre_v)))
    else:
        vf_out[...] = v.astype(vf_out.dtype)
    kk = k * kk_ref[...]
    k2 = k * (1.0 + (a - 1.0) * ka_ref[...])

    ti = lax.broadcasted_iota(jnp.int32, (c, c), 0)
    tj = lax.broadcasted_iota(jnp.int32, (c, c), 1)
    tri = jnp.where(tj <= ti, 1.0, 0.0).astype(BF16)
    hi, mid, lo = _split3(logd)
    cum = _dot(tri, hi) + _dot(tri, mid) + _dot(tri, lo)
    p_inc = jnp.exp(cum)
    p_exc = jnp.exp(cum - logd)
    p_inv = jnp.exp(-cum)
    p_end = jnp.exp(cum[c - 1:c, :] - cum)
    p_all = p_inc[c - 1:c, :]

    strict = tj < ti
    incl = tj <= ti
    eye = jnp.where(tj == ti, 1.0, 0.0)
    vb = v.astype(BF16)

    for h in range(D_HEADS):
        sl = slice(h * D_HEAD, (h + 1) * D_HEAD)
        kkh = kk[:, sl]
        kkh = kkh / jnp.maximum(jnp.sqrt(jnp.sum(kkh * kkh, -1, keepdims=True)), 1e-12)
        ah = a[:, sl]
        at = (-kkh * p_exc[:, sl]).astype(BF16)
        rt = (r[:, sl] * p_inc[:, sl]).astype(BF16)
        bt = (kkh * ah * p_inv[:, sl]).astype(BF16)
        kt = (k2[:, sl] * p_inv[:, sl]).astype(BF16)
        b_end = (kkh * ah * p_end[:, sl]).astype(BF16)
        k_end = (k2[:, sl] * p_end[:, sl]).astype(BF16)
        vh = vb[:, sl]
        s0 = state_ref[h]
        s0b = s0.astype(BF16)

        a_ab = jnp.where(strict, _dot_nt(at, bt), 0.0)
        a_ak = jnp.where(strict, _dot_nt(at, kt), 0.0)
        r_b = jnp.where(incl, _dot_nt(rt, bt), 0.0)
        r_k = jnp.where(incl, _dot_nt(rt, kt), 0.0)

        xp = a_ab
        tinv = eye + xp
        for _ in range(int(math.log2(c)) - 1):
            xp = jnp.dot(xp, xp, preferred_element_type=F32, precision=lax.Precision.HIGHEST)
            tinv = tinv + jnp.dot(tinv, xp, preferred_element_type=F32, precision=lax.Precision.HIGHEST)

        rhs = _dot_nt(at, s0b) + _dot(a_ak.astype(BF16), vh)
        u = jnp.dot(tinv, rhs, preferred_element_type=F32, precision=lax.Precision.HIGHEST)
        ub = u.astype(BF16)
        y = _dot_nt(rt, s0b) + _dot(r_b.astype(BF16), ub) + _dot(r_k.astype(BF16), vh)
        state_ref[h] = s0 * p_all[:, sl] + _dot_tn(ub, b_end) + _dot_tn(vh, k_end)

        mu = jnp.mean(y, -1, keepdims=True)
        yc = y - mu
        var = jnp.mean(yc * yc, -1, keepdims=True)
        yn = yc * lax.rsqrt(var + RWKV_LN_EPS) * lw_ref[:, sl] + lb_ref[:, sl]
        bonus = jnp.sum(r[:, sl] * k2[:, sl] * rk_ref[:, sl], -1, keepdims=True) * v[:, sl]
        out_ref[:, sl] = (yn + bonus).astype(out_ref.dtype)


def _rwkv(dproj, v_first, p, b, s):
    c = D_CHUNK
    vres = v_first is not None
    row = lambda wd: pl.BlockSpec((None, c, wd), lambda bi, i: (bi, i, 0))
    full = lambda a: pl.BlockSpec(a.shape, lambda bi, i: (0, 0))
    ins = [dproj.reshape(b, s, D_PROJ_PAD)]
    in_specs = [row(D_PROJ_PAD)]
    if vres:
        ins.append(v_first.reshape(b, s, D_WIDTH))
        in_specs.append(row(D_WIDTH))
    names = ['mu', 'w0', 'w2', 'a0', 'a2'] + (['v0', 'v2'] if vres else []) + ['k_k', 'k_a', 'r_k', 'lnx_w', 'lnx_b']
    for nm in names:
        ins.append(p[nm])
        in_specs.append(full(p[nm]))
    out_shape = [jax.ShapeDtypeStruct((b, s, D_WIDTH), BF16)]
    out_specs = [row(D_WIDTH)]
    if not vres:
        out_shape.append(jax.ShapeDtypeStruct((b, s, D_WIDTH), BF16))
        out_specs.append(row(D_WIDTH))
    res = pl.pallas_call(
        functools.partial(_rwkv_kernel, vres=vres), grid=(b, s // c),
        in_specs=in_specs, out_specs=out_specs, out_shape=out_shape,
        scratch_shapes=[pltpu.VMEM((D_HEADS, D_HEAD, D_HEAD), F32), pltpu.VMEM((1, D_PROJ_PAD), F32)],
        compiler_params=_cparams("parallel", "arbitrary"), name="rwkv7",
    )(*ins)
    d_out = res[0].reshape(b * s, D_WIDTH)
    vf = v_first if vres else res[1].reshape(b * s, D_WIDTH)
    return d_out, vf


def _rope_tables(positions, dim, theta, period, base, half):
    inv = jnp.exp(-math.log(theta) * jnp.arange(0, dim, 2, dtype=F32) / dim)
    ang = positions.astype(F32)[..., None] * inv
    cos, sin = jnp.cos(ang), jnp.sin(ang)
    lane = np.arange(LANES) % period - base
    lo = (lane >= 0) & (lane < half)
    hi = (lane >= half) & (lane < 2 * half)
    idx = np.where(lo, lane, np.where(hi, lane - half, 0))
    cos_l = jnp.take(cos, idx, axis=-1)
    sin_l = jnp.take(sin, idx, axis=-1)
    c = jnp.where(lo | hi, cos_l, 1.0)
    s1 = jnp.where(hi, sin_l, 0.0)
    s2 = jnp.where(lo, -sin_l, 0.0)
    n = positions.shape[0] * positions.shape[1]
    return tuple(t.reshape(n, LANES).astype(F32) for t in (c, s1, s2))


def _even_layer(x2d, b, s, mem_kv, tabs_a, tabs_r, norm, w_in, w_out, final_gain):
    plan = [(0, 512, 0, 0, (0, ROT_DIM // 2)), (512, 512, 0, 512, (0, ROT_DIM // 2)), (1024, 512, 0, 1024, None),
            (1536, 512, 0, 1536, (1, B_QK_DIM // 2)), (2048, 512, 0, 2048, None),
            (2560, 256, 1, 0, None), (2816, 640, 2, 0, None), (3456, 640, 2, 640, None)]
    main, qm, gate = _norm_proj(x2d, norm, w_in.astype(BF16), plan, (2560, 256, MIX_WIDTH), (tabs_a, tabs_r))
    main3 = main.reshape(b, s, 2560)
    outs, lses = [], []
    for (_, dil) in A_PATTERNS:
        o, lse = _band_attn(main3, dil)
        outs.append(o)
        lses.append(lse)
    r_out = _retention(main3)
    m_out = _mem_attn(qm, mem_kv, b, s)
    return _out_proj(outs + lses, r_out, m_out, gate, x2d, w_out.astype(BF16), final_gain)


def _odd_layer(x2d, b, s, mem_kv, tabs_m, v_first, final_gain, norm, w_in, q_norm, w_qb, kv_norm, w_kvb,
               mu_shift, w0, w2, a0, a2, v0, v2, k_k, k_a, r_k, lnx_w, lnx_b, w_out):
    vres = v0 is not None
    d_cols = mu_shift.shape[0]
    k_dim = w_in.shape[0]
    z = lambda n: jnp.zeros((k_dim, n), F32)
    o = 0
    cq = w_in[:, o:o + C_Q_RANK]; o += C_Q_RANK
    ckv = w_in[:, o:o + C_KV_RANK]; o += C_KV_RANK
    kro = w_in[:, o:o + C_ROPE]; o += C_ROPE
    dp = w_in[:, o:o + d_cols]; o += d_cols
    qm_w = w_in[:, o:o + MEM_WIDTH]; o += MEM_WIDTH
    gate_w = w_in[:, o:]
    dp_pad = jnp.concatenate([dp, z(D_PROJ_PAD - d_cols)], axis=1)
    w_pad = jnp.concatenate([cq, ckv, z(C_NOPE), kro, z(LANES - C_NOPE - C_ROPE), dp_pad, qm_w, gate_w], axis=1)
    plan = [(0, 512, 0, 0, None), (512, 896, 1, 0, None), (1408, 896, 1, 896, None),
            (2304, 256, 2, 0, None), (2560, 640, 3, 0, None), (3200, 640, 3, 640, None)]
    mla_in, dproj, qm, gate = _norm_proj(x2d, norm, w_pad.astype(BF16), plan, (512, D_PROJ_PAD, 256, MIX_WIDTH))

    wq = w_qb.reshape(C_Q_RANK, C_HEADS, C_NOPE + C_ROPE)
    wq = jnp.pad(wq, ((0, 0), (0, 0), (0, LANES - C_NOPE - C_ROPE))).reshape(C_Q_RANK, C_HEADS * LANES)
    wkv = w_kvb.reshape(C_KV_RANK, C_HEADS, C_NOPE + C_V)
    wk = jnp.pad(wkv[:, :, :C_NOPE], ((0, 0), (0, 0), (0, LANES - C_NOPE))).reshape(C_KV_RANK, C_HEADS * LANES)
    wv = wkv[:, :, C_NOPE:].reshape(C_KV_RANK, C_HEADS * C_V)
    q, k, v = _mla_prep(mla_in, q_norm, kv_norm, wq.astype(BF16), wk.astype(BF16), wv.astype(BF16), tabs_m)
    c_out = _mla_attn(q, k, v, b, s)

    row = lambda t: t.reshape(1, -1).astype(F32)
    pad_rows = lambda t, top, n: jnp.concatenate(
        [jnp.zeros((top, t.shape[1]), F32), t, jnp.zeros((n - top - t.shape[0], t.shape[1]), F32)], axis=0)
    p = dict(mu=row(jnp.concatenate([mu_shift, jnp.zeros((D_PROJ_PAD - d_cols,), F32)])),
             w0=row(w0), w2=pad_rows(w2, 0, LANES).astype(BF16),
             a0=row(a0), a2=pad_rows(a2, D_DECAY_LORA, LANES).astype(BF16),
             k_k=row(k_k), k_a=row(k_a), r_k=row(r_k), lnx_w=row(lnx_w), lnx_b=row(lnx_b))
    if vres:
        p['v0'] = row(v0)
        p['v2'] = pad_rows(v2, 0, LANES).astype(BF16)
    d_out, v_first = _rwkv(dproj, v_first, p, b, s)
    m_out = _mem_attn(qm, mem_kv, b, s)
    x_new = _out_proj([c_out], d_out, m_out, gate, x2d, w_out.astype(BF16), final_gain)
    return x_new, v_first


def kernel(x, mem, positions, mem_norm, final_norm, l0_norm, l0_w_in, l0_w_mem_kv, l0_w_out, l1_norm, l1_w_in, l1_q_norm, l1_w_qb, l1_kv_norm, l1_w_kvb, l1_mu_shift, l1_w0, l1_w2, l1_a0, l1_a2, l1_k_k, l1_k_a, l1_r_k, l1_lnx_w, l1_lnx_b, l1_w_mem_kv, l1_w_out, l2_norm, l2_w_in, l2_w_mem_kv, l2_w_out, l3_norm, l3_w_in, l3_q_norm, l3_w_qb, l3_kv_norm, l3_w_kvb, l3_mu_shift, l3_w0, l3_w2, l3_a0, l3_a2, l3_v0, l3_v2, l3_k_k, l3_k_a, l3_r_k, l3_lnx_w, l3_lnx_b, l3_w_mem_kv, l3_w_out):
    b, s, d = x.shape
    x2d = x.reshape(b * s, d)
    mem2d = mem.reshape(b * MEM_TOKENS, d)
    tabs_a = _rope_tables(positions, ROT_DIM, ROPE_THETA, HEAD_DIM, 0, ROT_DIM // 2)
    tabs_r = _rope_tables(positions, B_QK_DIM, RET_THETA, B_QK_DIM, 0, B_QK_DIM // 2)
    tabs_m = _rope_tables(positions, C_ROPE, ROPE_THETA, LANES, C_NOPE, C_ROPE // 2)

    def mem_kv(w):
        plan = [(0, 2 * MEM_WIDTH, 0, 0, None)]
        return _norm_proj(mem2d, mem_norm, w.astype(BF16), plan, (2 * MEM_WIDTH,))[0]

    x2d = _even_layer(x2d, b, s, mem_kv(l0_w_mem_kv), tabs_a, tabs_r, l0_norm, l0_w_in, l0_w_out, None)
    x2d, v_first = _odd_layer(x2d, b, s, mem_kv(l1_w_mem_kv), tabs_m, None, None, l1_norm, l1_w_in, l1_q_norm,
                              l1_w_qb, l1_kv_norm, l1_w_kvb, l1_mu_shift, l1_w0, l1_w2, l1_a0, l1_a2, None, None,
                              l1_k_k, l1_k_a, l1_r_k, l1_lnx_w, l1_lnx_b, l1_w_out)
    x2d = _even_layer(x2d, b, s, mem_kv(l2_w_mem_kv), tabs_a, tabs_r, l2_norm, l2_w_in, l2_w_out, None)
    x2d, _ = _odd_layer(x2d, b, s, mem_kv(l3_w_mem_kv), tabs_m, v_first, final_norm, l3_norm, l3_w_in, l3_q_norm,
                        l3_w_qb, l3_kv_norm, l3_w_kvb, l3_mu_shift, l3_w0, l3_w2, l3_a0, l3_a2, l3_v0, l3_v2,
                        l3_k_k, l3_k_a, l3_r_k, l3_lnx_w, l3_lnx_b, l3_w_out)
    return x2d.reshape(b, s, d)
```

```python
import functools
import math

import numpy as np
import jax
import jax.numpy as jnp
from jax import lax
from jax.experimental import pallas as pl
from jax.experimental.pallas import tpu as pltpu

F32 = jnp.float32
BF16 = jnp.bfloat16

LANES = 128
VMEM_LIMIT = 48 * 1024 * 1024

D_MODEL = 1024
HEAD_DIM = 64
ROPE_THETA = 500000.0
ROT_DIM = HEAD_DIM // 4
NORM_EPS = 1e-6
NEG_INF = -1e30
LN2 = math.log(2.0)
LOG2E = 1.0 / LN2

A_HEADS = 8
A_WIDTH = A_HEADS * HEAD_DIM
A_PATTERNS = ((128, 1), (512, 4), (2048, 16))
A_BLOCK = 128

B_HEADS = 4
B_QK_DIM = 64
B_V_DIM = 128
B_CHUNK = 128
RET_THETA = 10000.0

MEM_TOKENS = 256
MEM_HEADS = 4
MEM_WIDTH = MEM_HEADS * HEAD_DIM

C_HEADS = 8
C_Q_RANK = 256
C_KV_RANK = 128
C_NOPE = 64
C_ROPE = 32
C_V = 64

D_HEADS = 8
D_HEAD = 64
D_WIDTH = D_HEADS * D_HEAD
D_DECAY_LORA = 64
D_AAA_LORA = 64
D_MV_LORA = 32
RWKV_LN_EPS = 64e-5
D_CHUNK = 64
D_PROJ_PAD = 3 * D_WIDTH + 2 * LANES

MIX_WIDTH = 1280


def _cparams(*sem):
    return pltpu.CompilerParams(dimension_semantics=sem, vmem_limit_bytes=VMEM_LIMIT)


def _dot(a, b):
    return jnp.dot(a, b, preferred_element_type=F32)


def _dot_nt(a, b):
    return lax.dot_general(a, b, (((1,), (1,)), ((), ())), preferred_element_type=F32)


def _dot_tn(a, b):
    return lax.dot_general(a, b, (((0,), (0,)), ((), ())), preferred_element_type=F32)


def _rot_slab(x, c, s1, s2, half):
    return x * c + pltpu.roll(x, half, 1) * s1 + pltpu.roll(x, LANES - half, 1) * s2


def _rotary(x, c, s1, s2, half):
    slabs = [_rot_slab(x[:, i:i + LANES], c, s1, s2, half) for i in range(0, x.shape[1], LANES)]
    return slabs[0] if len(slabs) == 1 else jnp.concatenate(slabs, axis=1)


def _norm_proj_kernel(*refs, plan, n_tab, n_out):
    x_ref, g_ref, w_ref = refs[:3]
    tabs = refs[3:3 + 3 * n_tab]
    outs = refs[3 + 3 * n_tab:3 + 3 * n_tab + n_out]
    x = x_ref[...].astype(F32)
    ms = jnp.mean(x * x, axis=-1, keepdims=True)
    xn = (x * lax.rsqrt(ms + NORM_EPS) * g_ref[...]).astype(BF16)
    for (wc, width, oi, oc, rot) in plan:
        acc = _dot(xn, w_ref[:, wc:wc + width])
        if rot is not None:
            t, half, post_scale = rot
            acc = _rotary(acc, tabs[3 * t][...], tabs[3 * t + 1][...], tabs[3 * t + 2][...], half)
            if post_scale != 1.0:
                acc = acc * post_scale
        outs[oi][:, oc:oc + width] = acc.astype(outs[oi].dtype)


def _norm_proj(x2d, g, w, plan, out_widths, tables=(), tm=256):
    t_rows, k = x2d.shape
    n = w.shape[1]
    assert t_rows % tm == 0
    flat_tabs = [t for tab in tables for t in tab]
    in_specs = [pl.BlockSpec((tm, k), lambda i: (i, 0)),
                pl.BlockSpec((1, k), lambda i: (0, 0)),
                pl.BlockSpec((k, n), lambda i: (0, 0))]
    in_specs += [pl.BlockSpec((tm, LANES), lambda i: (i, 0)) for _ in flat_tabs]
    out_specs = [pl.BlockSpec((tm, ow), lambda i: (i, 0)) for ow in out_widths]
    out_shape = [jax.ShapeDtypeStruct((t_rows, ow), BF16) for ow in out_widths]
    kern = functools.partial(_norm_proj_kernel, plan=tuple(plan), n_tab=len(tables), n_out=len(out_widths))
    return pl.pallas_call(
        kern, grid=(t_rows // tm,), in_specs=in_specs, out_specs=out_specs, out_shape=out_shape,
        compiler_params=_cparams("parallel"), name="norm_proj",
    )(x2d, g.reshape(1, k).astype(F32), w, *flat_tabs)


def _band_attn_kernel(q_ref, kp_ref, kc_ref, vp_ref, vc_ref, o_ref, lse_ref, *, nq):
    i = pl.program_id(2)
    qi = lax.broadcasted_iota(jnp.int32, (A_BLOCK, A_BLOCK), 0)
    kj = lax.broadcasted_iota(jnp.int32, (A_BLOCK, A_BLOCK), 1)
    bias_prev = jnp.where(kj >= qi, 0.0, NEG_INF)
    bias_cur = jnp.where(kj <= qi, 0.0, NEG_INF)
    bias_first = bias_prev + jnp.where(i > 0, 0.0, NEG_INF)
    lo = lax.broadcasted_iota(jnp.int32, (A_BLOCK, LANES), 1) < HEAD_DIM
    keep = (jnp.where(lo, 1.0, 0.0).astype(BF16), jnp.where(lo, 0.0, 1.0).astype(BF16))
    ones = jnp.ones((A_BLOCK, LANES), BF16)
    for u in range(nq):
        rows = slice(u * A_BLOCK, (u + 1) * A_BLOCK)
        for pair in range(A_HEADS // 2):
            sl = slice(pair * LANES, (pair + 1) * LANES)
            if u == 0:
                kp, vp, bp = kp_ref[:, sl], vp_ref[:, sl], bias_first
            else:
                prow = slice((u - 1) * A_BLOCK, u * A_BLOCK)
                kp, vp, bp = kc_ref[prow, sl], vc_ref[prow, sl], bias_prev
            kc, vc = kc_ref[rows, sl], vc_ref[rows, sl]
            q = q_ref[rows, sl]
            o, den, mx = [], [], []
            for half in range(2):
                qh = q * keep[half]
                sp = _dot_nt(qh, kp) + bp
                sc = _dot_nt(qh, kc) + bias_cur
                m = jnp.max(jnp.maximum(sp, sc), -1, keepdims=True)
                ep = jnp.exp2(sp - m).astype(BF16)
                ec = jnp.exp2(sc - m).astype(BF16)
                o.append(_dot(ep, vp) + _dot(ec, vc))
                den.append(_dot(ep, ones) + _dot(ec, ones))
                mx.append(m)
            den2 = jnp.where(lo, den[0], den[1])
            o_ref[rows, sl] = (jnp.where(lo, o[0], o[1]) / den2).astype(o_ref.dtype)
            lse_ref[rows, sl] = jnp.where(lo, mx[0], mx[1]) * LN2 + jnp.log(den2)


def _band_attn(main, dilation):
    b, s, w = main.shape
    sub_len = s // dilation
    nq = 2 if sub_len % (2 * A_BLOCK) == 0 else 1
    tq = nq * A_BLOCK
    cb = w // A_WIDTH
    view = main.reshape(b, sub_len, dilation * w)

    def cur(col):
        return pl.BlockSpec((None, tq, A_WIDTH), lambda bi, r, i: (bi, i, r * cb + col))

    def prev(col):
        return pl.BlockSpec((None, A_BLOCK, A_WIDTH),
                            lambda bi, r, i: (bi, jnp.maximum(i * nq - 1, 0), r * cb + col))

    out_spec = pl.BlockSpec((None, tq, A_WIDTH), lambda bi, r, i: (bi, i, r))
    o, lse = pl.pallas_call(
        functools.partial(_band_attn_kernel, nq=nq), grid=(b, dilation, sub_len // tq),
        in_specs=[cur(0), prev(1), cur(1), prev(2), cur(2)],
        out_specs=[out_spec, out_spec],
        out_shape=[jax.ShapeDtypeStruct((b, sub_len, dilation * A_WIDTH), BF16),
                   jax.ShapeDtypeStruct((b, sub_len, dilation * A_WIDTH), F32)],
        compiler_params=_cparams("parallel", "parallel", "parallel"), name="band_attn_d%d" % dilation,
    )(view, view, view, view, view)
    return o.reshape(b * s, A_WIDTH), lse.reshape(b * s, A_WIDTH)


def _retention_kernel(qk_ref, v_ref, o_ref, state_ref):
    c = B_CHUNK
    @pl.when(pl.program_id(1) == 0)
    def _():
        state_ref[...] = jnp.zeros_like(state_ref)

    qi = lax.broadcasted_iota(jnp.int32, (c, c), 0)
    kj = lax.broadcasted_iota(jnp.int32, (c, c), 1)
    diff = (qi - kj).astype(F32)
    row = lax.broadcasted_iota(jnp.int32, (c, 1), 0).astype(F32)
    kt_all = qk_ref[:, B_HEADS * B_QK_DIM:].astype(F32).T
    for h in range(B_HEADS):
        log_g = math.log1p(-2.0 ** (-5.0 - h))
        q = qk_ref[:, h * B_QK_DIM:(h + 1) * B_QK_DIM]
        k = qk_ref[:, (B_HEADS + h) * B_QK_DIM:(B_HEADS + h + 1) * B_QK_DIM]
        v = v_ref[:, h * B_V_DIM:(h + 1) * B_V_DIM]
        d_in = jnp.where(diff >= 0, jnp.exp(jnp.maximum(diff, 0.0) * log_g), 0.0)
        scores = _dot_nt(q, k) * (B_QK_DIM ** -0.5) * d_in
        o = _dot(scores.astype(BF16), v)
        state = state_ref[h]
        from_start = jnp.exp((row + 1.0) * log_g)
        o = o + _dot(q, state.astype(BF16)) * from_start
        to_end = jnp.exp((c - 1.0 - lax.broadcasted_iota(jnp.int32, (1, c), 1).astype(F32)) * log_g)
        kt = (kt_all[h * B_QK_DIM:(h + 1) * B_QK_DIM, :] * (to_end * B_QK_DIM ** -0.5)).astype(BF16)
        state_ref[h] = state * math.exp(c * log_g) + _dot(kt, v)
        o = o * lax.rsqrt(jnp.mean(o * o, -1, keepdims=True) + NORM_EPS)
        o_ref[:, h * B_V_DIM:(h + 1) * B_V_DIM] = o.astype(o_ref.dtype)


def _retention(main):
    b, s, _ = main.shape
    blk = (None, B_CHUNK, 512)
    o = pl.pallas_call(
        _retention_kernel, grid=(b, s // B_CHUNK),
        in_specs=[pl.BlockSpec(blk, lambda bi, i: (bi, i, 3)), pl.BlockSpec(blk, lambda bi, i: (bi, i, 4))],
        out_specs=pl.BlockSpec(blk, lambda bi, i: (bi, i, 0)),
        out_shape=jax.ShapeDtypeStruct((b, s, B_HEADS * B_V_DIM), BF16),
        scratch_shapes=[pltpu.VMEM((B_HEADS, B_QK_DIM, B_V_DIM), F32)],
        compiler_params=_cparams("parallel", "arbitrary"), name="retention",
    )(main, main)
    return o.reshape(b * s, B_HEADS * B_V_DIM)


def _mem_attn_kernel(q_ref, kv_ref, o_ref):
    scale = HEAD_DIM ** -0.5
    for h in range(MEM_HEADS):
        sl = slice(h * HEAD_DIM, (h + 1) * HEAD_DIM)
        sv = slice(MEM_WIDTH + h * HEAD_DIM, MEM_WIDTH + (h + 1) * HEAD_DIM)
        s = _dot_nt(q_ref[:, sl], kv_ref[:, sl]) * scale
        m = jnp.max(s, -1, keepdims=True)
        e = jnp.exp(s - m)
        den = jnp.sum(e, -1, keepdims=True)
        o = _dot(e.astype(BF16), kv_ref[:, sv])
        o_ref[:, sl] = (o / den).astype(o_ref.dtype)


def _mem_attn(qm, kv, b, s, tq=512):
    o = pl.pallas_call(
        _mem_attn_kernel, grid=(b, s // tq),
        in_specs=[pl.BlockSpec((None, tq, MEM_WIDTH), lambda bi, i: (bi, i, 0)),
                  pl.BlockSpec((None, MEM_TOKENS, 2 * MEM_WIDTH), lambda bi, i: (bi, 0, 0))],
        out_specs=pl.BlockSpec((None, tq, MEM_WIDTH), lambda bi, i: (bi, i, 0)),
        out_shape=jax.ShapeDtypeStruct((b, s, MEM_WIDTH), BF16),
        compiler_params=_cparams("parallel", "parallel"), name="mem_attn",
    )(qm.reshape(b, s, MEM_WIDTH), kv.reshape(b, MEM_TOKENS, 2 * MEM_WIDTH))
    return o.reshape(b * s, MEM_WIDTH)


def _out_proj_kernel(*refs, mix, final):
    if mix:
        o1, o2, o3, l1, l2, l3 = refs[:6]
        refs = refs[6:]
        m = jnp.maximum(jnp.maximum(l1[...], l2[...]), l3[...])
        e1 = jnp.exp(l1[...] - m)
        e2 = jnp.exp(l2[...] - m)
        e3 = jnp.exp(l3[...] - m)
        first = (e1 * o1[...].astype(F32) + e2 * o2[...].astype(F32) + e3 * o3[...].astype(F32)) / (e1 + e2 + e3)
    else:
        first = refs[0][...].astype(F32)
        refs = refs[1:]
    second, third, gate_ref, x_ref, w_ref = refs[:5]
    refs = refs[5:]
    if final:
        g_ref, out_ref = refs
    else:
        (out_ref,) = refs
    g = gate_ref[...].astype(F32)
    sg = g * (1.0 / (1.0 + jnp.exp(-g)))
    w1 = first.shape[1]
    w2 = w1 + second.shape[1]
    y1 = (first * sg[:, :w1]).astype(BF16)
    y2 = (second[...].astype(F32) * sg[:, w1:w2]).astype(BF16)
    y3 = (third[...].astype(F32) * sg[:, w2:]).astype(BF16)
    xn = x_ref[...] + _dot(y1, w_ref[:w1, :]) + _dot(y2, w_ref[w1:w2, :]) + _dot(y3, w_ref[w2:, :])
    if final:
        xn = xn * lax.rsqrt(jnp.mean(xn * xn, -1, keepdims=True) + NORM_EPS) * g_ref[...]
    out_ref[...] = xn


def _out_proj(firsts, second, third, gate, x2d, w_out, final_gain=None, tm=256):
    t_rows, d = x2d.shape
    mix = len(firsts) == 6
    ins = list(firsts) + [second, third, gate, x2d, w_out]
    in_specs = [pl.BlockSpec((tm, a.shape[1]), lambda i: (i, 0)) for a in ins[:-1]]
    in_specs.append(pl.BlockSpec(w_out.shape, lambda i: (0, 0)))
    if final_gain is not None:
        ins.append(final_gain.reshape(1, d).astype(F32))
        in_specs.append(pl.BlockSpec((1, d), lambda i: (0, 0)))
    kern = functools.partial(_out_proj_kernel, mix=mix, final=final_gain is not None)
    return pl.pallas_call(
        kern, grid=(t_rows // tm,), in_specs=in_specs,
        out_specs=pl.BlockSpec((tm, d), lambda i: (i, 0)),
        out_shape=jax.ShapeDtypeStruct((t_rows, d), F32),
        compiler_params=_cparams("parallel"), name="out_proj",
    )(*ins)


def _mla_prep_kernel(in_ref, qn_ref, kvn_ref, wq_ref, wk_ref, wv_ref, c_ref, s1_ref, s2_ref,
                     q_out, k_out, v_out):
    half = C_ROPE // 2
    c, s1, s2 = c_ref[...], s1_ref[...], s2_ref[...]
    cq = in_ref[:, :C_Q_RANK].astype(F32)
    ckv = in_ref[:, C_Q_RANK:C_Q_RANK + C_KV_RANK].astype(F32)
    kr = in_ref[:, C_Q_RANK + C_KV_RANK:].astype(F32)
    cqn = (cq * lax.rsqrt(jnp.mean(cq * cq, -1, keepdims=True) + NORM_EPS) * qn_ref[...]).astype(BF16)
    ckvn = (ckv * lax.rsqrt(jnp.mean(ckv * ckv, -1, keepdims=True) + NORM_EPS) * kvn_ref[...]).astype(BF16)
    q_scale = (C_NOPE + C_ROPE) ** -0.5 * LOG2E
    q_out[...] = (_rotary(_dot(cqn, wq_ref[...]), c, s1, s2, half) * q_scale).astype(q_out.dtype)
    kpe = _rot_slab(kr, c, s1, s2, half)
    kn = _dot(ckvn, wk_ref[...])
    for h in range(C_HEADS):
        k_out[:, h * LANES:(h + 1) * LANES] = (kn[:, h * LANES:(h + 1) * LANES] + kpe).astype(k_out.dtype)
    lane = lax.broadcasted_iota(jnp.int32, (1, C_HEADS * LANES), 1)
    ones = jnp.where(lane % LANES >= C_V, 1.0, 0.0)
    v_out[...] = (_dot(ckvn, wv_ref[...]) + ones).astype(v_out.dtype)


def _mla_prep(mla_in, q_norm, kv_norm, wq, wk, wv, tabs, tm=256):
    t_rows = mla_in.shape[0]
    row = lambda w: pl.BlockSpec((tm, w), lambda i: (i, 0))
    full = lambda a: pl.BlockSpec(a.shape, lambda i: (0, 0))
    qn = q_norm.reshape(1, -1).astype(F32)
    kvn = kv_norm.reshape(1, -1).astype(F32)
    return pl.pallas_call(
        _mla_prep_kernel, grid=(t_rows // tm,),
        in_specs=[row(512), full(qn), full(kvn), full(wq), full(wk), full(wv), row(LANES), row(LANES), row(LANES)],
        out_specs=[row(C_HEADS * LANES)] * 3,
        out_shape=[jax.ShapeDtypeStruct((t_rows, C_HEADS * LANES), BF16)] * 3,
        compiler_params=_cparams("parallel"), name="mla_prep",
    )(mla_in, qn, kvn, wq, wk, wv, *tabs)


def _mla_attn_kernel(q_ref, k_ref, v_ref, o_ref, m_ref, acc_ref, *, tq):
    i = pl.program_id(1)
    diag_ok = (lax.broadcasted_iota(jnp.int32, (tq, tq), 1) <= lax.broadcasted_iota(jnp.int32, (tq, tq), 0))
    m_ref[...] = jnp.full(m_ref.shape, NEG_INF, F32)
    acc_ref[...] = jnp.zeros(acc_ref.shape, F32)

    def block(j, masked):
        rows = pl.ds(pl.multiple_of(j * tq, tq), tq)
        for h in range(C_HEADS):
            sl = slice(h * LANES, (h + 1) * LANES)
            s = _dot_nt(q_ref[:, sl], k_ref[rows, sl])
            if masked:
                s = jnp.where(diag_ok, s, NEG_INF)
            m_old = m_ref[h]
            m_new = jnp.maximum(m_old, jnp.max(s, -1, keepdims=True))
            alpha = jnp.exp2(m_old - m_new)
            p = jnp.exp2(s - jnp.tile(m_new, (1, tq // LANES)))
            acc_ref[h] = alpha * acc_ref[h] + _dot(p.astype(BF16), v_ref[rows, sl])
            m_ref[h] = m_new

    def body(j, carry):
        block(j, False)
        return carry

    lax.fori_loop(0, i, body, 0)
    block(i, True)
    for h in range(C_HEADS):
        acc = acc_ref[h]
        o_ref[:, h * C_V:(h + 1) * C_V] = (acc[:, :C_V] / acc[:, C_V:]).astype(o_ref.dtype)


def _mla_attn(q, k, v, b, s, tq=256):
    qw, vw = C_HEADS * LANES, C_HEADS * C_V
    o = pl.pallas_call(
        functools.partial(_mla_attn_kernel, tq=tq), grid=(b, s // tq),
        in_specs=[pl.BlockSpec((None, tq, qw), lambda bi, i: (bi, i, 0)),
                  pl.BlockSpec((None, s, qw), lambda bi, i: (bi, 0, 0)),
                  pl.BlockSpec((None, s, qw), lambda bi, i: (bi, 0, 0))],
        out_specs=pl.BlockSpec((None, tq, vw), lambda bi, i: (bi, i, 0)),
        out_shape=jax.ShapeDtypeStruct((b, s, vw), BF16),
        scratch_shapes=[pltpu.VMEM((C_HEADS, tq, LANES), F32), pltpu.VMEM((C_HEADS, tq, LANES), F32)],
        compiler_params=_cparams("parallel", "parallel"), name="mla_attn",
    )(q.reshape(b, s, qw), k.reshape(b, s, qw), v.reshape(b, s, qw))
    return o.reshape(b * s, vw)


def _split3(x):
    hi = x.astype(BF16)
    r1 = x - hi.astype(F32)
    mid = r1.astype(BF16)
    lo = (r1 - mid.astype(F32)).astype(BF16)
    return hi, mid, lo


def _rwkv_kernel(*refs, vres):
    c = D_CHUNK
    if vres:
        (dp_ref, vf_ref, mu_ref, w0_ref, w2_ref, a0_ref, a2_ref, v0_ref, v2_ref,
         kk_ref, ka_ref, rk_ref, lw_ref, lb_ref, ones_ref, out_ref, state_ref, carry_ref, y_ref) = refs
    else:
        (dp_ref, mu_ref, w0_ref, w2_ref, a0_ref, a2_ref,
         kk_ref, ka_ref, rk_ref, lw_ref, lb_ref, ones_ref, out_ref, vf_out, state_ref, carry_ref, y_ref) = refs

    @pl.when(pl.program_id(1) == 0)
    def _():
        state_ref[...] = jnp.zeros_like(state_ref)
        carry_ref[...] = jnp.zeros_like(carry_ref)

    hcur = dp_ref[...].astype(F32)
    row = lax.broadcasted_iota(jnp.int32, (c, 1), 0)
    prev = jnp.where(row == 0, carry_ref[...], pltpu.roll(hcur, 1, 0))
    carry_ref[...] = hcur[c - 1:c, :]
    sh = hcur + (prev - hcur) * mu_ref[...]

    w = D_WIDTH
    r = sh[:, 0:w]
    k = sh[:, w:2 * w]
    v = sh[:, 2 * w:3 * w]
    wa = sh[:, 3 * w:3 * w + LANES]
    pre_w = w0_ref[...] + _dot(jnp.tanh(wa).astype(BF16), w2_ref[...])
    logw = -(jnp.maximum(-pre_w, 0.0) + jnp.log1p(jnp.exp(-jnp.abs(pre_w)))) - 0.5
    logd = -jnp.exp(logw)
    pre_a = a0_ref[...] + _dot(wa.astype(BF16), a2_ref[...])
    a = 1.0 / (1.0 + jnp.exp(-pre_a))
    if vres:
        vd = sh[:, 3 * w + LANES:3 * w + 2 * LANES]
        pre_v = v0_ref[...] + _dot(vd.astype(BF16), v2_ref[...])
        v = v + (vf_ref[...].astype(F32) - v) * (1.0 / (1.0 + jnp.exp(-pre_v)))
    else:
        vf_out[...] = v.astype(vf_out.dtype)
    def head_sum(t):
        t_hi = t.astype(BF16)
        t_lo = (t - t_hi.astype(F32)).astype(BF16)
        return _dot(t_hi, ones_ref[...]) + _dot(t_lo, ones_ref[...])

    kk = k * kk_ref[...]
    kk = kk / jnp.maximum(jnp.sqrt(head_sum(kk * kk)), 1e-12)
    k2 = k * (1.0 + (a - 1.0) * ka_ref[...])

    ti = lax.broadcasted_iota(jnp.int32, (c, c), 0)
    tj = lax.broadcasted_iota(jnp.int32, (c, c), 1)
    tri = jnp.where(tj <= ti, 1.0, 0.0).astype(BF16)
    hi, mid, lo = _split3(logd)
    cum = _dot(tri, hi) + _dot(tri, mid) + _dot(tri, lo)
    p_inc = jnp.exp(cum)
    p_exc = jnp.where(row == 0, 1.0, pltpu.roll(p_inc, 1, 0))
    p_inv = jnp.exp(-cum)
    p_all = p_inc[c - 1:c, :]
    kb = kk * a * p_inv
    kt = k2 * p_inv
    at_b = (-kk * p_exc).astype(BF16)
    rt_b = (r * p_inc).astype(BF16)
    bt_b = kb.astype(BF16)
    kt_b = kt.astype(BF16)
    bend_b = (kb * p_all).astype(BF16)
    kend_b = (kt * p_all).astype(BF16)
    vb = v.astype(BF16)

    pc = lax.broadcasted_iota(jnp.int32, (c, LANES), 1) % D_HEAD
    pr = lax.broadcasted_iota(jnp.int32, (c, LANES), 0)
    strict = pc < pr
    incl = pc <= pr
    eye = jnp.where(pc == pr, 1.0, 0.0)
    lo_lane = lax.broadcasted_iota(jnp.int32, (c, LANES), 1) < D_HEAD
    sr = lax.broadcasted_iota(jnp.int32, (LANES, LANES), 0) < D_HEAD
    sc = lax.broadcasted_iota(jnp.int32, (LANES, LANES), 1) < D_HEAD
    same_head = sr == sc

    def pair_diag(t):
        return jnp.concatenate([jnp.where(lo_lane, t, 0.0), jnp.where(lo_lane, 0.0, t)], axis=0).astype(BF16)

    pairs = range(D_HEADS // 2)
    psl = [slice(p * LANES, (p + 1) * LANES) for p in pairs]
    lhs = [jnp.concatenate([at_b[:, s], rt_b[:, s]], axis=0) for s in psl]
    s0 = [state_ref[p] for p in pairs]
    vd = [pair_diag(v[:, s]) for s in psl]
    mats = [_dot_nt(lhs[p], jnp.concatenate([pair_diag(kb[:, psl[p]]), pair_diag(kt[:, psl[p]])], axis=0))
            for p in pairs]
    a_ab = [jnp.where(strict, m[:c, :LANES], 0.0) for m in mats]
    a_ak = [jnp.where(strict, m[:c, LANES:], 0.0).astype(BF16) for m in mats]
    r_b = [jnp.where(incl, m[c:, :LANES], 0.0).astype(BF16) for m in mats]
    r_k = [jnp.where(incl, m[c:, LANES:], 0.0).astype(BF16) for m in mats]
    from_state = [_dot_nt(lhs[p], s0[p].astype(BF16)) for p in pairs]

    xp = a_ab
    tinv = [eye + x for x in xp]
    for _ in range(int(math.log2(c)) - 1):
        xp = [_dot(x.astype(BF16), pair_diag(x)) for x in xp]
        tinv = [t + _dot(t.astype(BF16), pair_diag(x)) for t, x in zip(tinv, xp)]

    rhs = [from_state[p][:c] + _dot(a_ak[p], vd[p]) for p in pairs]
    u = [_dot(tinv[p].astype(BF16), pair_diag(rhs[p])) for p in pairs]
    for p in pairs:
        y_ref[:, psl[p]] = from_state[p][c:] + _dot(r_b[p], pair_diag(u[p])) + _dot(r_k[p], vd[p])
        upd = _dot_tn(jnp.concatenate([u[p].astype(BF16), vb[:, psl[p]]], axis=0),
                      jnp.concatenate([bend_b[:, psl[p]], kend_b[:, psl[p]]], axis=0))
        state_ref[p] = s0[p] * p_all[:, psl[p]] + jnp.where(same_head, upd, 0.0)

    y = y_ref[...]
    yc = y - head_sum(y) * (1.0 / D_HEAD)
    var = head_sum(yc * yc) * (1.0 / D_HEAD)
    yn = yc * lax.rsqrt(var + RWKV_LN_EPS) * lw_ref[...] + lb_ref[...]
    bonus = head_sum(r * k2 * rk_ref[...]) * v
    out_ref[...] = (yn + bonus).astype(out_ref.dtype)


def _rwkv(dproj, v_first, p, b, s):
    c = D_CHUNK
    vres = v_first is not None
    row = lambda wd: pl.BlockSpec((None, c, wd), lambda bi, i: (bi, i, 0))
    full = lambda a: pl.BlockSpec(a.shape, lambda bi, i: (0, 0))
    ins = [dproj.reshape(b, s, D_PROJ_PAD)]
    in_specs = [row(D_PROJ_PAD)]
    if vres:
        ins.append(v_first.reshape(b, s, D_WIDTH))
        in_specs.append(row(D_WIDTH))
    names = ['mu', 'w0', 'w2', 'a0', 'a2'] + (['v0', 'v2'] if vres else []) + ['k_k', 'k_a', 'r_k', 'lnx_w', 'lnx_b']
    for nm in names:
        ins.append(p[nm])
        in_specs.append(full(p[nm]))
    head_id = np.arange(D_WIDTH) // D_HEAD
    ones_bd = jnp.asarray(head_id[:, None] == head_id[None, :], BF16)
    ins.append(ones_bd)
    in_specs.append(full(ones_bd))
    out_shape = [jax.ShapeDtypeStruct((b, s, D_WIDTH), BF16)]
    out_specs = [row(D_WIDTH)]
    if not vres:
        out_shape.append(jax.ShapeDtypeStruct((b, s, D_WIDTH), BF16))
        out_specs.append(row(D_WIDTH))
    res = pl.pallas_call(
        functools.partial(_rwkv_kernel, vres=vres), grid=(b, s // c),
        in_specs=in_specs, out_specs=out_specs, out_shape=out_shape,
        scratch_shapes=[pltpu.VMEM((D_HEADS // 2, LANES, LANES), F32),
                        pltpu.VMEM((1, D_PROJ_PAD), F32),
                        pltpu.VMEM((c, D_WIDTH), F32)],
        compiler_params=_cparams("parallel", "arbitrary"), name="rwkv7",
    )(*ins)
    d_out = res[0].reshape(b * s, D_WIDTH)
    vf = v_first if vres else res[1].reshape(b * s, D_WIDTH)
    return d_out, vf


def _rope_tables(positions, dim, theta, period, base, half):
    inv = jnp.exp(-math.log(theta) * jnp.arange(0, dim, 2, dtype=F32) / dim)
    ang = positions.astype(F32)[..., None] * inv
    cos, sin = jnp.cos(ang), jnp.sin(ang)
    lane = np.arange(LANES) % period - base
    lo = (lane >= 0) & (lane < half)
    hi = (lane >= half) & (lane < 2 * half)
    idx = np.where(lo, lane, np.where(hi, lane - half, 0))
    cos_l = jnp.take(cos, idx, axis=-1)
    sin_l = jnp.take(sin, idx, axis=-1)
    c = jnp.where(lo | hi, cos_l, 1.0)
    s1 = jnp.where(hi, sin_l, 0.0)
    s2 = jnp.where(lo, -sin_l, 0.0)
    n = positions.shape[0] * positions.shape[1]
    return tuple(t.reshape(n, LANES).astype(F32) for t in (c, s1, s2))


def _even_layer(x2d, b, s, mem_kv, tabs_a, tabs_r, norm, w_in, w_out, final_gain):
    q_scale = HEAD_DIM ** -0.5 * LOG2E
    plan = [(0, 512, 0, 0, (0, ROT_DIM // 2, q_scale)), (512, 512, 0, 512, (0, ROT_DIM // 2, 1.0)),
            (1024, 512, 0, 1024, None),
            (1536, 512, 0, 1536, (1, B_QK_DIM // 2, 1.0)), (2048, 512, 0, 2048, None),
            (2560, 256, 1, 0, None), (2816, 640, 2, 0, None), (3456, 640, 2, 640, None)]
    main, qm, gate = _norm_proj(x2d, norm, w_in.astype(BF16), plan, (2560, 256, MIX_WIDTH), (tabs_a, tabs_r))
    main3 = main.reshape(b, s, 2560)
    outs, lses = [], []
    for (_, dil) in A_PATTERNS:
        o, lse = _band_attn(main3, dil)
        outs.append(o)
        lses.append(lse)
    r_out = _retention(main3)
    m_out = _mem_attn(qm, mem_kv, b, s)
    return _out_proj(outs + lses, r_out, m_out, gate, x2d, w_out.astype(BF16), final_gain)


def _odd_layer(x2d, b, s, mem_kv, tabs_m, v_first, final_gain, norm, w_in, q_norm, w_qb, kv_norm, w_kvb,
               mu_shift, w0, w2, a0, a2, v0, v2, k_k, k_a, r_k, lnx_w, lnx_b, w_out):
    vres = v0 is not None
    d_cols = mu_shift.shape[0]
    k_dim = w_in.shape[0]
    z = lambda n: jnp.zeros((k_dim, n), F32)
    o = 0
    cq = w_in[:, o:o + C_Q_RANK]; o += C_Q_RANK
    ckv = w_in[:, o:o + C_KV_RANK]; o += C_KV_RANK
    kro = w_in[:, o:o + C_ROPE]; o += C_ROPE
    dp = w_in[:, o:o + d_cols]; o += d_cols
    qm_w = w_in[:, o:o + MEM_WIDTH]; o += MEM_WIDTH
    gate_w = w_in[:, o:]
    dp_pad = jnp.concatenate([dp, z(D_PROJ_PAD - d_cols)], axis=1)
    w_pad = jnp.concatenate([cq, ckv, z(C_NOPE), kro, z(LANES - C_NOPE - C_ROPE), dp_pad, qm_w, gate_w], axis=1)
    plan = [(0, 512, 0, 0, None), (512, 896, 1, 0, None), (1408, 896, 1, 896, None),
            (2304, 256, 2, 0, None), (2560, 640, 3, 0, None), (3200, 640, 3, 640, None)]
    mla_in, dproj, qm, gate = _norm_proj(x2d, norm, w_pad.astype(BF16), plan, (512, D_PROJ_PAD, 256, MIX_WIDTH))

    wq = w_qb.reshape(C_Q_RANK, C_HEADS, C_NOPE + C_ROPE)
    wq = jnp.pad(wq, ((0, 0), (0, 0), (0, LANES - C_NOPE - C_ROPE))).reshape(C_Q_RANK, C_HEADS * LANES)
    wkv = w_kvb.reshape(C_KV_RANK, C_HEADS, C_NOPE + C_V)
    wk = jnp.pad(wkv[:, :, :C_NOPE], ((0, 0), (0, 0), (0, LANES - C_NOPE))).reshape(C_KV_RANK, C_HEADS * LANES)
    wv = jnp.pad(wkv[:, :, C_NOPE:], ((0, 0), (0, 0), (0, LANES - C_V))).reshape(C_KV_RANK, C_HEADS * LANES)
    q, k, v = _mla_prep(mla_in, q_norm, kv_norm, wq.astype(BF16), wk.astype(BF16), wv.astype(BF16), tabs_m)
    c_out = _mla_attn(q, k, v, b, s)

    row = lambda t: t.reshape(1, -1).astype(F32)
    pad_rows = lambda t, top, n: jnp.concatenate(
        [jnp.zeros((top, t.shape[1]), F32), t, jnp.zeros((n - top - t.shape[0], t.shape[1]), F32)], axis=0)
    p = dict(mu=row(jnp.concatenate([mu_shift, jnp.zeros((D_PROJ_PAD - d_cols,), F32)])),
             w0=row(w0), w2=pad_rows(w2, 0, LANES).astype(BF16),
             a0=row(a0), a2=pad_rows(a2, D_DECAY_LORA, LANES).astype(BF16),
             k_k=row(k_k), k_a=row(k_a), r_k=row(r_k), lnx_w=row(lnx_w), lnx_b=row(lnx_b))
    if vres:
        p['v0'] = row(v0)
        p['v2'] = pad_rows(v2, 0, LANES).astype(BF16)
    d_out, v_first = _rwkv(dproj, v_first, p, b, s)
    m_out = _mem_attn(qm, mem_kv, b, s)
    x_new = _out_proj([c_out], d_out, m_out, gate, x2d, w_out.astype(BF16), final_gain)
    return x_new, v_first


def kernel(x, mem, positions, mem_norm, final_norm, l0_norm, l0_w_in, l0_w_mem_kv, l0_w_out, l1_norm, l1_w_in, l1_q_norm, l1_w_qb, l1_kv_norm, l1_w_kvb, l1_mu_shift, l1_w0, l1_w2, l1_a0, l1_a2, l1_k_k, l1_k_a, l1_r_k, l1_lnx_w, l1_lnx_b, l1_w_mem_kv, l1_w_out, l2_norm, l2_w_in, l2_w_mem_kv, l2_w_out, l3_norm, l3_w_in, l3_q_norm, l3_w_qb, l3_kv_norm, l3_w_kvb, l3_mu_shift, l3_w0, l3_w2, l3_a0, l3_a2, l3_v0, l3_v2, l3_k_k, l3_k_a, l3_r_k, l3_lnx_w, l3_lnx_b, l3_w_mem_kv, l3_w_out):
    b, s, d = x.shape
    x2d = x.reshape(b * s, d)
    mem2d = mem.reshape(b * MEM_TOKENS, d)
    tabs_a = _rope_tables(positions, ROT_DIM, ROPE_THETA, HEAD_DIM, 0, ROT_DIM // 2)
    tabs_r = _rope_tables(positions, B_QK_DIM, RET_THETA, B_QK_DIM, 0, B_QK_DIM // 2)
    tabs_m = _rope_tables(positions, C_ROPE, ROPE_THETA, LANES, C_NOPE, C_ROPE // 2)

    def mem_kv(w):
        plan = [(0, 2 * MEM_WIDTH, 0, 0, None)]
        return _norm_proj(mem2d, mem_norm, w.astype(BF16), plan, (2 * MEM_WIDTH,))[0]

    x2d = _even_layer(x2d, b, s, mem_kv(l0_w_mem_kv), tabs_a, tabs_r, l0_norm, l0_w_in, l0_w_out, None)
    x2d, v_first = _odd_layer(x2d, b, s, mem_kv(l1_w_mem_kv), tabs_m, None, None, l1_norm, l1_w_in, l1_q_norm,
                              l1_w_qb, l1_kv_norm, l1_w_kvb, l1_mu_shift, l1_w0, l1_w2, l1_a0, l1_a2, None, None,
                              l1_k_k, l1_k_a, l1_r_k, l1_lnx_w, l1_lnx_b, l1_w_out)
    x2d = _even_layer(x2d, b, s, mem_kv(l2_w_mem_kv), tabs_a, tabs_r, l2_norm, l2_w_in, l2_w_out, None)
    x2d, _ = _odd_layer(x2d, b, s, mem_kv(l3_w_mem_kv), tabs_m, v_first, final_norm, l3_norm, l3_w_in, l3_q_norm,
                        l3_w_qb, l3_kv_norm, l3_w_kvb, l3_mu_shift, l3_w0, l3_w2, l3_a0, l3_a2, l3_v0, l3_v2,
                        l3_k_k, l3_k_a, l3_r_k, l3_lnx_w, l3_lnx_b, l3_w_out)
    return x2d.reshape(b, s, d)
```

```python
import functools
import math

import numpy as np
import jax
import jax.numpy as jnp
from jax import lax
from jax.experimental import pallas as pl
from jax.experimental.pallas import tpu as pltpu

F32 = jnp.float32
BF16 = jnp.bfloat16

LANES = 128
VMEM_LIMIT = 48 * 1024 * 1024

D_MODEL = 1024
HEAD_DIM = 64
ROPE_THETA = 500000.0
ROT_DIM = HEAD_DIM // 4
NORM_EPS = 1e-6
NEG_INF = -1e30
LN2 = math.log(2.0)
LOG2E = 1.0 / LN2

A_HEADS = 8
A_WIDTH = A_HEADS * HEAD_DIM
A_PATTERNS = ((128, 1), (512, 4), (2048, 16))
A_BLOCK = 128
A_DILATIONS = tuple(d for (_, d) in A_PATTERNS)
A_LSE_GROUP = HEAD_DIM // (A_HEADS // 2)

B_HEADS = 4
B_QK_DIM = 64
B_V_DIM = 128
B_CHUNK = 128
RET_THETA = 10000.0

MEM_TOKENS = 256
MEM_HEADS = 4
MEM_WIDTH = MEM_HEADS * HEAD_DIM

C_HEADS = 8
C_Q_RANK = 256
C_KV_RANK = 128
C_NOPE = 64
C_ROPE = 32
C_V = 64

D_HEADS = 8
D_HEAD = 64
D_WIDTH = D_HEADS * D_HEAD
D_DECAY_LORA = 64
D_AAA_LORA = 64
D_MV_LORA = 32
RWKV_LN_EPS = 64e-5
D_CHUNK = 64
D_PROJ_PAD = 3 * D_WIDTH + 2 * LANES

MIX_WIDTH = 1280


def _cparams(*sem):
    return pltpu.CompilerParams(dimension_semantics=sem, vmem_limit_bytes=VMEM_LIMIT)


def _dot(a, b):
    return jnp.dot(a, b, preferred_element_type=F32)


def _dot_nt(a, b):
    return lax.dot_general(a, b, (((1,), (1,)), ((), ())), preferred_element_type=F32)


def _dot_tn(a, b):
    return lax.dot_general(a, b, (((0,), (0,)), ((), ())), preferred_element_type=F32)


def _rot_slab(x, c, s1, s2, half):
    return x * c + pltpu.roll(x, half, 1) * s1 + pltpu.roll(x, LANES - half, 1) * s2


def _rotary(x, c, s1, s2, half):
    slabs = [_rot_slab(x[:, i:i + LANES], c, s1, s2, half) for i in range(0, x.shape[1], LANES)]
    return slabs[0] if len(slabs) == 1 else jnp.concatenate(slabs, axis=1)


def _norm_proj_kernel(*refs, plan, n_tab, n_out, dils):
    x_ref, g_ref, w_ref = refs[:3]
    tabs = refs[3:3 + 3 * n_tab]
    outs = refs[3 + 3 * n_tab:3 + 3 * n_tab + n_out]
    sub_outs = refs[3 + 3 * n_tab + n_out:3 + 3 * n_tab + n_out + len(dils)]
    stage_ref = refs[-1] if dils else None
    tm = x_ref.shape[0]
    x = x_ref[...].astype(F32)
    ms = jnp.mean(x * x, axis=-1, keepdims=True)
    xn = (x * lax.rsqrt(ms + NORM_EPS) * g_ref[...]).astype(BF16)
    for (wc, width, oi, oc, rot, sub_col) in plan:
        acc = _dot(xn, w_ref[:, wc:wc + width])
        if rot is not None:
            t, half, post_scale = rot
            acc = _rotary(acc, tabs[3 * t][...], tabs[3 * t + 1][...], tabs[3 * t + 2][...], half)
            if post_scale != 1.0:
                acc = acc * post_scale
        outs[oi][:, oc:oc + width] = acc.astype(outs[oi].dtype)
        if sub_col is not None:
            for c in range(width // LANES):
                stage_ref[c] = acc[:, c * LANES:(c + 1) * LANES]
            for d, sub in zip(dils, sub_outs):
                for r in range(d):
                    for c in range(width // LANES):
                        col = sub_col + c * LANES
                        sub[r, :, col:col + LANES] = stage_ref[c, pl.ds(r, tm // d, stride=d), :].astype(sub.dtype)


def _norm_proj(x2d, g, w, plan, out_widths, tables=(), tm=256, dils=(), sub_width=0, seq=None):
    t_rows, k = x2d.shape
    n = w.shape[1]
    assert t_rows % tm == 0
    plan = tuple(tuple(p) + (None,) * (6 - len(p)) for p in plan)
    flat_tabs = [t for tab in tables for t in tab]
    in_specs = [pl.BlockSpec((tm, k), lambda i: (i, 0)),
                pl.BlockSpec((1, k), lambda i: (0, 0)),
                pl.BlockSpec((k, n), lambda i: (0, 0))]
    in_specs += [pl.BlockSpec((tm, LANES), lambda i: (i, 0)) for _ in flat_tabs]
    out_specs = [pl.BlockSpec((tm, ow), lambda i: (i, 0)) for ow in out_widths]
    out_shape = [jax.ShapeDtypeStruct((t_rows, ow), BF16) for ow in out_widths]
    scratch = []
    if dils:
        nt = seq // tm
        for d in dils:
            out_specs.append(pl.BlockSpec((None, d, tm // d, sub_width), lambda i: (i // nt, 0, i % nt, 0)))
            out_shape.append(jax.ShapeDtypeStruct((t_rows // seq, d, seq // d, sub_width), BF16))
        stage_slabs = max(p[1] for p in plan if p[5] is not None) // LANES
        scratch = [pltpu.VMEM((stage_slabs, tm, LANES), F32)]
    kern = functools.partial(_norm_proj_kernel, plan=plan, n_tab=len(tables), n_out=len(out_widths), dils=tuple(dils))
    return pl.pallas_call(
        kern, grid=(t_rows // tm,), in_specs=in_specs, out_specs=out_specs, out_shape=out_shape,
        scratch_shapes=scratch, compiler_params=_cparams("parallel"), name="norm_proj",
    )(x2d, g.reshape(1, k).astype(F32), w, *flat_tabs)


def _band_attn_kernel(q_ref, kp_ref, kc_ref, vp_ref, vc_ref, o_ref, lse_ref, s_ref, p_ref, m_ref, *, nq):
    i = pl.program_id(2)
    qi = lax.broadcasted_iota(jnp.int32, (A_BLOCK, A_BLOCK), 0)
    kj = lax.broadcasted_iota(jnp.int32, (A_BLOCK, A_BLOCK), 1)
    bias_prev = jnp.where(kj >= qi, 0.0, NEG_INF)
    bias_cur = jnp.where(kj <= qi, 0.0, NEG_INF)
    bias_first = bias_prev + jnp.where(i > 0, 0.0, NEG_INF)
    lane = lax.broadcasted_iota(jnp.int32, (A_BLOCK, LANES), 1)
    lo = lane < HEAD_DIM
    keep = (jnp.where(lo, 1.0, 0.0).astype(BF16), jnp.where(lo, 0.0, 1.0).astype(BF16))
    npair = A_HEADS // 2

    def operands(u, pair):
        sl = slice(pair * LANES, (pair + 1) * LANES)
        rows = slice(u * A_BLOCK, (u + 1) * A_BLOCK)
        if u == 0:
            return rows, sl, kp_ref[:, sl], vp_ref[:, sl], bias_first
        prow = slice((u - 1) * A_BLOCK, u * A_BLOCK)
        return rows, sl, kc_ref[prow, sl], vc_ref[prow, sl], bias_prev

    for u in range(nq):
        for pair in range(npair):
            rows, sl, kp, _, bp = operands(u, pair)
            q = q_ref[rows, sl]
            kc = kc_ref[rows, sl]
            for half in range(2):
                idx = (u * npair + pair) * 2 + half
                qh = q * keep[half]
                s_ref[idx, :, :A_BLOCK] = _dot_nt(qh, kp) + bp
                s_ref[idx, :, A_BLOCK:] = _dot_nt(qh, kc) + bias_cur
    for idx in range(nq * A_HEADS):
        s = s_ref[idx]
        m = jnp.max(s, -1, keepdims=True)
        p_ref[idx] = jnp.exp2(s - m).astype(BF16)
        m_ref[idx] = jnp.broadcast_to(m, (A_BLOCK, LANES))
    for u in range(nq):
        lse = None
        for pair in range(npair):
            rows, sl, _, vp, _ = operands(u, pair)
            vc = vc_ref[rows, sl]
            idx0 = (u * npair + pair) * 2
            o = [_dot(p_ref[idx0 + half, :, :A_BLOCK], vp * keep[half] + keep[1 - half])
                 + _dot(p_ref[idx0 + half, :, A_BLOCK:], vc * keep[half] + keep[1 - half]) for half in range(2)]
            den = pltpu.roll(jnp.where(lo, o[1], o[0]), HEAD_DIM, 1)
            o_ref[rows, sl] = (jnp.where(lo, o[0], o[1]) / den).astype(o_ref.dtype)
            lse_pair = jnp.where(lo, m_ref[idx0], m_ref[idx0 + 1]) * LN2 + jnp.log(den)
            lse = lse_pair if lse is None else jnp.where(lane % HEAD_DIM // A_LSE_GROUP == pair, lse_pair, lse)
        lse_ref[rows, :] = lse


def _band_attn(qkv, col0, dilation):
    b, d, sub_len, _ = qkv.shape
    assert d == dilation
    nq = 2 if sub_len % (2 * A_BLOCK) == 0 else 1
    tq = nq * A_BLOCK

    def cur(col):
        return pl.BlockSpec((None, None, tq, A_WIDTH), lambda bi, r, i: (bi, r, i, col0 + col))

    def prev(col):
        return pl.BlockSpec((None, None, A_BLOCK, A_WIDTH),
                            lambda bi, r, i: (bi, r, jnp.maximum(i * nq - 1, 0), col0 + col))

    return pl.pallas_call(
        functools.partial(_band_attn_kernel, nq=nq), grid=(b, d, sub_len // tq),
        in_specs=[cur(0), prev(1), cur(1), prev(2), cur(2)],
        out_specs=[pl.BlockSpec((None, None, tq, A_WIDTH), lambda bi, r, i: (bi, r, i, 0)),
                   pl.BlockSpec((None, None, tq, LANES), lambda bi, r, i: (bi, r, i, 0))],
        out_shape=[jax.ShapeDtypeStruct((b, d, sub_len, A_WIDTH), BF16),
                   jax.ShapeDtypeStruct((b, d, sub_len, LANES), F32)],
        scratch_shapes=[pltpu.VMEM((nq * A_HEADS, A_BLOCK, 2 * A_BLOCK), F32),
                        pltpu.VMEM((nq * A_HEADS, A_BLOCK, 2 * A_BLOCK), BF16),
                        pltpu.VMEM((nq * A_HEADS, A_BLOCK, LANES), F32)],
        compiler_params=_cparams("parallel", "parallel", "parallel"), name="band_attn_d%d" % dilation,
    )(qkv, qkv, qkv, qkv, qkv)


def _retention_kernel(qk_ref, v_ref, o_ref, state_ref):
    c = B_CHUNK
    @pl.when(pl.program_id(1) == 0)
    def _():
        state_ref[...] = jnp.zeros_like(state_ref)

    qi = lax.broadcasted_iota(jnp.int32, (c, c), 0)
    kj = lax.broadcasted_iota(jnp.int32, (c, c), 1)
    diff = (qi - kj).astype(F32)
    row = lax.broadcasted_iota(jnp.int32, (c, 1), 0).astype(F32)
    kt_all = qk_ref[:, B_HEADS * B_QK_DIM:].astype(F32).T
    for h in range(B_HEADS):
        log_g = math.log1p(-2.0 ** (-5.0 - h))
        q = qk_ref[:, h * B_QK_DIM:(h + 1) * B_QK_DIM]
        k = qk_ref[:, (B_HEADS + h) * B_QK_DIM:(B_HEADS + h + 1) * B_QK_DIM]
        v = v_ref[:, h * B_V_DIM:(h + 1) * B_V_DIM]
        d_in = jnp.where(diff >= 0, jnp.exp(jnp.maximum(diff, 0.0) * log_g), 0.0)
        scores = _dot_nt(q, k) * (B_QK_DIM ** -0.5) * d_in
        o = _dot(scores.astype(BF16), v)
        state = state_ref[h]
        from_start = jnp.exp((row + 1.0) * log_g)
        o = o + _dot(q, state.astype(BF16)) * from_start
        to_end = jnp.exp((c - 1.0 - lax.broadcasted_iota(jnp.int32, (1, c), 1).astype(F32)) * log_g)
        kt = (kt_all[h * B_QK_DIM:(h + 1) * B_QK_DIM, :] * (to_end * B_QK_DIM ** -0.5)).astype(BF16)
        state_ref[h] = state * math.exp(c * log_g) + _dot(kt, v)
        o = o * lax.rsqrt(jnp.mean(o * o, -1, keepdims=True) + NORM_EPS)
        o_ref[:, h * B_V_DIM:(h + 1) * B_V_DIM] = o.astype(o_ref.dtype)


def _retention(main):
    b, s, _ = main.shape
    blk = (None, B_CHUNK, 512)
    o = pl.pallas_call(
        _retention_kernel, grid=(b, s // B_CHUNK),
        in_specs=[pl.BlockSpec(blk, lambda bi, i: (bi, i, 3)), pl.BlockSpec(blk, lambda bi, i: (bi, i, 4))],
        out_specs=pl.BlockSpec(blk, lambda bi, i: (bi, i, 0)),
        out_shape=jax.ShapeDtypeStruct((b, s, B_HEADS * B_V_DIM), BF16),
        scratch_shapes=[pltpu.VMEM((B_HEADS, B_QK_DIM, B_V_DIM), F32)],
        compiler_params=_cparams("parallel", "arbitrary"), name="retention",
    )(main, main)
    return o.reshape(b * s, B_HEADS * B_V_DIM)


def _mem_attn_kernel(q_ref, kv_ref, o_ref):
    scale = HEAD_DIM ** -0.5
    for h in range(MEM_HEADS):
        sl = slice(h * HEAD_DIM, (h + 1) * HEAD_DIM)
        sv = slice(MEM_WIDTH + h * HEAD_DIM, MEM_WIDTH + (h + 1) * HEAD_DIM)
        s = _dot_nt(q_ref[:, sl], kv_ref[:, sl]) * scale
        m = jnp.max(s, -1, keepdims=True)
        e = jnp.exp(s - m)
        den = jnp.sum(e, -1, keepdims=True)
        o = _dot(e.astype(BF16), kv_ref[:, sv])
        o_ref[:, sl] = (o / den).astype(o_ref.dtype)


def _mem_attn(qm, kv, b, s, tq=512):
    o = pl.pallas_call(
        _mem_attn_kernel, grid=(b, s // tq),
        in_specs=[pl.BlockSpec((None, tq, MEM_WIDTH), lambda bi, i: (bi, i, 0)),
                  pl.BlockSpec((None, MEM_TOKENS, 2 * MEM_WIDTH), lambda bi, i: (bi, 0, 0))],
        out_specs=pl.BlockSpec((None, tq, MEM_WIDTH), lambda bi, i: (bi, i, 0)),
        out_shape=jax.ShapeDtypeStruct((b, s, MEM_WIDTH), BF16),
        compiler_params=_cparams("parallel", "parallel"), name="mem_attn",
    )(qm.reshape(b, s, MEM_WIDTH), kv.reshape(b, MEM_TOKENS, 2 * MEM_WIDTH))
    return o.reshape(b * s, MEM_WIDTH)


def _mixture(o_refs, l_refs, e_ref, stage_o, stage_l, dils, tm):
    n_slab = A_WIDTH // LANES
    lses = []
    for pi, d in enumerate(dils):
        if d == 1:
            lses.append(l_refs[pi][0])
        else:
            for r in range(d):
                stage_l[pi, pl.ds(r, tm // d, stride=d), :] = l_refs[pi][r]
                for c in range(n_slab):
                    stage_o[pi, c, pl.ds(r, tm // d, stride=d), :] = (
                        o_refs[pi][r, :, c * LANES:(c + 1) * LANES].astype(F32))
            lses.append(stage_l[pi])
    m = functools.reduce(jnp.maximum, lses)
    es = [jnp.exp(l - m) for l in lses]
    inv = 1.0 / functools.reduce(lambda a, b: a + b, es)
    first = None
    for pi, d in enumerate(dils):
        wgt = es[pi] * inv
        w_hi = wgt.astype(BF16)
        w_lo = (wgt - w_hi.astype(F32)).astype(BF16)
        w_full = _dot(w_hi, e_ref[...]) + _dot(w_lo, e_ref[...])
        if d == 1:
            o = o_refs[pi][0].astype(F32)
        else:
            o = jnp.concatenate([stage_o[pi, c] for c in range(n_slab)], axis=1)
        first = w_full * o if first is None else first + w_full * o
    return first


def _out_proj_kernel(*refs, dils, final):
    n_mix = len(dils)
    if n_mix:
        o_refs, l_refs, e_ref = refs[:n_mix], refs[n_mix:2 * n_mix], refs[2 * n_mix]
        refs = refs[2 * n_mix + 1:]
    else:
        first = refs[0][...].astype(F32)
        refs = refs[1:]
    second, third, gate_ref, x_ref, w_ref = refs[:5]
    refs = refs[5:]
    if final:
        g_ref, out_ref = refs[:2]
    else:
        out_ref = refs[0]
    if n_mix:
        stage_o, stage_l = refs[-2:]
        first = _mixture(o_refs, l_refs, e_ref, stage_o, stage_l, dils, x_ref.shape[0])
    g = gate_ref[...].astype(F32)
    sg = g * (1.0 / (1.0 + jnp.exp(-g)))
    w1 = first.shape[1]
    w2 = w1 + second.shape[1]
    y1 = (first * sg[:, :w1]).astype(BF16)
    y2 = (second[...].astype(F32) * sg[:, w1:w2]).astype(BF16)
    y3 = (third[...].astype(F32) * sg[:, w2:]).astype(BF16)
    xn = x_ref[...] + _dot(y1, w_ref[:w1, :]) + _dot(y2, w_ref[w1:w2, :]) + _dot(y3, w_ref[w2:, :])
    if final:
        xn = xn * lax.rsqrt(jnp.mean(xn * xn, -1, keepdims=True) + NORM_EPS) * g_ref[...]
    out_ref[...] = xn


def _lse_expansion():
    dst = np.arange(A_WIDTH)
    head = dst // HEAD_DIM
    src = (head % 2) * HEAD_DIM + (head // 2) * A_LSE_GROUP
    return jnp.asarray(np.arange(LANES)[:, None] == src[None, :], BF16)


def _out_proj(first, second, third, gate, x2d, w_out, final_gain=None, tm=256, seq=None):
    t_rows, dm = x2d.shape
    row = lambda a: pl.BlockSpec((tm, a.shape[1]), lambda i: (i, 0))
    full = lambda a: pl.BlockSpec(a.shape, lambda i: (0, 0))
    dils, scratch = (), []
    if isinstance(first, tuple):
        o_list, l_list = first
        dils = tuple(o.shape[1] for o in o_list)
        nt = seq // tm
        sub = lambda a: pl.BlockSpec((None, a.shape[1], tm // a.shape[1], a.shape[3]),
                                     lambda i: (i // nt, 0, i % nt, 0))
        expand = _lse_expansion()
        ins = list(o_list) + list(l_list) + [expand]
        in_specs = [sub(a) for a in ins[:-1]] + [full(expand)]
        scratch = [pltpu.VMEM((len(dils), A_WIDTH // LANES, tm, LANES), F32),
                   pltpu.VMEM((len(dils), tm, LANES), F32)]
    else:
        ins, in_specs = [first], [row(first)]
    for a in (second, third, gate, x2d):
        ins.append(a)
        in_specs.append(row(a))
    ins.append(w_out)
    in_specs.append(full(w_out))
    if final_gain is not None:
        ins.append(final_gain.reshape(1, dm).astype(F32))
        in_specs.append(pl.BlockSpec((1, dm), lambda i: (0, 0)))
    kern = functools.partial(_out_proj_kernel, dils=dils, final=final_gain is not None)
    return pl.pallas_call(
        kern, grid=(t_rows // tm,), in_specs=in_specs,
        out_specs=pl.BlockSpec((tm, dm), lambda i: (i, 0)),
        out_shape=jax.ShapeDtypeStruct((t_rows, dm), F32), scratch_shapes=scratch,
        compiler_params=_cparams("parallel"), name="out_proj",
    )(*ins)


def _mla_prep_kernel(in_ref, qn_ref, kvn_ref, wq_ref, wk_ref, wv_ref, c_ref, s1_ref, s2_ref,
                     q_out, k_out, v_out):
    half = C_ROPE // 2
    c, s1, s2 = c_ref[...], s1_ref[...], s2_ref[...]
    cq = in_ref[:, :C_Q_RANK].astype(F32)
    ckv = in_ref[:, C_Q_RANK:C_Q_RANK + C_KV_RANK].astype(F32)
    kr = in_ref[:, C_Q_RANK + C_KV_RANK:].astype(F32)
    cqn = (cq * lax.rsqrt(jnp.mean(cq * cq, -1, keepdims=True) + NORM_EPS) * qn_ref[...]).astype(BF16)
    ckvn = (ckv * lax.rsqrt(jnp.mean(ckv * ckv, -1, keepdims=True) + NORM_EPS) * kvn_ref[...]).astype(BF16)
    q_scale = (C_NOPE + C_ROPE) ** -0.5 * LOG2E
    q_out[...] = (_rotary(_dot(cqn, wq_ref[...]), c, s1, s2, half) * q_scale).astype(q_out.dtype)
    kpe = _rot_slab(kr, c, s1, s2, half)
    kn = _dot(ckvn, wk_ref[...])
    for h in range(C_HEADS):
        k_out[:, h * LANES:(h + 1) * LANES] = (kn[:, h * LANES:(h + 1) * LANES] + kpe).astype(k_out.dtype)
    lane = lax.broadcasted_iota(jnp.int32, (1, C_HEADS * LANES), 1)
    ones = jnp.where(lane % LANES >= C_V, 1.0, 0.0)
    v_out[...] = (_dot(ckvn, wv_ref[...]) + ones).astype(v_out.dtype)


def _mla_prep(mla_in, q_norm, kv_norm, wq, wk, wv, tabs, tm=256):
    t_rows = mla_in.shape[0]
    row = lambda w: pl.BlockSpec((tm, w), lambda i: (i, 0))
    full = lambda a: pl.BlockSpec(a.shape, lambda i: (0, 0))
    qn = q_norm.reshape(1, -1).astype(F32)
    kvn = kv_norm.reshape(1, -1).astype(F32)
    return pl.pallas_call(
        _mla_prep_kernel, grid=(t_rows // tm,),
        in_specs=[row(512), full(qn), full(kvn), full(wq), full(wk), full(wv), row(LANES), row(LANES), row(LANES)],
        out_specs=[row(C_HEADS * LANES)] * 3,
        out_shape=[jax.ShapeDtypeStruct((t_rows, C_HEADS * LANES), BF16)] * 3,
        compiler_params=_cparams("parallel"), name="mla_prep",
    )(mla_in, qn, kvn, wq, wk, wv, *tabs)


def _mla_attn_kernel(q_ref, k_ref, v_ref, o_ref, m_ref, acc_ref, s_ref, p_ref, alpha_ref, *, tq):
    i = pl.program_id(1)
    diag_ok = (lax.broadcasted_iota(jnp.int32, (tq, tq), 1) <= lax.broadcasted_iota(jnp.int32, (tq, tq), 0))
    m_ref[...] = jnp.full(m_ref.shape, NEG_INF, F32)
    acc_ref[...] = jnp.zeros(acc_ref.shape, F32)

    def block(j, masked):
        rows = pl.ds(pl.multiple_of(j * tq, tq), tq)
        for h in range(C_HEADS):
            sl = slice(h * LANES, (h + 1) * LANES)
            s = _dot_nt(q_ref[:, sl], k_ref[rows, sl])
            s_ref[h] = jnp.where(diag_ok, s, NEG_INF) if masked else s
        for h in range(C_HEADS):
            s = s_ref[h]
            m_old = m_ref[h]
            m_new = jnp.maximum(m_old, jnp.max(s, -1, keepdims=True))
            alpha_ref[h] = jnp.exp2(m_old - m_new)
            p_ref[h] = jnp.exp2(s - jnp.tile(m_new, (1, tq // LANES))).astype(BF16)
            m_ref[h] = m_new
        for h in range(C_HEADS):
            sl = slice(h * LANES, (h + 1) * LANES)
            acc_ref[h] = alpha_ref[h] * acc_ref[h] + _dot(p_ref[h], v_ref[rows, sl])

    def body(j, carry):
        block(j, False)
        return carry

    lax.fori_loop(0, i, body, 0)
    block(i, True)
    for h in range(C_HEADS):
        acc = acc_ref[h]
        o_ref[:, h * C_V:(h + 1) * C_V] = (acc[:, :C_V] / acc[:, C_V:]).astype(o_ref.dtype)


def _mla_attn(q, k, v, b, s, tq=256):
    qw, vw = C_HEADS * LANES, C_HEADS * C_V
    o = pl.pallas_call(
        functools.partial(_mla_attn_kernel, tq=tq), grid=(b, s // tq),
        in_specs=[pl.BlockSpec((None, tq, qw), lambda bi, i: (bi, i, 0)),
                  pl.BlockSpec((None, s, qw), lambda bi, i: (bi, 0, 0)),
                  pl.BlockSpec((None, s, qw), lambda bi, i: (bi, 0, 0))],
        out_specs=pl.BlockSpec((None, tq, vw), lambda bi, i: (bi, i, 0)),
        out_shape=jax.ShapeDtypeStruct((b, s, vw), BF16),
        scratch_shapes=[pltpu.VMEM((C_HEADS, tq, LANES), F32),
                        pltpu.VMEM((C_HEADS, tq, LANES), F32),
                        pltpu.VMEM((C_HEADS, tq, tq), F32),
                        pltpu.VMEM((C_HEADS, tq, tq), BF16),
                        pltpu.VMEM((C_HEADS, tq, LANES), F32)],
        compiler_params=_cparams("parallel", "parallel"), name="mla_attn",
    )(q.reshape(b, s, qw), k.reshape(b, s, qw), v.reshape(b, s, qw))
    return o.reshape(b * s, vw)


def _split3(x):
    hi = x.astype(BF16)
    r1 = x - hi.astype(F32)
    mid = r1.astype(BF16)
    lo = (r1 - mid.astype(F32)).astype(BF16)
    return hi, mid, lo


def _rwkv_kernel(*refs, vres):
    c = D_CHUNK
    if vres:
        (dp_ref, vf_ref, mu_ref, w0_ref, w2_ref, a0_ref, a2_ref, v0_ref, v2_ref,
         kk_ref, ka_ref, rk_ref, lw_ref, lb_ref, ones_ref, out_ref, state_ref, carry_ref, y_ref) = refs
    else:
        (dp_ref, mu_ref, w0_ref, w2_ref, a0_ref, a2_ref,
         kk_ref, ka_ref, rk_ref, lw_ref, lb_ref, ones_ref, out_ref, vf_out, state_ref, carry_ref, y_ref) = refs

    @pl.when(pl.program_id(1) == 0)
    def _():
        state_ref[...] = jnp.zeros_like(state_ref)
        carry_ref[...] = jnp.zeros_like(carry_ref)

    hcur = dp_ref[...].astype(F32)
    row = lax.broadcasted_iota(jnp.int32, (c, 1), 0)
    prev = jnp.where(row == 0, carry_ref[...], pltpu.roll(hcur, 1, 0))
    carry_ref[...] = hcur[c - 1:c, :]
    sh = hcur + (prev - hcur) * mu_ref[...]

    w = D_WIDTH
    r = sh[:, 0:w]
    k = sh[:, w:2 * w]
    v = sh[:, 2 * w:3 * w]
    wa = sh[:, 3 * w:3 * w + LANES]
    pre_w = w0_ref[...] + _dot(jnp.tanh(wa).astype(BF16), w2_ref[...])
    logw = -(jnp.maximum(-pre_w, 0.0) + jnp.log1p(jnp.exp(-jnp.abs(pre_w)))) - 0.5
    logd = -jnp.exp(logw)
    pre_a = a0_ref[...] + _dot(wa.astype(BF16), a2_ref[...])
    a = 1.0 / (1.0 + jnp.exp(-pre_a))
    if vres:
        vd = sh[:, 3 * w + LANES:3 * w + 2 * LANES]
        pre_v = v0_ref[...] + _dot(vd.astype(BF16), v2_ref[...])
        v = v + (vf_ref[...].astype(F32) - v) * (1.0 / (1.0 + jnp.exp(-pre_v)))
    else:
        vf_out[...] = v.astype(vf_out.dtype)
    def head_sum(t):
        t_hi = t.astype(BF16)
        t_lo = (t - t_hi.astype(F32)).astype(BF16)
        return _dot(t_hi, ones_ref[...]) + _dot(t_lo, ones_ref[...])

    kk = k * kk_ref[...]
    kk = kk / jnp.maximum(jnp.sqrt(head_sum(kk * kk)), 1e-12)
    k2 = k * (1.0 + (a - 1.0) * ka_ref[...])

    ti = lax.broadcasted_iota(jnp.int32, (c, c), 0)
    tj = lax.broadcasted_iota(jnp.int32, (c, c), 1)
    tri = jnp.where(tj <= ti, 1.0, 0.0).astype(BF16)
    hi, mid, lo = _split3(logd)
    cum = _dot(tri, hi) + _dot(tri, mid) + _dot(tri, lo)
    p_inc = jnp.exp(cum)
    p_exc = jnp.where(row == 0, 1.0, pltpu.roll(p_inc, 1, 0))
    p_inv = jnp.exp(-cum)
    p_all = p_inc[c - 1:c, :]
    kb = kk * a * p_inv
    kt = k2 * p_inv
    at_b = (-kk * p_exc).astype(BF16)
    rt_b = (r * p_inc).astype(BF16)
    bt_b = kb.astype(BF16)
    kt_b = kt.astype(BF16)
    bend_b = (kb * p_all).astype(BF16)
    kend_b = (kt * p_all).astype(BF16)
    vb = v.astype(BF16)

    pc = lax.broadcasted_iota(jnp.int32, (c, LANES), 1) % D_HEAD
    pr = lax.broadcasted_iota(jnp.int32, (c, LANES), 0)
    strict = pc < pr
    incl = pc <= pr
    eye = jnp.where(pc == pr, 1.0, 0.0)
    lo_lane = lax.broadcasted_iota(jnp.int32, (c, LANES), 1) < D_HEAD
    sr = lax.broadcasted_iota(jnp.int32, (LANES, LANES), 0) < D_HEAD
    sc = lax.broadcasted_iota(jnp.int32, (LANES, LANES), 1) < D_HEAD
    same_head = sr == sc

    def pair_diag(t):
        return jnp.concatenate([jnp.where(lo_lane, t, 0.0), jnp.where(lo_lane, 0.0, t)], axis=0).astype(BF16)

    pairs = range(D_HEADS // 2)
    psl = [slice(p * LANES, (p + 1) * LANES) for p in pairs]
    lhs = [jnp.concatenate([at_b[:, s], rt_b[:, s]], axis=0) for s in psl]
    s0 = [state_ref[p] for p in pairs]
    vd = [pair_diag(v[:, s]) for s in psl]
    mats = [_dot_nt(lhs[p], jnp.concatenate([pair_diag(kb[:, psl[p]]), pair_diag(kt[:, psl[p]])], axis=0))
            for p in pairs]
    a_ab = [jnp.where(strict, m[:c, :LANES], 0.0) for m in mats]
    a_ak = [jnp.where(strict, m[:c, LANES:], 0.0).astype(BF16) for m in mats]
    r_b = [jnp.where(incl, m[c:, :LANES], 0.0).astype(BF16) for m in mats]
    r_k = [jnp.where(incl, m[c:, LANES:], 0.0).astype(BF16) for m in mats]
    from_state = [_dot_nt(lhs[p], s0[p].astype(BF16)) for p in pairs]

    xp = a_ab
    tinv = [eye + x for x in xp]
    for _ in range(int(math.log2(c)) - 1):
        xp = [_dot(x.astype(BF16), pair_diag(x)) for x in xp]
        tinv = [t + _dot(t.astype(BF16), pair_diag(x)) for t, x in zip(tinv, xp)]

    rhs = [from_state[p][:c] + _dot(a_ak[p], vd[p]) for p in pairs]
    u = [_dot(tinv[p].astype(BF16), pair_diag(rhs[p])) for p in pairs]
    for p in pairs:
        y_ref[:, psl[p]] = from_state[p][c:] + _dot(r_b[p], pair_diag(u[p])) + _dot(r_k[p], vd[p])
        upd = _dot_tn(jnp.concatenate([u[p].astype(BF16), vb[:, psl[p]]], axis=0),
                      jnp.concatenate([bend_b[:, psl[p]], kend_b[:, psl[p]]], axis=0))
        state_ref[p] = s0[p] * p_all[:, psl[p]] + jnp.where(same_head, upd, 0.0)

    y = y_ref[...]
    yc = y - head_sum(y) * (1.0 / D_HEAD)
    var = head_sum(yc * yc) * (1.0 / D_HEAD)
    yn = yc * lax.rsqrt(var + RWKV_LN_EPS) * lw_ref[...] + lb_ref[...]
    bonus = head_sum(r * k2 * rk_ref[...]) * v
    out_ref[...] = (yn + bonus).astype(out_ref.dtype)


def _rwkv(dproj, v_first, p, b, s):
    c = D_CHUNK
    vres = v_first is not None
    row = lambda wd: pl.BlockSpec((None, c, wd), lambda bi, i: (bi, i, 0))
    full = lambda a: pl.BlockSpec(a.shape, lambda bi, i: (0, 0))
    ins = [dproj.reshape(b, s, D_PROJ_PAD)]
    in_specs = [row(D_PROJ_PAD)]
    if vres:
        ins.append(v_first.reshape(b, s, D_WIDTH))
        in_specs.append(row(D_WIDTH))
    names = ['mu', 'w0', 'w2', 'a0', 'a2'] + (['v0', 'v2'] if vres else []) + ['k_k', 'k_a', 'r_k', 'lnx_w', 'lnx_b']
    for nm in names:
        ins.append(p[nm])
        in_specs.append(full(p[nm]))
    head_id = np.arange(D_WIDTH) // D_HEAD
    ones_bd = jnp.asarray(head_id[:, None] == head_id[None, :], BF16)
    ins.append(ones_bd)
    in_specs.append(full(ones_bd))
    out_shape = [jax.ShapeDtypeStruct((b, s, D_WIDTH), BF16)]
    out_specs = [row(D_WIDTH)]
    if not vres:
        out_shape.append(jax.ShapeDtypeStruct((b, s, D_WIDTH), BF16))
        out_specs.append(row(D_WIDTH))
    res = pl.pallas_call(
        functools.partial(_rwkv_kernel, vres=vres), grid=(b, s // c),
        in_specs=in_specs, out_specs=out_specs, out_shape=out_shape,
        scratch_shapes=[pltpu.VMEM((D_HEADS // 2, LANES, LANES), F32),
                        pltpu.VMEM((1, D_PROJ_PAD), F32),
                        pltpu.VMEM((c, D_WIDTH), F32)],
        compiler_params=_cparams("parallel", "arbitrary"), name="rwkv7",
    )(*ins)
    d_out = res[0].reshape(b * s, D_WIDTH)
    vf = v_first if vres else res[1].reshape(b * s, D_WIDTH)
    return d_out, vf


def _rope_tables(positions, dim, theta, period, base, half):
    inv = jnp.exp(-math.log(theta) * jnp.arange(0, dim, 2, dtype=F32) / dim)
    ang = positions.astype(F32)[..., None] * inv
    cos, sin = jnp.cos(ang), jnp.sin(ang)
    lane = np.arange(LANES) % period - base
    lo = (lane >= 0) & (lane < half)
    hi = (lane >= half) & (lane < 2 * half)
    idx = np.where(lo, lane, np.where(hi, lane - half, 0))
    cos_l = jnp.take(cos, idx, axis=-1)
    sin_l = jnp.take(sin, idx, axis=-1)
    c = jnp.where(lo | hi, cos_l, 1.0)
    s1 = jnp.where(hi, sin_l, 0.0)
    s2 = jnp.where(lo, -sin_l, 0.0)
    n = positions.shape[0] * positions.shape[1]
    return tuple(t.reshape(n, LANES).astype(F32) for t in (c, s1, s2))


def _even_layer(x2d, b, s, mem_kv, tabs_a, tabs_r, norm, w_in, w_out, final_gain):
    q_scale = HEAD_DIM ** -0.5 * LOG2E
    aw = A_WIDTH
    plan = [(0, aw, 0, 0, (0, ROT_DIM // 2, q_scale), 0), (aw, aw, 0, aw, (0, ROT_DIM // 2, 1.0), aw),
            (2 * aw, aw, 0, 2 * aw, None, 2 * aw),
            (1536, 512, 0, 1536, (1, B_QK_DIM // 2, 1.0)), (2048, 512, 0, 2048, None),
            (2560, 256, 1, 0, None), (2816, 640, 2, 0, None), (3456, 640, 2, 640, None)]
    sub_dils = tuple(d for d in A_DILATIONS if d > 1)
    res = _norm_proj(x2d, norm, w_in.astype(BF16), plan, (2560, 256, MIX_WIDTH), (tabs_a, tabs_r),
                     dils=sub_dils, sub_width=3 * aw, seq=s)
    main, qm, gate = res[:3]
    main3 = main.reshape(b, s, 2560)
    qkv = {1: main3.reshape(b, 1, s, 2560)}
    qkv.update(zip(sub_dils, res[3:]))
    outs, lses = [], []
    for dil in A_DILATIONS:
        o, lse = _band_attn(qkv[dil], 0, dil)
        outs.append(o)
        lses.append(lse)
    r_out = _retention(main3)
    m_out = _mem_attn(qm, mem_kv, b, s)
    return _out_proj((outs, lses), r_out, m_out, gate, x2d, w_out.astype(BF16), final_gain, seq=s)


def _odd_layer(x2d, b, s, mem_kv, tabs_m, v_first, final_gain, norm, w_in, q_norm, w_qb, kv_norm, w_kvb,
               mu_shift, w0, w2, a0, a2, v0, v2, k_k, k_a, r_k, lnx_w, lnx_b, w_out):
    vres = v0 is not None
    d_cols = mu_shift.shape[0]
    k_dim = w_in.shape[0]
    z = lambda n: jnp.zeros((k_dim, n), F32)
    o = 0
    cq = w_in[:, o:o + C_Q_RANK]; o += C_Q_RANK
    ckv = w_in[:, o:o + C_KV_RANK]; o += C_KV_RANK
    kro = w_in[:, o:o + C_ROPE]; o += C_ROPE
    dp = w_in[:, o:o + d_cols]; o += d_cols
    qm_w = w_in[:, o:o + MEM_WIDTH]; o += MEM_WIDTH
    gate_w = w_in[:, o:]
    dp_pad = jnp.concatenate([dp, z(D_PROJ_PAD - d_cols)], axis=1)
    w_pad = jnp.concatenate([cq, ckv, z(C_NOPE), kro, z(LANES - C_NOPE - C_ROPE), dp_pad, qm_w, gate_w], axis=1)
    plan = [(0, 512, 0, 0, None), (512, 896, 1, 0, None), (1408, 896, 1, 896, None),
            (2304, 256, 2, 0, None), (2560, 640, 3, 0, None), (3200, 640, 3, 640, None)]
    mla_in, dproj, qm, gate = _norm_proj(x2d, norm, w_pad.astype(BF16), plan, (512, D_PROJ_PAD, 256, MIX_WIDTH))

    wq = w_qb.reshape(C_Q_RANK, C_HEADS, C_NOPE + C_ROPE)
    wq = jnp.pad(wq, ((0, 0), (0, 0), (0, LANES - C_NOPE - C_ROPE))).reshape(C_Q_RANK, C_HEADS * LANES)
    wkv = w_kvb.reshape(C_KV_RANK, C_HEADS, C_NOPE + C_V)
    wk = jnp.pad(wkv[:, :, :C_NOPE], ((0, 0), (0, 0), (0, LANES - C_NOPE))).reshape(C_KV_RANK, C_HEADS * LANES)
    wv = jnp.pad(wkv[:, :, C_NOPE:], ((0, 0), (0, 0), (0, LANES - C_V))).reshape(C_KV_RANK, C_HEADS * LANES)
    q, k, v = _mla_prep(mla_in, q_norm, kv_norm, wq.astype(BF16), wk.astype(BF16), wv.astype(BF16), tabs_m)
    c_out = _mla_attn(q, k, v, b, s)

    row = lambda t: t.reshape(1, -1).astype(F32)
    pad_rows = lambda t, top, n: jnp.concatenate(
        [jnp.zeros((top, t.shape[1]), F32), t, jnp.zeros((n - top - t.shape[0], t.shape[1]), F32)], axis=0)
    p = dict(mu=row(jnp.concatenate([mu_shift, jnp.zeros((D_PROJ_PAD - d_cols,), F32)])),
             w0=row(w0), w2=pad_rows(w2, 0, LANES).astype(BF16),
             a0=row(a0), a2=pad_rows(a2, D_DECAY_LORA, LANES).astype(BF16),
             k_k=row(k_k), k_a=row(k_a), r_k=row(r_k), lnx_w=row(lnx_w), lnx_b=row(lnx_b))
    if vres:
        p['v0'] = row(v0)
        p['v2'] = pad_rows(v2, 0, LANES).astype(BF16)
    d_out, v_first = _rwkv(dproj, v_first, p, b, s)
    m_out = _mem_attn(qm, mem_kv, b, s)
    x_new = _out_proj(c_out, d_out, m_out, gate, x2d, w_out.astype(BF16), final_gain)
    return x_new, v_first


def kernel(x, mem, positions, mem_norm, final_norm, l0_norm, l0_w_in, l0_w_mem_kv, l0_w_out, l1_norm, l1_w_in, l1_q_norm, l1_w_qb, l1_kv_norm, l1_w_kvb, l1_mu_shift, l1_w0, l1_w2, l1_a0, l1_a2, l1_k_k, l1_k_a, l1_r_k, l1_lnx_w, l1_lnx_b, l1_w_mem_kv, l1_w_out, l2_norm, l2_w_in, l2_w_mem_kv, l2_w_out, l3_norm, l3_w_in, l3_q_norm, l3_w_qb, l3_kv_norm, l3_w_kvb, l3_mu_shift, l3_w0, l3_w2, l3_a0, l3_a2, l3_v0, l3_v2, l3_k_k, l3_k_a, l3_r_k, l3_lnx_w, l3_lnx_b, l3_w_mem_kv, l3_w_out):
    b, s, d = x.shape
    x2d = x.reshape(b * s, d)
    mem2d = mem.reshape(b * MEM_TOKENS, d)
    tabs_a = _rope_tables(positions, ROT_DIM, ROPE_THETA, HEAD_DIM, 0, ROT_DIM // 2)
    tabs_r = _rope_tables(positions, B_QK_DIM, RET_THETA, B_QK_DIM, 0, B_QK_DIM // 2)
    tabs_m = _rope_tables(positions, C_ROPE, ROPE_THETA, LANES, C_NOPE, C_ROPE // 2)

    def mem_kv(w):
        plan = [(0, 2 * MEM_WIDTH, 0, 0, None)]
        return _norm_proj(mem2d, mem_norm, w.astype(BF16), plan, (2 * MEM_WIDTH,))[0]

    x2d = _even_layer(x2d, b, s, mem_kv(l0_w_mem_kv), tabs_a, tabs_r, l0_norm, l0_w_in, l0_w_out, None)
    x2d, v_first = _odd_layer(x2d, b, s, mem_kv(l1_w_mem_kv), tabs_m, None, None, l1_norm, l1_w_in, l1_q_norm,
                              l1_w_qb, l1_kv_norm, l1_w_kvb, l1_mu_shift, l1_w0, l1_w2, l1_a0, l1_a2, None, None,
                              l1_k_k, l1_k_a, l1_r_k, l1_lnx_w, l1_lnx_b, l1_w_out)
    x2d = _even_layer(x2d, b, s, mem_kv(l2_w_mem_kv), tabs_a, tabs_r, l2_norm, l2_w_in, l2_w_out, None)
    x2d, _ = _odd_layer(x2d, b, s, mem_kv(l3_w_mem_kv), tabs_m, v_first, final_norm, l3_norm, l3_w_in, l3_q_norm,
                        l3_w_qb, l3_kv_norm, l3_w_kvb, l3_mu_shift, l3_w0, l3_w2, l3_a0, l3_a2, l3_v0, l3_v2,
                        l3_k_k, l3_k_a, l3_r_k, l3_lnx_w, l3_lnx_b, l3_w_out)
    return x2d.reshape(b, s, d)
```

```python
import functools
import math

import numpy as np
import jax
import jax.numpy as jnp
from jax import lax
from jax.experimental import pallas as pl
from jax.experimental.pallas import tpu as pltpu

F32 = jnp.float32
BF16 = jnp.bfloat16

LANES = 128
VMEM_LIMIT = 48 * 1024 * 1024

D_MODEL = 1024
HEAD_DIM = 64
ROPE_THETA = 500000.0
ROT_DIM = HEAD_DIM // 4
NORM_EPS = 1e-6
NEG_INF = -1e30
LN2 = math.log(2.0)
LOG2E = 1.0 / LN2

A_HEADS = 8
A_WIDTH = A_HEADS * HEAD_DIM
A_PATTERNS = ((128, 1), (512, 4), (2048, 16))
A_BLOCK = 128
A_DILATIONS = tuple(d for (_, d) in A_PATTERNS)
A_LSE_GROUP = HEAD_DIM // (A_HEADS // 2)

B_HEADS = 4
B_QK_DIM = 64
B_V_DIM = 128
B_CHUNK = 128
RET_THETA = 10000.0

MEM_TOKENS = 256
MEM_HEADS = 4
MEM_WIDTH = MEM_HEADS * HEAD_DIM
MEM_Q_SCALE = HEAD_DIM ** -0.5 / math.log(2.0)

C_HEADS = 8
C_Q_RANK = 256
C_KV_RANK = 128
C_NOPE = 64
C_ROPE = 32
C_V = 64

D_HEADS = 8
D_HEAD = 64
D_WIDTH = D_HEADS * D_HEAD
D_DECAY_LORA = 64
D_AAA_LORA = 64
D_MV_LORA = 32
RWKV_LN_EPS = 64e-5
D_CHUNK = 64
D_STEP_CHUNKS = 4
D_PROJ_PAD = 3 * D_WIDTH + 2 * LANES

MIX_WIDTH = 1280


def _cparams(*sem):
    return pltpu.CompilerParams(dimension_semantics=sem, vmem_limit_bytes=VMEM_LIMIT)


def _dot(a, b):
    return jnp.dot(a, b, preferred_element_type=F32)


def _dot_nt(a, b):
    return lax.dot_general(a, b, (((1,), (1,)), ((), ())), preferred_element_type=F32)


def _dot_tn(a, b):
    return lax.dot_general(a, b, (((0,), (0,)), ((), ())), preferred_element_type=F32)


def _rot_slab(x, c, s1, s2, half):
    return x * c + pltpu.roll(x, half, 1) * s1 + pltpu.roll(x, LANES - half, 1) * s2


def _rotary(x, c, s1, s2, half):
    slabs = [_rot_slab(x[:, i:i + LANES], c, s1, s2, half) for i in range(0, x.shape[1], LANES)]
    return slabs[0] if len(slabs) == 1 else jnp.concatenate(slabs, axis=1)


def _norm_proj_kernel(*refs, plan, n_tab, n_out, dils):
    x_ref, g_ref, w_ref = refs[:3]
    tabs = refs[3:3 + 3 * n_tab]
    outs = refs[3 + 3 * n_tab:3 + 3 * n_tab + n_out]
    sub_outs = refs[3 + 3 * n_tab + n_out:3 + 3 * n_tab + n_out + len(dils)]
    stage_ref = refs[-1] if dils else None
    tm = x_ref.shape[0]
    x = x_ref[...].astype(F32)
    ms = jnp.mean(x * x, axis=-1, keepdims=True)
    xn = (x * lax.rsqrt(ms + NORM_EPS) * g_ref[...]).astype(BF16)
    for (wc, width, oi, oc, rot, sub_col) in plan:
        acc = _dot(xn, w_ref[:, wc:wc + width])
        if rot is not None:
            t, half, post_scale = rot
            if t is not None:
                acc = _rotary(acc, tabs[3 * t][...], tabs[3 * t + 1][...], tabs[3 * t + 2][...], half)
            if post_scale != 1.0:
                acc = acc * post_scale
        outs[oi][:, oc:oc + width] = acc.astype(outs[oi].dtype)
        if sub_col is not None:
            for c in range(width // LANES):
                stage_ref[c] = acc[:, c * LANES:(c + 1) * LANES]
            for d, sub in zip(dils, sub_outs):
                for r in range(d):
                    for c in range(width // LANES):
                        col = sub_col + c * LANES
                        sub[r, :, col:col + LANES] = stage_ref[c, pl.ds(r, tm // d, stride=d), :].astype(sub.dtype)


def _norm_proj(x2d, g, w, plan, out_widths, tables=(), tm=256, dils=(), sub_width=0, seq=None):
    t_rows, k = x2d.shape
    n = w.shape[1]
    assert t_rows % tm == 0
    plan = tuple(tuple(p) + (None,) * (6 - len(p)) for p in plan)
    flat_tabs = [t for tab in tables for t in tab]
    in_specs = [pl.BlockSpec((tm, k), lambda i: (i, 0)),
                pl.BlockSpec((1, k), lambda i: (0, 0)),
                pl.BlockSpec((k, n), lambda i: (0, 0))]
    in_specs += [pl.BlockSpec((tm, LANES), lambda i: (i, 0)) for _ in flat_tabs]
    out_specs = [pl.BlockSpec((tm, ow), lambda i: (i, 0)) for ow in out_widths]
    out_shape = [jax.ShapeDtypeStruct((t_rows, ow), BF16) for ow in out_widths]
    scratch = []
    if dils:
        nt = seq // tm
        for d in dils:
            out_specs.append(pl.BlockSpec((None, d, tm // d, sub_width), lambda i: (i // nt, 0, i % nt, 0)))
            out_shape.append(jax.ShapeDtypeStruct((t_rows // seq, d, seq // d, sub_width), BF16))
        stage_slabs = max(p[1] for p in plan if p[5] is not None) // LANES
        scratch = [pltpu.VMEM((stage_slabs, tm, LANES), F32)]
    kern = functools.partial(_norm_proj_kernel, plan=plan, n_tab=len(tables), n_out=len(out_widths), dils=tuple(dils))
    return pl.pallas_call(
        kern, grid=(t_rows // tm,), in_specs=in_specs, out_specs=out_specs, out_shape=out_shape,
        scratch_shapes=scratch, compiler_params=_cparams("parallel"), name="norm_proj",
    )(x2d, g.reshape(1, k).astype(F32), w, *flat_tabs)


def _band_attn_kernel(q_ref, kp_ref, kc_ref, vp_ref, vc_ref, o_ref, lse_ref, s_ref, p_ref, m_ref, *, nq):
    i = pl.program_id(2)
    qi = lax.broadcasted_iota(jnp.int32, (A_BLOCK, A_BLOCK), 0)
    kj = lax.broadcasted_iota(jnp.int32, (A_BLOCK, A_BLOCK), 1)
    bias_prev = jnp.where(kj >= qi, 0.0, NEG_INF)
    bias_cur = jnp.where(kj <= qi, 0.0, NEG_INF)
    bias_first = bias_prev + jnp.where(i > 0, 0.0, NEG_INF)
    lane = lax.broadcasted_iota(jnp.int32, (A_BLOCK, LANES), 1)
    lo = lane < HEAD_DIM
    keep = (jnp.where(lo, 1.0, 0.0).astype(BF16), jnp.where(lo, 0.0, 1.0).astype(BF16))
    npair = A_HEADS // 2

    def operands(u, pair):
        sl = slice(pair * LANES, (pair + 1) * LANES)
        rows = slice(u * A_BLOCK, (u + 1) * A_BLOCK)
        if u == 0:
            return rows, sl, kp_ref[:, sl], vp_ref[:, sl], bias_first
        prow = slice((u - 1) * A_BLOCK, u * A_BLOCK)
        return rows, sl, kc_ref[prow, sl], vc_ref[prow, sl], bias_prev

    for u in range(nq):
        for pair in range(npair):
            rows, sl, kp, _, bp = operands(u, pair)
            q = q_ref[rows, sl]
            kc = kc_ref[rows, sl]
            for half in range(2):
                idx = (u * npair + pair) * 2 + half
                qh = q * keep[half]
                s_ref[idx, :, :A_BLOCK] = _dot_nt(qh, kp) + bp
                s_ref[idx, :, A_BLOCK:] = _dot_nt(qh, kc) + bias_cur
    for idx in range(nq * A_HEADS):
        s = s_ref[idx]
        m = jnp.max(s, -1, keepdims=True)
        p_ref[idx] = jnp.exp2(s - m).astype(BF16)
        m_ref[idx] = jnp.broadcast_to(m, (A_BLOCK, LANES))
    for u in range(nq):
        lse = None
        for pair in range(npair):
            rows, sl, _, vp, _ = operands(u, pair)
            vc = vc_ref[rows, sl]
            idx0 = (u * npair + pair) * 2
            o = [_dot(p_ref[idx0 + half, :, :A_BLOCK], vp * keep[half] + keep[1 - half])
                 + _dot(p_ref[idx0 + half, :, A_BLOCK:], vc * keep[half] + keep[1 - half]) for half in range(2)]
            den = pltpu.roll(jnp.where(lo, o[1], o[0]), HEAD_DIM, 1)
            o_ref[rows, sl] = (jnp.where(lo, o[0], o[1]) / den).astype(o_ref.dtype)
            lse_pair = jnp.where(lo, m_ref[idx0], m_ref[idx0 + 1]) * LN2 + jnp.log(den)
            lse = lse_pair if lse is None else jnp.where(lane % HEAD_DIM // A_LSE_GROUP == pair, lse_pair, lse)
        lse_ref[rows, :] = lse


def _band_attn(qkv, col0, dilation):
    b, d, sub_len, _ = qkv.shape
    assert d == dilation
    nq = 2 if sub_len % (2 * A_BLOCK) == 0 else 1
    tq = nq * A_BLOCK

    def cur(col):
        return pl.BlockSpec((None, None, tq, A_WIDTH), lambda bi, r, i: (bi, r, i, col0 + col))

    def prev(col):
        return pl.BlockSpec((None, None, A_BLOCK, A_WIDTH),
                            lambda bi, r, i: (bi, r, jnp.maximum(i * nq - 1, 0), col0 + col))

    return pl.pallas_call(
        functools.partial(_band_attn_kernel, nq=nq), grid=(b, d, sub_len // tq),
        in_specs=[cur(0), prev(1), cur(1), prev(2), cur(2)],
        out_specs=[pl.BlockSpec((None, None, tq, A_WIDTH), lambda bi, r, i: (bi, r, i, 0)),
                   pl.BlockSpec((None, None, tq, LANES), lambda bi, r, i: (bi, r, i, 0))],
        out_shape=[jax.ShapeDtypeStruct((b, d, sub_len, A_WIDTH), BF16),
                   jax.ShapeDtypeStruct((b, d, sub_len, LANES), F32)],
        scratch_shapes=[pltpu.VMEM((nq * A_HEADS, A_BLOCK, 2 * A_BLOCK), F32),
                        pltpu.VMEM((nq * A_HEADS, A_BLOCK, 2 * A_BLOCK), BF16),
                        pltpu.VMEM((nq * A_HEADS, A_BLOCK, LANES), F32)],
        compiler_params=_cparams("parallel", "parallel", "parallel"), name="band_attn_d%d" % dilation,
    )(qkv, qkv, qkv, qkv, qkv)


def _retention_kernel(qk_ref, v_ref, o_ref, state_ref):
    c = B_CHUNK
    @pl.when(pl.program_id(1) == 0)
    def _():
        state_ref[...] = jnp.zeros_like(state_ref)

    qi = lax.broadcasted_iota(jnp.int32, (c, c), 0)
    kj = lax.broadcasted_iota(jnp.int32, (c, c), 1)
    diff = (qi - kj).astype(F32)
    row = lax.broadcasted_iota(jnp.int32, (c, 1), 0).astype(F32)
    kt_all = qk_ref[:, B_HEADS * B_QK_DIM:].astype(F32).T
    for h in range(B_HEADS):
        log_g = math.log1p(-2.0 ** (-5.0 - h))
        q = qk_ref[:, h * B_QK_DIM:(h + 1) * B_QK_DIM]
        k = qk_ref[:, (B_HEADS + h) * B_QK_DIM:(B_HEADS + h + 1) * B_QK_DIM]
        v = v_ref[:, h * B_V_DIM:(h + 1) * B_V_DIM]
        d_in = jnp.where(diff >= 0, jnp.exp(jnp.maximum(diff, 0.0) * log_g), 0.0)
        scores = _dot_nt(q, k) * (B_QK_DIM ** -0.5) * d_in
        o = _dot(scores.astype(BF16), v)
        state = state_ref[h]
        from_start = jnp.exp((row + 1.0) * log_g)
        o = o + _dot(q, state.astype(BF16)) * from_start
        to_end = jnp.exp((c - 1.0 - lax.broadcasted_iota(jnp.int32, (1, c), 1).astype(F32)) * log_g)
        kt = (kt_all[h * B_QK_DIM:(h + 1) * B_QK_DIM, :] * (to_end * B_QK_DIM ** -0.5)).astype(BF16)
        state_ref[h] = state * math.exp(c * log_g) + _dot(kt, v)
        o = o * lax.rsqrt(jnp.mean(o * o, -1, keepdims=True) + NORM_EPS)
        o_ref[:, h * B_V_DIM:(h + 1) * B_V_DIM] = o.astype(o_ref.dtype)


def _retention(main):
    b, s, _ = main.shape
    blk = (None, B_CHUNK, 512)
    o = pl.pallas_call(
        _retention_kernel, grid=(b, s // B_CHUNK),
        in_specs=[pl.BlockSpec(blk, lambda bi, i: (bi, i, 3)), pl.BlockSpec(blk, lambda bi, i: (bi, i, 4))],
        out_specs=pl.BlockSpec(blk, lambda bi, i: (bi, i, 0)),
        out_shape=jax.ShapeDtypeStruct((b, s, B_HEADS * B_V_DIM), BF16),
        scratch_shapes=[pltpu.VMEM((B_HEADS, B_QK_DIM, B_V_DIM), F32)],
        compiler_params=_cparams("parallel", "arbitrary"), name="retention",
    )(main, main)
    return o.reshape(b * s, B_HEADS * B_V_DIM)


def _mem_attn_kernel(q_ref, kv_ref, o_ref, s_ref, p_ref):
    tq = q_ref.shape[0]
    lo_q = lax.broadcasted_iota(jnp.int32, (tq, LANES), 1) < HEAD_DIM
    lo_k = lax.broadcasted_iota(jnp.int32, (MEM_TOKENS, LANES), 1) < HEAD_DIM
    keep_q = (jnp.where(lo_q, 1.0, 0.0).astype(BF16), jnp.where(lo_q, 0.0, 1.0).astype(BF16))
    keep_k = (jnp.where(lo_k, 1.0, 0.0).astype(BF16), jnp.where(lo_k, 0.0, 1.0).astype(BF16))
    npair = MEM_HEADS // 2
    for pair in range(npair):
        sl = slice(pair * LANES, (pair + 1) * LANES)
        for half in range(2):
            s_ref[2 * pair + half] = _dot_nt(q_ref[:, sl] * keep_q[half], kv_ref[:, sl])
    for idx in range(MEM_HEADS):
        s = s_ref[idx]
        p_ref[idx] = jnp.exp2(s - jnp.max(s, -1, keepdims=True)).astype(BF16)
    for pair in range(npair):
        sl = slice(pair * LANES, (pair + 1) * LANES)
        v = kv_ref[:, MEM_WIDTH + pair * LANES:MEM_WIDTH + (pair + 1) * LANES]
        o = [_dot(p_ref[2 * pair + half], v * keep_k[half] + keep_k[1 - half]) for half in range(2)]
        den = pltpu.roll(jnp.where(lo_q, o[1], o[0]), HEAD_DIM, 1)
        o_ref[:, sl] = (jnp.where(lo_q, o[0], o[1]) / den).astype(o_ref.dtype)


def _mem_attn(qm, kv, b, s, tq=512):
    o = pl.pallas_call(
        _mem_attn_kernel, grid=(b, s // tq),
        in_specs=[pl.BlockSpec((None, tq, MEM_WIDTH), lambda bi, i: (bi, i, 0)),
                  pl.BlockSpec((None, MEM_TOKENS, 2 * MEM_WIDTH), lambda bi, i: (bi, 0, 0))],
        out_specs=pl.BlockSpec((None, tq, MEM_WIDTH), lambda bi, i: (bi, i, 0)),
        out_shape=jax.ShapeDtypeStruct((b, s, MEM_WIDTH), BF16),
        scratch_shapes=[pltpu.VMEM((MEM_HEADS, tq, MEM_TOKENS), F32),
                        pltpu.VMEM((MEM_HEADS, tq, MEM_TOKENS), BF16)],
        compiler_params=_cparams("parallel", "parallel"), name="mem_attn",
    )(qm.reshape(b, s, MEM_WIDTH), kv.reshape(b, MEM_TOKENS, 2 * MEM_WIDTH))
    return o.reshape(b * s, MEM_WIDTH)


def _mixture(o_refs, l_refs, e_ref, stage_o, stage_l, dils, tm):
    n_slab = A_WIDTH // LANES
    lses = []
    for pi, d in enumerate(dils):
        if d == 1:
            lses.append(l_refs[pi][0])
        else:
            for r in range(d):
                stage_l[pi, pl.ds(r, tm // d, stride=d), :] = l_refs[pi][r]
                for c in range(n_slab):
                    stage_o[pi, c, pl.ds(r, tm // d, stride=d), :] = (
                        o_refs[pi][r, :, c * LANES:(c + 1) * LANES].astype(F32))
            lses.append(stage_l[pi])
    m = functools.reduce(jnp.maximum, lses)
    es = [jnp.exp(l - m) for l in lses]
    inv = 1.0 / functools.reduce(lambda a, b: a + b, es)
    first = None
    for pi, d in enumerate(dils):
        wgt = es[pi] * inv
        w_hi = wgt.astype(BF16)
        w_lo = (wgt - w_hi.astype(F32)).astype(BF16)
        w_full = _dot(w_hi, e_ref[...]) + _dot(w_lo, e_ref[...])
        if d == 1:
            o = o_refs[pi][0].astype(F32)
        else:
            o = jnp.concatenate([stage_o[pi, c] for c in range(n_slab)], axis=1)
        first = w_full * o if first is None else first + w_full * o
    return first


def _out_proj_kernel(*refs, dils, final):
    n_mix = len(dils)
    if n_mix:
        o_refs, l_refs, e_ref = refs[:n_mix], refs[n_mix:2 * n_mix], refs[2 * n_mix]
        refs = refs[2 * n_mix + 1:]
    else:
        first = refs[0][...].astype(F32)
        refs = refs[1:]
    second, third, gate_ref, x_ref, w_ref = refs[:5]
    refs = refs[5:]
    if final:
        g_ref, out_ref = refs[:2]
    else:
        out_ref = refs[0]
    if n_mix:
        stage_o, stage_l = refs[-2:]
        first = _mixture(o_refs, l_refs, e_ref, stage_o, stage_l, dils, x_ref.shape[0])
    g = gate_ref[...].astype(F32)
    sg = g * (1.0 / (1.0 + jnp.exp(-g)))
    w1 = first.shape[1]
    w2 = w1 + second.shape[1]
    y1 = (first * sg[:, :w1]).astype(BF16)
    y2 = (second[...].astype(F32) * sg[:, w1:w2]).astype(BF16)
    y3 = (third[...].astype(F32) * sg[:, w2:]).astype(BF16)
    xn = x_ref[...] + _dot(y1, w_ref[:w1, :]) + _dot(y2, w_ref[w1:w2, :]) + _dot(y3, w_ref[w2:, :])
    if final:
        xn = xn * lax.rsqrt(jnp.mean(xn * xn, -1, keepdims=True) + NORM_EPS) * g_ref[...]
    out_ref[...] = xn


def _lse_expansion():
    dst = np.arange(A_WIDTH)
    head = dst // HEAD_DIM
    src = (head % 2) * HEAD_DIM + (head // 2) * A_LSE_GROUP
    return jnp.asarray(np.arange(LANES)[:, None] == src[None, :], BF16)


def _out_proj(first, second, third, gate, x2d, w_out, final_gain=None, tm=256, seq=None):
    t_rows, dm = x2d.shape
    row = lambda a: pl.BlockSpec((tm, a.shape[1]), lambda i: (i, 0))
    full = lambda a: pl.BlockSpec(a.shape, lambda i: (0, 0))
    dils, scratch = (), []
    if isinstance(first, tuple):
        o_list, l_list = first
        dils = tuple(o.shape[1] for o in o_list)
        nt = seq // tm
        sub = lambda a: pl.BlockSpec((None, a.shape[1], tm // a.shape[1], a.shape[3]),
                                     lambda i: (i // nt, 0, i % nt, 0))
        expand = _lse_expansion()
        ins = list(o_list) + list(l_list) + [expand]
        in_specs = [sub(a) for a in ins[:-1]] + [full(expand)]
        scratch = [pltpu.VMEM((len(dils), A_WIDTH // LANES, tm, LANES), F32),
                   pltpu.VMEM((len(dils), tm, LANES), F32)]
    else:
        ins, in_specs = [first], [row(first)]
    for a in (second, third, gate, x2d):
        ins.append(a)
        in_specs.append(row(a))
    ins.append(w_out)
    in_specs.append(full(w_out))
    if final_gain is not None:
        ins.append(final_gain.reshape(1, dm).astype(F32))
        in_specs.append(pl.BlockSpec((1, dm), lambda i: (0, 0)))
    kern = functools.partial(_out_proj_kernel, dils=dils, final=final_gain is not None)
    return pl.pallas_call(
        kern, grid=(t_rows // tm,), in_specs=in_specs,
        out_specs=pl.BlockSpec((tm, dm), lambda i: (i, 0)),
        out_shape=jax.ShapeDtypeStruct((t_rows, dm), F32), scratch_shapes=scratch,
        compiler_params=_cparams("parallel"), name="out_proj",
    )(*ins)


def _mla_prep_kernel(in_ref, qn_ref, kvn_ref, wq_ref, wk_ref, wv_ref, c_ref, s1_ref, s2_ref,
                     q_out, k_out, v_out):
    half = C_ROPE // 2
    c, s1, s2 = c_ref[...], s1_ref[...], s2_ref[...]
    cq = in_ref[:, :C_Q_RANK].astype(F32)
    ckv = in_ref[:, C_Q_RANK:C_Q_RANK + C_KV_RANK].astype(F32)
    kr = in_ref[:, C_Q_RANK + C_KV_RANK:].astype(F32)
    cqn = (cq * lax.rsqrt(jnp.mean(cq * cq, -1, keepdims=True) + NORM_EPS) * qn_ref[...]).astype(BF16)
    ckvn = (ckv * lax.rsqrt(jnp.mean(ckv * ckv, -1, keepdims=True) + NORM_EPS) * kvn_ref[...]).astype(BF16)
    q_scale = (C_NOPE + C_ROPE) ** -0.5 * LOG2E
    q_out[...] = (_rotary(_dot(cqn, wq_ref[...]), c, s1, s2, half) * q_scale).astype(q_out.dtype)
    kpe = _rot_slab(kr, c, s1, s2, half)
    kn = _dot(ckvn, wk_ref[...])
    for h in range(C_HEADS):
        k_out[:, h * LANES:(h + 1) * LANES] = (kn[:, h * LANES:(h + 1) * LANES] + kpe).astype(k_out.dtype)
    lane = lax.broadcasted_iota(jnp.int32, (1, C_HEADS * LANES), 1)
    ones = jnp.where(lane % LANES >= C_V, 1.0, 0.0)
    v_out[...] = (_dot(ckvn, wv_ref[...]) + ones).astype(v_out.dtype)


def _mla_prep(mla_in, q_norm, kv_norm, wq, wk, wv, tabs, tm=256):
    t_rows = mla_in.shape[0]
    row = lambda w: pl.BlockSpec((tm, w), lambda i: (i, 0))
    full = lambda a: pl.BlockSpec(a.shape, lambda i: (0, 0))
    qn = q_norm.reshape(1, -1).astype(F32)
    kvn = kv_norm.reshape(1, -1).astype(F32)
    return pl.pallas_call(
        _mla_prep_kernel, grid=(t_rows // tm,),
        in_specs=[row(512), full(qn), full(kvn), full(wq), full(wk), full(wv), row(LANES), row(LANES), row(LANES)],
        out_specs=[row(C_HEADS * LANES)] * 3,
        out_shape=[jax.ShapeDtypeStruct((t_rows, C_HEADS * LANES), BF16)] * 3,
        compiler_params=_cparams("parallel"), name="mla_prep",
    )(mla_in, qn, kvn, wq, wk, wv, *tabs)


def _mla_attn_kernel(q_ref, k_ref, v_ref, o_ref, m_ref, acc_ref, s_ref, p_ref, alpha_ref, *, tq):
    i = pl.program_id(1)
    diag_ok = (lax.broadcasted_iota(jnp.int32, (tq, tq), 1) <= lax.broadcasted_iota(jnp.int32, (tq, tq), 0))
    m_ref[...] = jnp.full(m_ref.shape, NEG_INF, F32)
    acc_ref[...] = jnp.zeros(acc_ref.shape, F32)

    def block(j, masked):
        rows = pl.ds(pl.multiple_of(j * tq, tq), tq)
        for h in range(C_HEADS):
            sl = slice(h * LANES, (h + 1) * LANES)
            s = _dot_nt(q_ref[:, sl], k_ref[rows, sl])
            s_ref[h] = jnp.where(diag_ok, s, NEG_INF) if masked else s
        for h in range(C_HEADS):
            s = s_ref[h]
            m_old = m_ref[h]
            m_new = jnp.maximum(m_old, jnp.max(s, -1, keepdims=True))
            alpha_ref[h] = jnp.exp2(m_old - m_new)
            p_ref[h] = jnp.exp2(s - jnp.tile(m_new, (1, tq // LANES))).astype(BF16)
            m_ref[h] = m_new
        for h in range(C_HEADS):
            sl = slice(h * LANES, (h + 1) * LANES)
            acc_ref[h] = alpha_ref[h] * acc_ref[h] + _dot(p_ref[h], v_ref[rows, sl])

    def body(j, carry):
        block(j, False)
        return carry

    lax.fori_loop(0, i, body, 0)
    block(i, True)
    for h in range(C_HEADS):
        acc = acc_ref[h]
        o_ref[:, h * C_V:(h + 1) * C_V] = (acc[:, :C_V] / acc[:, C_V:]).astype(o_ref.dtype)


def _mla_attn(q, k, v, b, s, tq=256):
    qw, vw = C_HEADS * LANES, C_HEADS * C_V
    o = pl.pallas_call(
        functools.partial(_mla_attn_kernel, tq=tq), grid=(b, s // tq),
        in_specs=[pl.BlockSpec((None, tq, qw), lambda bi, i: (bi, i, 0)),
                  pl.BlockSpec((None, s, qw), lambda bi, i: (bi, 0, 0)),
                  pl.BlockSpec((None, s, qw), lambda bi, i: (bi, 0, 0))],
        out_specs=pl.BlockSpec((None, tq, vw), lambda bi, i: (bi, i, 0)),
        out_shape=jax.ShapeDtypeStruct((b, s, vw), BF16),
        scratch_shapes=[pltpu.VMEM((C_HEADS, tq, LANES), F32),
                        pltpu.VMEM((C_HEADS, tq, LANES), F32),
                        pltpu.VMEM((C_HEADS, tq, tq), F32),
                        pltpu.VMEM((C_HEADS, tq, tq), BF16),
                        pltpu.VMEM((C_HEADS, tq, LANES), F32)],
        compiler_params=_cparams("parallel", "parallel"), name="mla_attn",
    )(q.reshape(b, s, qw), k.reshape(b, s, qw), v.reshape(b, s, qw))
    return o.reshape(b * s, vw)


def _split3(x):
    hi = x.astype(BF16)
    r1 = x - hi.astype(F32)
    mid = r1.astype(BF16)
    lo = (r1 - mid.astype(F32)).astype(BF16)
    return hi, mid, lo


def _rwkv_kernel(*refs, vres, nchunk):
    c = D_CHUNK
    tr = nchunk * c
    if vres:
        (dp_ref, vf_ref, mu_ref, w0_ref, w2_ref, a0_ref, a2_ref, v0_ref, v2_ref,
         kk_ref, ka_ref, rk_ref, lw_ref, lb_ref, ones_ref, out_ref,
         state_ref, carry_ref, y_ref) = refs
    else:
        (dp_ref, mu_ref, w0_ref, w2_ref, a0_ref, a2_ref,
         kk_ref, ka_ref, rk_ref, lw_ref, lb_ref, ones_ref, out_ref, vf_out,
         state_ref, carry_ref, y_ref) = refs

    @pl.when(pl.program_id(1) == 0)
    def _():
        state_ref[...] = jnp.zeros_like(state_ref)
        carry_ref[...] = jnp.zeros_like(carry_ref)

    hcur = dp_ref[...].astype(F32)
    row = lax.broadcasted_iota(jnp.int32, (tr, 1), 0)
    prev = jnp.where(row == 0, carry_ref[...], pltpu.roll(hcur, 1, 0))
    carry_ref[...] = hcur[tr - 1:tr, :]
    sh = hcur + (prev - hcur) * mu_ref[...]

    w = D_WIDTH
    r = sh[:, 0:w]
    k = sh[:, w:2 * w]
    v = sh[:, 2 * w:3 * w]
    wa = sh[:, 3 * w:3 * w + LANES]
    pre_w = w0_ref[...] + _dot(jnp.tanh(wa).astype(BF16), w2_ref[...])
    logd = -math.exp(-0.5) / (1.0 + jnp.exp(-pre_w))
    pre_a = a0_ref[...] + _dot(wa.astype(BF16), a2_ref[...])
    a = 1.0 / (1.0 + jnp.exp(-pre_a))
    if vres:
        vd = sh[:, 3 * w + LANES:3 * w + 2 * LANES]
        pre_v = v0_ref[...] + _dot(vd.astype(BF16), v2_ref[...])
        v = v + (vf_ref[...].astype(F32) - v) * (1.0 / (1.0 + jnp.exp(-pre_v)))
    else:
        vf_out[...] = v.astype(vf_out.dtype)
    def head_sum(t):
        t_hi = t.astype(BF16)
        t_lo = (t - t_hi.astype(F32)).astype(BF16)
        n = t.shape[1] // LANES
        stack = jnp.concatenate([piece[:, i * LANES:(i + 1) * LANES] for i in range(n) for piece in (t_hi, t_lo)], axis=0)
        res = _dot(stack, ones_ref[...])
        rows = t.shape[0]
        return jnp.concatenate([res[2 * i * rows:(2 * i + 1) * rows] + res[(2 * i + 1) * rows:(2 * i + 2) * rows]
                                for i in range(n)], axis=1)

    kk = k * kk_ref[...]
    kk = kk / jnp.maximum(jnp.sqrt(head_sum(kk * kk)), 1e-12)
    k2 = k * (1.0 + (a - 1.0) * ka_ref[...])

    ti = lax.broadcasted_iota(jnp.int32, (tr, tr), 0)
    tj = lax.broadcasted_iota(jnp.int32, (tr, tr), 1)
    tri = jnp.where((tj <= ti) & (tj >= ti // c * c), 1.0, 0.0).astype(BF16)
    hi, mid, lo = _split3(logd)
    cum = _dot(tri, hi) + _dot(tri, mid) + _dot(tri, lo)
    p_inc = jnp.exp(cum)
    p_exc = jnp.where(row % c == 0, 1.0, pltpu.roll(p_inc, 1, 0))
    p_inv = jnp.exp(-cum)
    p_all = [p_inc[(ci + 1) * c - 1:(ci + 1) * c, :] for ci in range(nchunk)]
    p_all_rows = jnp.concatenate([jnp.broadcast_to(pa, (c, D_WIDTH)) for pa in p_all], axis=0)
    kb = kk * a * p_inv
    kt = k2 * p_inv
    at_b = (-kk * p_exc).astype(BF16)
    rt_b = (r * p_inc).astype(BF16)
    bend_b = (kb * p_all_rows).astype(BF16)
    kend_b = (kt * p_all_rows).astype(BF16)
    vb = v.astype(BF16)

    pc = lax.broadcasted_iota(jnp.int32, (c, LANES), 1) % D_HEAD
    pr = lax.broadcasted_iota(jnp.int32, (c, LANES), 0)
    strict = pc < pr
    incl = pc <= pr
    eye = jnp.where(pc == pr, 1.0, 0.0)
    lo_lane = lax.broadcasted_iota(jnp.int32, (c, LANES), 1) < D_HEAD
    sr = lax.broadcasted_iota(jnp.int32, (LANES, LANES), 0) < D_HEAD
    sc = lax.broadcasted_iota(jnp.int32, (LANES, LANES), 1) < D_HEAD
    same_head = sr == sc

    def pair_diag(t):
        return jnp.concatenate([jnp.where(lo_lane, t, 0.0), jnp.where(lo_lane, 0.0, t)], axis=0).astype(BF16)

    items = [(ci, p) for ci in range(nchunk) for p in range(D_HEADS // 2)]
    rsl = lambda ci: slice(ci * c, (ci + 1) * c)
    psl = lambda p: slice(p * LANES, (p + 1) * LANES)
    sub = lambda arr, it: arr[rsl(it[0]), psl(it[1])]
    lhs = {it: jnp.concatenate([sub(at_b, it), sub(rt_b, it)], axis=0) for it in items}
    mats = {it: _dot_nt(lhs[it], jnp.concatenate([pair_diag(sub(kb, it)), pair_diag(sub(kt, it))], axis=0))
            for it in items}
    r_b = {it: jnp.where(incl, mats[it][c:, :LANES], 0.0).astype(BF16) for it in items}
    x0 = {it: jnp.where(strict, mats[it][:c, :LANES], 0.0) for it in items}
    xp = {it: _dot(x0[it].astype(BF16), pair_diag(x0[it])) for it in items}
    tinv = {it: eye + x0[it] for it in items}
    n_levels = int(math.log2(c))
    for lvl in range(1, n_levels - 1):
        prod = {it: _dot(jnp.concatenate([xp[it], tinv[it]], axis=0).astype(BF16), pair_diag(xp[it])) for it in items}
        xp = {it: prod[it][:c] for it in items}
        tinv = {it: tinv[it] + prod[it][c:] for it in items}
    tinv = {it: (tinv[it] + _dot(tinv[it].astype(BF16), pair_diag(xp[it]))).astype(BF16) for it in items}
    m_k = {it: jnp.concatenate([jnp.where(strict, mats[it][:c, LANES:], 0.0),
                                jnp.where(incl, mats[it][c:, LANES:], 0.0)], axis=0).astype(BF16) for it in items}
    from_v = {it: _dot(m_k[it], pair_diag(sub(v, it))) for it in items}
    upd_v = {it: jnp.where(same_head, _dot_tn(sub(vb, it), sub(kend_b, it)), 0.0) for it in items}

    state = [state_ref[p] for p in range(D_HEADS // 2)]
    pairs = range(D_HEADS // 2)
    for ci in range(nchunk):
        from_state = [_dot_nt(lhs[ci, p], state[p].astype(BF16)) for p in pairs]
        u = [_dot(tinv[ci, p], pair_diag(from_state[p][:c] + from_v[ci, p][:c])) for p in pairs]
        upd_u = [_dot_tn(u[p].astype(BF16), sub(bend_b, (ci, p))) for p in pairs]
        state = [state[p] * p_all[ci][:, psl(p)] + jnp.where(same_head, upd_u[p], 0.0) + upd_v[ci, p] for p in pairs]
        for p in pairs:
            y_ref[rsl(ci), psl(p)] = from_state[p][c:] + from_v[ci, p][c:] + _dot(r_b[ci, p], pair_diag(u[p]))
    for p in pairs:
        state_ref[p] = state[p]

    y = y_ref[...]
    yc = y - head_sum(y) * (1.0 / D_HEAD)
    var = head_sum(yc * yc) * (1.0 / D_HEAD)
    yn = yc * lax.rsqrt(var + RWKV_LN_EPS) * lw_ref[...] + lb_ref[...]
    bonus = head_sum(r * k2 * rk_ref[...]) * v
    out_ref[...] = (yn + bonus).astype(out_ref.dtype)


def _rwkv(dproj, v_first, p, b, s):
    tr = D_STEP_CHUNKS * D_CHUNK
    vres = v_first is not None
    row = lambda wd: pl.BlockSpec((None, tr, wd), lambda bi, i: (bi, i, 0))
    full = lambda a: pl.BlockSpec(a.shape, lambda bi, i: (0, 0))
    ins = [dproj.reshape(b, s, D_PROJ_PAD)]
    in_specs = [row(D_PROJ_PAD)]
    if vres:
        ins.append(v_first.reshape(b, s, D_WIDTH))
        in_specs.append(row(D_WIDTH))
    names = ['mu', 'w0', 'w2', 'a0', 'a2'] + (['v0', 'v2'] if vres else []) + ['k_k', 'k_a', 'r_k', 'lnx_w', 'lnx_b']
    for nm in names:
        ins.append(p[nm])
        in_specs.append(full(p[nm]))
    head_id = np.arange(LANES) // D_HEAD
    ones_bd = jnp.asarray(head_id[:, None] == head_id[None, :], BF16)
    ins.append(ones_bd)
    in_specs.append(full(ones_bd))
    out_shape = [jax.ShapeDtypeStruct((b, s, D_WIDTH), BF16)]
    out_specs = [row(D_WIDTH)]
    if not vres:
        out_shape.append(jax.ShapeDtypeStruct((b, s, D_WIDTH), BF16))
        out_specs.append(row(D_WIDTH))
    res = pl.pallas_call(
        functools.partial(_rwkv_kernel, vres=vres, nchunk=D_STEP_CHUNKS), grid=(b, s // tr),
        in_specs=in_specs, out_specs=out_specs, out_shape=out_shape,
        scratch_shapes=[pltpu.VMEM((D_HEADS // 2, LANES, LANES), F32),
                        pltpu.VMEM((1, D_PROJ_PAD), F32),
                        pltpu.VMEM((tr, D_WIDTH), F32)],
        compiler_params=_cparams("parallel", "arbitrary"), name="rwkv7",
    )(*ins)
    d_out = res[0].reshape(b * s, D_WIDTH)
    vf = v_first if vres else res[1].reshape(b * s, D_WIDTH)
    return d_out, vf


def _rope_tables(positions, dim, theta, period, base, half):
    inv = jnp.exp(-math.log(theta) * jnp.arange(0, dim, 2, dtype=F32) / dim)
    ang = positions.astype(F32)[..., None] * inv
    cos, sin = jnp.cos(ang), jnp.sin(ang)
    lane = np.arange(LANES) % period - base
    lo = (lane >= 0) & (lane < half)
    hi = (lane >= half) & (lane < 2 * half)
    idx = np.where(lo, lane, np.where(hi, lane - half, 0))
    cos_l = jnp.take(cos, idx, axis=-1)
    sin_l = jnp.take(sin, idx, axis=-1)
    c = jnp.where(lo | hi, cos_l, 1.0)
    s1 = jnp.where(hi, sin_l, 0.0)
    s2 = jnp.where(lo, -sin_l, 0.0)
    n = positions.shape[0] * positions.shape[1]
    return tuple(t.reshape(n, LANES).astype(F32) for t in (c, s1, s2))


def _even_layer(x2d, b, s, mem_kv, tabs_a, tabs_r, norm, w_in, w_out, final_gain):
    q_scale = HEAD_DIM ** -0.5 * LOG2E
    aw = A_WIDTH
    plan = [(0, aw, 0, 0, (0, ROT_DIM // 2, q_scale), 0), (aw, aw, 0, aw, (0, ROT_DIM // 2, 1.0), aw),
            (2 * aw, aw, 0, 2 * aw, None, 2 * aw),
            (1536, 512, 0, 1536, (1, B_QK_DIM // 2, 1.0)), (2048, 512, 0, 2048, None),
            (2560, 256, 1, 0, (None, 0, MEM_Q_SCALE)), (2816, 640, 2, 0, None), (3456, 640, 2, 640, None)]
    sub_dils = tuple(d for d in A_DILATIONS if d > 1)
    res = _norm_proj(x2d, norm, w_in.astype(BF16), plan, (2560, 256, MIX_WIDTH), (tabs_a, tabs_r),
                     dils=sub_dils, sub_width=3 * aw, seq=s)
    main, qm, gate = res[:3]
    main3 = main.reshape(b, s, 2560)
    qkv = {1: main3.reshape(b, 1, s, 2560)}
    qkv.update(zip(sub_dils, res[3:]))
    outs, lses = [], []
    for dil in A_DILATIONS:
        o, lse = _band_attn(qkv[dil], 0, dil)
        outs.append(o)
        lses.append(lse)
    r_out = _retention(main3)
    m_out = _mem_attn(qm, mem_kv, b, s)
    return _out_proj((outs, lses), r_out, m_out, gate, x2d, w_out.astype(BF16), final_gain, seq=s)


def _odd_layer(x2d, b, s, mem_kv, tabs_m, v_first, final_gain, norm, w_in, q_norm, w_qb, kv_norm, w_kvb,
               mu_shift, w0, w2, a0, a2, v0, v2, k_k, k_a, r_k, lnx_w, lnx_b, w_out):
    vres = v0 is not None
    d_cols = mu_shift.shape[0]
    k_dim = w_in.shape[0]
    z = lambda n: jnp.zeros((k_dim, n), F32)
    o = 0
    cq = w_in[:, o:o + C_Q_RANK]; o += C_Q_RANK
    ckv = w_in[:, o:o + C_KV_RANK]; o += C_KV_RANK
    kro = w_in[:, o:o + C_ROPE]; o += C_ROPE
    dp = w_in[:, o:o + d_cols]; o += d_cols
    qm_w = w_in[:, o:o + MEM_WIDTH]; o += MEM_WIDTH
    gate_w = w_in[:, o:]
    dp_pad = jnp.concatenate([dp, z(D_PROJ_PAD - d_cols)], axis=1)
    w_pad = jnp.concatenate([cq, ckv, z(C_NOPE), kro, z(LANES - C_NOPE - C_ROPE), dp_pad, qm_w, gate_w], axis=1)
    plan = [(0, 512, 0, 0, None), (512, 896, 1, 0, None), (1408, 896, 1, 896, None),
            (2304, 256, 2, 0, (None, 0, MEM_Q_SCALE)), (2560, 640, 3, 0, None), (3200, 640, 3, 640, None)]
    mla_in, dproj, qm, gate = _norm_proj(x2d, norm, w_pad.astype(BF16), plan, (512, D_PROJ_PAD, 256, MIX_WIDTH))

    wq = w_qb.reshape(C_Q_RANK, C_HEADS, C_NOPE + C_ROPE)
    wq = jnp.pad(wq, ((0, 0), (0, 0), (0, LANES - C_NOPE - C_ROPE))).reshape(C_Q_RANK, C_HEADS * LANES)
    wkv = w_kvb.reshape(C_KV_RANK, C_HEADS, C_NOPE + C_V)
    wk = jnp.pad(wkv[:, :, :C_NOPE], ((0, 0), (0, 0), (0, LANES - C_NOPE))).reshape(C_KV_RANK, C_HEADS * LANES)
    wv = jnp.pad(wkv[:, :, C_NOPE:], ((0, 0), (0, 0), (0, LANES - C_V))).reshape(C_KV_RANK, C_HEADS * LANES)
    q, k, v = _mla_prep(mla_in, q_norm, kv_norm, wq.astype(BF16), wk.astype(BF16), wv.astype(BF16), tabs_m)
    c_out = _mla_attn(q, k, v, b, s)

    row = lambda t: t.reshape(1, -1).astype(F32)
    pad_rows = lambda t, top, n: jnp.concatenate(
        [jnp.zeros((top, t.shape[1]), F32), t, jnp.zeros((n - top - t.shape[0], t.shape[1]), F32)], axis=0)
    p = dict(mu=row(jnp.concatenate([mu_shift, jnp.zeros((D_PROJ_PAD - d_cols,), F32)])),
             w0=row(w0), w2=pad_rows(w2, 0, LANES).astype(BF16),
             a0=row(a0), a2=pad_rows(a2, D_DECAY_LORA, LANES).astype(BF16),
             k_k=row(k_k), k_a=row(k_a), r_k=row(r_k), lnx_w=row(lnx_w), lnx_b=row(lnx_b))
    if vres:
        p['v0'] = row(v0)
        p['v2'] = pad_rows(v2, 0, LANES).astype(BF16)
    d_out, v_first = _rwkv(dproj, v_first, p, b, s)
    m_out = _mem_attn(qm, mem_kv, b, s)
    x_new = _out_proj(c_out, d_out, m_out, gate, x2d, w_out.astype(BF16), final_gain)
    return x_new, v_first


def kernel(x, mem, positions, mem_norm, final_norm, l0_norm, l0_w_in, l0_w_mem_kv, l0_w_out, l1_norm, l1_w_in, l1_q_norm, l1_w_qb, l1_kv_norm, l1_w_kvb, l1_mu_shift, l1_w0, l1_w2, l1_a0, l1_a2, l1_k_k, l1_k_a, l1_r_k, l1_lnx_w, l1_lnx_b, l1_w_mem_kv, l1_w_out, l2_norm, l2_w_in, l2_w_mem_kv, l2_w_out, l3_norm, l3_w_in, l3_q_norm, l3_w_qb, l3_kv_norm, l3_w_kvb, l3_mu_shift, l3_w0, l3_w2, l3_a0, l3_a2, l3_v0, l3_v2, l3_k_k, l3_k_a, l3_r_k, l3_lnx_w, l3_lnx_b, l3_w_mem_kv, l3_w_out):
    b, s, d = x.shape
    x2d = x.reshape(b * s, d)
    mem2d = mem.reshape(b * MEM_TOKENS, d)
    tabs_a = _rope_tables(positions, ROT_DIM, ROPE_THETA, HEAD_DIM, 0, ROT_DIM // 2)
    tabs_r = _rope_tables(positions, B_QK_DIM, RET_THETA, B_QK_DIM, 0, B_QK_DIM // 2)
    tabs_m = _rope_tables(positions, C_ROPE, ROPE_THETA, LANES, C_NOPE, C_ROPE // 2)

    def mem_kv(w):
        plan = [(0, 2 * MEM_WIDTH, 0, 0, None)]
        return _norm_proj(mem2d, mem_norm, w.astype(BF16), plan, (2 * MEM_WIDTH,))[0]

    x2d = _even_layer(x2d, b, s, mem_kv(l0_w_mem_kv), tabs_a, tabs_r, l0_norm, l0_w_in, l0_w_out, None)
    x2d, v_first = _odd_layer(x2d, b, s, mem_kv(l1_w_mem_kv), tabs_m, None, None, l1_norm, l1_w_in, l1_q_norm,
                              l1_w_qb, l1_kv_norm, l1_w_kvb, l1_mu_shift, l1_w0, l1_w2, l1_a0, l1_a2, None, None,
                              l1_k_k, l1_k_a, l1_r_k, l1_lnx_w, l1_lnx_b, l1_w_out)
    x2d = _even_layer(x2d, b, s, mem_kv(l2_w_mem_kv), tabs_a, tabs_r, l2_norm, l2_w_in, l2_w_out, None)
    x2d, _ = _odd_layer(x2d, b, s, mem_kv(l3_w_mem_kv), tabs_m, v_first, final_norm, l3_norm, l3_w_in, l3_q_norm,
                        l3_w_qb, l3_kv_norm, l3_w_kvb, l3_mu_shift, l3_w0, l3_w2, l3_a0, l3_a2, l3_v0, l3_v2,
                        l3_k_k, l3_k_a, l3_r_k, l3_lnx_w, l3_lnx_b, l3_w_out)
    return x2d.reshape(b, s, d)
```

```python
import functools
import math

import numpy as np
import jax
import jax.numpy as jnp
from jax import lax
from jax.experimental import pallas as pl
from jax.experimental.pallas import tpu as pltpu

F32 = jnp.float32
BF16 = jnp.bfloat16

LANES = 128
VMEM_LIMIT = 48 * 1024 * 1024

D_MODEL = 1024
HEAD_DIM = 64
ROPE_THETA = 500000.0
ROT_DIM = HEAD_DIM // 4
NORM_EPS = 1e-6
NEG_INF = -1e30
LN2 = math.log(2.0)
LOG2E = 1.0 / LN2

A_HEADS = 8
A_WIDTH = A_HEADS * HEAD_DIM
A_PATTERNS = ((128, 1), (512, 4), (2048, 16))
A_BLOCK = 128
A_DILATIONS = tuple(d for (_, d) in A_PATTERNS)
A_LSE_GROUP = HEAD_DIM // (A_HEADS // 2)

B_HEADS = 4
B_QK_DIM = 64
B_V_DIM = 128
B_CHUNK = 128
RET_THETA = 10000.0

MEM_TOKENS = 256
MEM_HEADS = 4
MEM_WIDTH = MEM_HEADS * HEAD_DIM
MEM_Q_SCALE = HEAD_DIM ** -0.5 / math.log(2.0)

C_HEADS = 8
C_Q_RANK = 256
C_KV_RANK = 128
C_NOPE = 64
C_ROPE = 32
C_V = 64

D_HEADS = 8
D_HEAD = 64
D_WIDTH = D_HEADS * D_HEAD
D_DECAY_LORA = 64
D_AAA_LORA = 64
D_MV_LORA = 32
RWKV_LN_EPS = 64e-5
D_CHUNK = 64
D_STEP_CHUNKS = 4
D_PROJ_PAD = 3 * D_WIDTH + 2 * LANES

MIX_WIDTH = 1280


def _cparams(*sem):
    return pltpu.CompilerParams(dimension_semantics=sem, vmem_limit_bytes=VMEM_LIMIT)


def _dot(a, b):
    return jnp.dot(a, b, preferred_element_type=F32)


def _dot_nt(a, b):
    return lax.dot_general(a, b, (((1,), (1,)), ((), ())), preferred_element_type=F32)


def _dot_tn(a, b):
    return lax.dot_general(a, b, (((0,), (0,)), ((), ())), preferred_element_type=F32)


def _rot_slab(x, c, s1, s2, half):
    return x * c + pltpu.roll(x, half, 1) * s1 + pltpu.roll(x, LANES - half, 1) * s2


def _rotary(x, c, s1, s2, half):
    slabs = [_rot_slab(x[:, i:i + LANES], c, s1, s2, half) for i in range(0, x.shape[1], LANES)]
    return slabs[0] if len(slabs) == 1 else jnp.concatenate(slabs, axis=1)


def _norm_proj_kernel(*refs, plan, n_tab, n_out, dils):
    x_ref, g_ref, w_ref = refs[:3]
    tabs = refs[3:3 + 3 * n_tab]
    outs = refs[3 + 3 * n_tab:3 + 3 * n_tab + n_out]
    sub_outs = refs[3 + 3 * n_tab + n_out:3 + 3 * n_tab + n_out + len(dils)]
    stage_ref = refs[-1] if dils else None
    tm = x_ref.shape[0]
    x = x_ref[...].astype(F32)
    ms = jnp.mean(x * x, axis=-1, keepdims=True)
    xn = (x * lax.rsqrt(ms + NORM_EPS) * g_ref[...]).astype(BF16)
    for (wc, width, oi, oc, rot, sub_col) in plan:
        acc = _dot(xn, w_ref[:, wc:wc + width])
        if rot is not None:
            t, half, post_scale = rot
            if t is not None:
                acc = _rotary(acc, tabs[3 * t][...], tabs[3 * t + 1][...], tabs[3 * t + 2][...], half)
            if post_scale != 1.0:
                acc = acc * post_scale
        outs[oi][:, oc:oc + width] = acc.astype(outs[oi].dtype)
        if sub_col is not None:
            for c in range(width // LANES):
                stage_ref[c] = acc[:, c * LANES:(c + 1) * LANES]
            for d, sub in zip(dils, sub_outs):
                for r in range(d):
                    for c in range(width // LANES):
                        col = sub_col + c * LANES
                        sub[r, :, col:col + LANES] = stage_ref[c, pl.ds(r, tm // d, stride=d), :].astype(sub.dtype)


def _norm_proj(x2d, g, w, plan, out_widths, tables=(), tm=256, dils=(), sub_width=0, seq=None):
    t_rows, k = x2d.shape
    n = w.shape[1]
    assert t_rows % tm == 0
    plan = tuple(tuple(p) + (None,) * (6 - len(p)) for p in plan)
    flat_tabs = [t for tab in tables for t in tab]
    in_specs = [pl.BlockSpec((tm, k), lambda i: (i, 0)),
                pl.BlockSpec((1, k), lambda i: (0, 0)),
                pl.BlockSpec((k, n), lambda i: (0, 0))]
    in_specs += [pl.BlockSpec((tm, LANES), lambda i: (i, 0)) for _ in flat_tabs]
    out_specs = [pl.BlockSpec((tm, ow), lambda i: (i, 0)) for ow in out_widths]
    out_shape = [jax.ShapeDtypeStruct((t_rows, ow), BF16) for ow in out_widths]
    scratch = []
    if dils:
        nt = seq // tm
        for d in dils:
            out_specs.append(pl.BlockSpec((None, d, tm // d, sub_width), lambda i: (i // nt, 0, i % nt, 0)))
            out_shape.append(jax.ShapeDtypeStruct((t_rows // seq, d, seq // d, sub_width), BF16))
        stage_slabs = max(p[1] for p in plan if p[5] is not None) // LANES
        scratch = [pltpu.VMEM((stage_slabs, tm, LANES), F32)]
    kern = functools.partial(_norm_proj_kernel, plan=plan, n_tab=len(tables), n_out=len(out_widths), dils=tuple(dils))
    return pl.pallas_call(
        kern, grid=(t_rows // tm,), in_specs=in_specs, out_specs=out_specs, out_shape=out_shape,
        scratch_shapes=scratch, compiler_params=_cparams("parallel"), name="norm_proj",
    )(x2d, g.reshape(1, k).astype(F32), w, *flat_tabs)


def _band_attn_kernel(q_ref, kp_ref, kc_ref, vp_ref, vc_ref, o_ref, lse_ref, s_ref, p_ref, m_ref, *, nq):
    i = pl.program_id(2)
    qi = lax.broadcasted_iota(jnp.int32, (A_BLOCK, A_BLOCK), 0)
    kj = lax.broadcasted_iota(jnp.int32, (A_BLOCK, A_BLOCK), 1)
    bias_prev = jnp.where(kj >= qi, 0.0, NEG_INF)
    bias_cur = jnp.where(kj <= qi, 0.0, NEG_INF)
    bias_first = bias_prev + jnp.where(i > 0, 0.0, NEG_INF)
    lane = lax.broadcasted_iota(jnp.int32, (A_BLOCK, LANES), 1)
    lo = lane < HEAD_DIM
    keep = (jnp.where(lo, 1.0, 0.0).astype(BF16), jnp.where(lo, 0.0, 1.0).astype(BF16))
    npair = A_HEADS // 2

    def operands(u, pair):
        sl = slice(pair * LANES, (pair + 1) * LANES)
        rows = slice(u * A_BLOCK, (u + 1) * A_BLOCK)
        if u == 0:
            return rows, sl, kp_ref[:, sl], vp_ref[:, sl], bias_first
        prow = slice((u - 1) * A_BLOCK, u * A_BLOCK)
        return rows, sl, kc_ref[prow, sl], vc_ref[prow, sl], bias_prev

    for u in range(nq):
        for pair in range(npair):
            rows, sl, kp, _, bp = operands(u, pair)
            q = q_ref[rows, sl]
            kc = kc_ref[rows, sl]
            for half in range(2):
                idx = (u * npair + pair) * 2 + half
                qh = q * keep[half]
                s_ref[idx, :, :A_BLOCK] = _dot_nt(qh, kp) + bp
                s_ref[idx, :, A_BLOCK:] = _dot_nt(qh, kc) + bias_cur
    for idx in range(nq * A_HEADS):
        s = s_ref[idx]
        m = jnp.max(s, -1, keepdims=True)
        p_ref[idx] = jnp.exp2(s - m).astype(BF16)
        m_ref[idx] = jnp.broadcast_to(m, (A_BLOCK, LANES))
    for u in range(nq):
        lse = None
        for pair in range(npair):
            rows, sl, _, vp, _ = operands(u, pair)
            vc = vc_ref[rows, sl]
            idx0 = (u * npair + pair) * 2
            o = [_dot(p_ref[idx0 + half, :, :A_BLOCK], vp * keep[half] + keep[1 - half])
                 + _dot(p_ref[idx0 + half, :, A_BLOCK:], vc * keep[half] + keep[1 - half]) for half in range(2)]
            den = pltpu.roll(jnp.where(lo, o[1], o[0]), HEAD_DIM, 1)
            o_ref[rows, sl] = (jnp.where(lo, o[0], o[1]) / den).astype(o_ref.dtype)
            lse_pair = jnp.where(lo, m_ref[idx0], m_ref[idx0 + 1]) * LN2 + jnp.log(den)
            lse = lse_pair if lse is None else jnp.where(lane % HEAD_DIM // A_LSE_GROUP == pair, lse_pair, lse)
        lse_ref[rows, :] = lse


def _single_block_attn_kernel(q_ref, k_ref, v_ref, o_ref, lse_ref, s_ref, p_ref, m_ref, *, nres):
    qi = lax.broadcasted_iota(jnp.int32, (A_BLOCK, A_BLOCK), 0)
    kj = lax.broadcasted_iota(jnp.int32, (A_BLOCK, A_BLOCK), 1)
    bias = jnp.where(kj <= qi, 0.0, NEG_INF)
    lane = lax.broadcasted_iota(jnp.int32, (A_BLOCK, LANES), 1)
    lo = lane < HEAD_DIM
    keep = (jnp.where(lo, 1.0, 0.0).astype(BF16), jnp.where(lo, 0.0, 1.0).astype(BF16))
    npair = A_HEADS // 2
    for r in range(nres):
        for pair in range(npair):
            sl = slice(pair * LANES, (pair + 1) * LANES)
            for half in range(2):
                s_ref[(r * npair + pair) * 2 + half] = _dot_nt(q_ref[r, :, sl] * keep[half], k_ref[r, :, sl]) + bias
    for idx in range(nres * A_HEADS):
        s = s_ref[idx]
        m = jnp.max(s, -1, keepdims=True)
        p_ref[idx] = jnp.exp2(s - m).astype(BF16)
        m_ref[idx] = jnp.broadcast_to(m, (A_BLOCK, LANES))
    for r in range(nres):
        lse = None
        for pair in range(npair):
            sl = slice(pair * LANES, (pair + 1) * LANES)
            idx0 = (r * npair + pair) * 2
            v = v_ref[r, :, sl]
            o = [_dot(p_ref[idx0 + half], v * keep[half] + keep[1 - half]) for half in range(2)]
            den = pltpu.roll(jnp.where(lo, o[1], o[0]), HEAD_DIM, 1)
            o_ref[r, :, sl] = (jnp.where(lo, o[0], o[1]) / den).astype(o_ref.dtype)
            lse_pair = jnp.where(lo, m_ref[idx0], m_ref[idx0 + 1]) * LN2 + jnp.log(den)
            lse = lse_pair if lse is None else jnp.where(lane % HEAD_DIM // A_LSE_GROUP == pair, lse_pair, lse)
        lse_ref[r] = lse


def _single_block_attn(qkv, col0, dilation, nres=4):
    b, d, sub_len, _ = qkv.shape
    assert sub_len == A_BLOCK and d % nres == 0
    spec = lambda col, w: pl.BlockSpec((None, nres, A_BLOCK, w), lambda bi, r: (bi, r, 0, col))
    return pl.pallas_call(
        functools.partial(_single_block_attn_kernel, nres=nres), grid=(b, d // nres),
        in_specs=[spec(col0, A_WIDTH), spec(col0 + 1, A_WIDTH), spec(col0 + 2, A_WIDTH)],
        out_specs=[spec(0, A_WIDTH), spec(0, LANES)],
        out_shape=[jax.ShapeDtypeStruct((b, d, sub_len, A_WIDTH), BF16),
                   jax.ShapeDtypeStruct((b, d, sub_len, LANES), F32)],
        scratch_shapes=[pltpu.VMEM((nres * A_HEADS, A_BLOCK, A_BLOCK), F32),
                        pltpu.VMEM((nres * A_HEADS, A_BLOCK, A_BLOCK), BF16),
                        pltpu.VMEM((nres * A_HEADS, A_BLOCK, LANES), F32)],
        compiler_params=_cparams("parallel", "parallel"), name="band_attn_d%d" % dilation,
    )(qkv, qkv, qkv)


def _band_attn(qkv, col0, dilation):
    b, d, sub_len, _ = qkv.shape
    assert d == dilation
    if sub_len == A_BLOCK:
        return _single_block_attn(qkv, col0, dilation)
    nq = 2 if sub_len % (2 * A_BLOCK) == 0 else 1
    tq = nq * A_BLOCK

    def cur(col):
        return pl.BlockSpec((None, None, tq, A_WIDTH), lambda bi, r, i: (bi, r, i, col0 + col))

    def prev(col):
        return pl.BlockSpec((None, None, A_BLOCK, A_WIDTH),
                            lambda bi, r, i: (bi, r, jnp.maximum(i * nq - 1, 0), col0 + col))

    return pl.pallas_call(
        functools.partial(_band_attn_kernel, nq=nq), grid=(b, d, sub_len // tq),
        in_specs=[cur(0), prev(1), cur(1), prev(2), cur(2)],
        out_specs=[pl.BlockSpec((None, None, tq, A_WIDTH), lambda bi, r, i: (bi, r, i, 0)),
                   pl.BlockSpec((None, None, tq, LANES), lambda bi, r, i: (bi, r, i, 0))],
        out_shape=[jax.ShapeDtypeStruct((b, d, sub_len, A_WIDTH), BF16),
                   jax.ShapeDtypeStruct((b, d, sub_len, LANES), F32)],
        scratch_shapes=[pltpu.VMEM((nq * A_HEADS, A_BLOCK, 2 * A_BLOCK), F32),
                        pltpu.VMEM((nq * A_HEADS, A_BLOCK, 2 * A_BLOCK), BF16),
                        pltpu.VMEM((nq * A_HEADS, A_BLOCK, LANES), F32)],
        compiler_params=_cparams("parallel", "parallel", "parallel"), name="band_attn_d%d" % dilation,
    )(qkv, qkv, qkv, qkv, qkv)


def _retention_kernel(qk_ref, v_ref, o_ref, state_ref, *, nchunk):
    c = B_CHUNK
    @pl.when(pl.program_id(1) == 0)
    def _():
        state_ref[...] = jnp.zeros_like(state_ref)

    scale = B_QK_DIM ** -0.5
    qi = lax.broadcasted_iota(jnp.int32, (c, c), 0)
    kj = lax.broadcasted_iota(jnp.int32, (c, c), 1)
    diff = (qi - kj).astype(F32)
    row = lax.broadcasted_iota(jnp.int32, (c, 1), 0).astype(F32)
    lo = lax.broadcasted_iota(jnp.int32, (c, LANES), 1) < B_QK_DIM
    keep = (jnp.where(lo, 1.0, 0.0).astype(BF16), jnp.where(lo, 0.0, 1.0).astype(BF16))
    log_g = [math.log1p(-2.0 ** (-5.0 - h)) for h in range(B_HEADS)]
    d_in = [jnp.where(diff >= 0, jnp.exp(jnp.maximum(diff, 0.0) * lg), 0.0) * scale for lg in log_g]
    from_start = [jnp.exp((row + 1.0) * lg) for lg in log_g]
    to_end = [jnp.exp((c - 1.0 - row) * lg) * scale for lg in log_g]
    items = [(ci, h) for ci in range(nchunk) for h in range(B_HEADS)]
    rows = lambda ci: slice(ci * c, (ci + 1) * c)
    qsl = lambda h: slice(h // 2 * LANES, (h // 2 + 1) * LANES)
    ksl = lambda h: slice(B_HEADS * B_QK_DIM + h // 2 * LANES, B_HEADS * B_QK_DIM + (h // 2 + 1) * LANES)
    vsl = lambda h: slice(h * B_V_DIM, (h + 1) * B_V_DIM)
    qm = {(ci, h): qk_ref[rows(ci), qsl(h)] * keep[h % 2] for ci, h in items}
    scores = {(ci, h): (_dot_nt(qm[ci, h], qk_ref[rows(ci), ksl(h)]) * d_in[h]).astype(BF16) for ci, h in items}
    o_in = {(ci, h): _dot(scores[ci, h], v_ref[rows(ci), vsl(h)]) for ci, h in items}
    kd = {(ci, h): ((qk_ref[rows(ci), ksl(h)] * keep[h % 2]).astype(F32) * to_end[h]).astype(BF16) for ci, h in items}
    kv = {(ci, h): _dot_tn(kd[ci, h], v_ref[rows(ci), vsl(h)]) for ci, h in items}
    state = [state_ref[h] for h in range(B_HEADS)]
    for ci in range(nchunk):
        o = [o_in[ci, h] + _dot(qm[ci, h], state[h].astype(BF16)) * from_start[h] for h in range(B_HEADS)]
        state = [state[h] * math.exp(c * log_g[h]) + kv[ci, h] for h in range(B_HEADS)]
        for h in range(B_HEADS):
            on = o[h] * lax.rsqrt(jnp.mean(o[h] * o[h], -1, keepdims=True) + NORM_EPS)
            o_ref[rows(ci), vsl(h)] = on.astype(o_ref.dtype)
    for h in range(B_HEADS):
        state_ref[h] = state[h]


def _retention(main, nchunk=2):
    b, s, _ = main.shape
    blk = (None, nchunk * B_CHUNK, 512)
    o = pl.pallas_call(
        functools.partial(_retention_kernel, nchunk=nchunk), grid=(b, s // (nchunk * B_CHUNK)),
        in_specs=[pl.BlockSpec(blk, lambda bi, i: (bi, i, 3)), pl.BlockSpec(blk, lambda bi, i: (bi, i, 4))],
        out_specs=pl.BlockSpec(blk, lambda bi, i: (bi, i, 0)),
        out_shape=jax.ShapeDtypeStruct((b, s, B_HEADS * B_V_DIM), BF16),
        scratch_shapes=[pltpu.VMEM((B_HEADS, LANES, B_V_DIM), F32)],
        compiler_params=_cparams("parallel", "arbitrary"), name="retention",
    )(main, main)
    return o.reshape(b * s, B_HEADS * B_V_DIM)


def _mem_attn_kernel(q_ref, kv_ref, o_ref, s_ref, p_ref):
    tq = q_ref.shape[0]
    lo_q = lax.broadcasted_iota(jnp.int32, (tq, LANES), 1) < HEAD_DIM
    lo_k = lax.broadcasted_iota(jnp.int32, (MEM_TOKENS, LANES), 1) < HEAD_DIM
    keep_q = (jnp.where(lo_q, 1.0, 0.0).astype(BF16), jnp.where(lo_q, 0.0, 1.0).astype(BF16))
    keep_k = (jnp.where(lo_k, 1.0, 0.0).astype(BF16), jnp.where(lo_k, 0.0, 1.0).astype(BF16))
    npair = MEM_HEADS // 2
    for pair in range(npair):
        sl = slice(pair * LANES, (pair + 1) * LANES)
        for half in range(2):
            s_ref[2 * pair + half] = _dot_nt(q_ref[:, sl] * keep_q[half], kv_ref[:, sl])
    for idx in range(MEM_HEADS):
        s = s_ref[idx]
        p_ref[idx] = jnp.exp2(s - jnp.max(s, -1, keepdims=True)).astype(BF16)
    for pair in range(npair):
        sl = slice(pair * LANES, (pair + 1) * LANES)
        v = kv_ref[:, MEM_WIDTH + pair * LANES:MEM_WIDTH + (pair + 1) * LANES]
        o = [_dot(p_ref[2 * pair + half], v * keep_k[half] + keep_k[1 - half]) for half in range(2)]
        den = pltpu.roll(jnp.where(lo_q, o[1], o[0]), HEAD_DIM, 1)
        o_ref[:, sl] = (jnp.where(lo_q, o[0], o[1]) / den).astype(o_ref.dtype)


def _mem_attn(qm, kv, b, s, tq=512):
    o = pl.pallas_call(
        _mem_attn_kernel, grid=(b, s // tq),
        in_specs=[pl.BlockSpec((None, tq, MEM_WIDTH), lambda bi, i: (bi, i, 0)),
                  pl.BlockSpec((None, MEM_TOKENS, 2 * MEM_WIDTH), lambda bi, i: (bi, 0, 0))],
        out_specs=pl.BlockSpec((None, tq, MEM_WIDTH), lambda bi, i: (bi, i, 0)),
        out_shape=jax.ShapeDtypeStruct((b, s, MEM_WIDTH), BF16),
        scratch_shapes=[pltpu.VMEM((MEM_HEADS, tq, MEM_TOKENS), F32),
                        pltpu.VMEM((MEM_HEADS, tq, MEM_TOKENS), BF16)],
        compiler_params=_cparams("parallel", "parallel"), name="mem_attn",
    )(qm.reshape(b, s, MEM_WIDTH), kv.reshape(b, MEM_TOKENS, 2 * MEM_WIDTH))
    return o.reshape(b * s, MEM_WIDTH)


def _mixture(o_refs, l_refs, e_ref, stage_o, stage_l, dils, tm):
    n_slab = A_WIDTH // LANES
    lses = []
    for pi, d in enumerate(dils):
        if d == 1:
            lses.append(l_refs[pi][0])
        else:
            for r in range(d):
                stage_l[pi, pl.ds(r, tm // d, stride=d), :] = l_refs[pi][r]
                for c in range(n_slab):
                    stage_o[pi, c, pl.ds(r, tm // d, stride=d), :] = (
                        o_refs[pi][r, :, c * LANES:(c + 1) * LANES].astype(F32))
            lses.append(stage_l[pi])
    m = functools.reduce(jnp.maximum, lses)
    es = [jnp.exp(l - m) for l in lses]
    inv = 1.0 / functools.reduce(lambda a, b: a + b, es)
    first = None
    for pi, d in enumerate(dils):
        wgt = es[pi] * inv
        w_hi = wgt.astype(BF16)
        w_lo = (wgt - w_hi.astype(F32)).astype(BF16)
        w_full = _dot(w_hi, e_ref[...]) + _dot(w_lo, e_ref[...])
        if d == 1:
            o = o_refs[pi][0].astype(F32)
        else:
            o = jnp.concatenate([stage_o[pi, c] for c in range(n_slab)], axis=1)
        first = w_full * o if first is None else first + w_full * o
    return first


def _out_proj_kernel(*refs, dils, final):
    n_mix = len(dils)
    if n_mix:
        o_refs, l_refs, e_ref = refs[:n_mix], refs[n_mix:2 * n_mix], refs[2 * n_mix]
        refs = refs[2 * n_mix + 1:]
    else:
        first = refs[0][...].astype(F32)
        refs = refs[1:]
    second, third, gate_ref, x_ref, w_ref = refs[:5]
    refs = refs[5:]
    if final:
        g_ref, out_ref = refs[:2]
    else:
        out_ref = refs[0]
    if n_mix:
        stage_o, stage_l = refs[-2:]
        first = _mixture(o_refs, l_refs, e_ref, stage_o, stage_l, dils, x_ref.shape[0])
    g = gate_ref[...].astype(F32)
    sg = g * (1.0 / (1.0 + jnp.exp(-g)))
    w1 = first.shape[1]
    w2 = w1 + second.shape[1]
    y1 = (first * sg[:, :w1]).astype(BF16)
    y2 = (second[...].astype(F32) * sg[:, w1:w2]).astype(BF16)
    y3 = (third[...].astype(F32) * sg[:, w2:]).astype(BF16)
    xn = x_ref[...] + _dot(y1, w_ref[:w1, :]) + _dot(y2, w_ref[w1:w2, :]) + _dot(y3, w_ref[w2:, :])
    if final:
        xn = xn * lax.rsqrt(jnp.mean(xn * xn, -1, keepdims=True) + NORM_EPS) * g_ref[...]
    out_ref[...] = xn


def _lse_expansion():
    dst = np.arange(A_WIDTH)
    head = dst // HEAD_DIM
    src = (head % 2) * HEAD_DIM + (head // 2) * A_LSE_GROUP
    return jnp.asarray(np.arange(LANES)[:, None] == src[None, :], BF16)


def _out_proj(first, second, third, gate, x2d, w_out, final_gain=None, tm=256, seq=None):
    t_rows, dm = x2d.shape
    row = lambda a: pl.BlockSpec((tm, a.shape[1]), lambda i: (i, 0))
    full = lambda a: pl.BlockSpec(a.shape, lambda i: (0, 0))
    dils, scratch = (), []
    if isinstance(first, tuple):
        o_list, l_list = first
        dils = tuple(o.shape[1] for o in o_list)
        nt = seq // tm
        sub = lambda a: pl.BlockSpec((None, a.shape[1], tm // a.shape[1], a.shape[3]),
                                     lambda i: (i // nt, 0, i % nt, 0))
        expand = _lse_expansion()
        ins = list(o_list) + list(l_list) + [expand]
        in_specs = [sub(a) for a in ins[:-1]] + [full(expand)]
        scratch = [pltpu.VMEM((len(dils), A_WIDTH // LANES, tm, LANES), F32),
                   pltpu.VMEM((len(dils), tm, LANES), F32)]
    else:
        ins, in_specs = [first], [row(first)]
    for a in (second, third, gate, x2d):
        ins.append(a)
        in_specs.append(row(a))
    ins.append(w_out)
    in_specs.append(full(w_out))
    if final_gain is not None:
        ins.append(final_gain.reshape(1, dm).astype(F32))
        in_specs.append(pl.BlockSpec((1, dm), lambda i: (0, 0)))
    kern = functools.partial(_out_proj_kernel, dils=dils, final=final_gain is not None)
    return pl.pallas_call(
        kern, grid=(t_rows // tm,), in_specs=in_specs,
        out_specs=pl.BlockSpec((tm, dm), lambda i: (i, 0)),
        out_shape=jax.ShapeDtypeStruct((t_rows, dm), F32), scratch_shapes=scratch,
        compiler_params=_cparams("parallel"), name="out_proj",
    )(*ins)


def _mla_prep_kernel(in_ref, qn_ref, kvn_ref, wq_ref, wqs_ref, wk_ref, wv_ref, c_ref, s1_ref, s2_ref,
                     q_out, k_out, v_out):
    c, s = c_ref[...], s1_ref[...] + s2_ref[...]
    k0 = C_Q_RANK + C_KV_RANK
    cq = in_ref[:, :C_Q_RANK].astype(F32)
    ckv = in_ref[:, C_Q_RANK:k0].astype(F32)
    kr = in_ref[:, k0:k0 + LANES].astype(F32)
    kr_sw = in_ref[:, k0 + LANES:].astype(F32)
    cqn = (cq * lax.rsqrt(jnp.mean(cq * cq, -1, keepdims=True) + NORM_EPS) * qn_ref[...]).astype(BF16)
    ckvn = (ckv * lax.rsqrt(jnp.mean(ckv * ckv, -1, keepdims=True) + NORM_EPS) * kvn_ref[...]).astype(BF16)
    q_scale = (C_NOPE + C_ROPE) ** -0.5 * LOG2E
    c_all = jnp.tile(c * q_scale, (1, C_HEADS))
    s_all = jnp.tile(s * q_scale, (1, C_HEADS))
    q_out[...] = (_dot(cqn, wq_ref[...]) * c_all + _dot(cqn, wqs_ref[...]) * s_all).astype(q_out.dtype)
    kpe = kr * c + kr_sw * s
    kn = _dot(ckvn, wk_ref[...])
    for h in range(C_HEADS):
        k_out[:, h * LANES:(h + 1) * LANES] = (kn[:, h * LANES:(h + 1) * LANES] + kpe).astype(k_out.dtype)
    lane = lax.broadcasted_iota(jnp.int32, (1, C_HEADS * LANES), 1)
    ones = jnp.where(lane % LANES >= C_V, 1.0, 0.0)
    v_out[...] = (_dot(ckvn, wv_ref[...]) + ones).astype(v_out.dtype)


def _mla_prep(mla_in, q_norm, kv_norm, wq, wq_sw, wk, wv, tabs, tm=256):
    t_rows = mla_in.shape[0]
    row = lambda w: pl.BlockSpec((tm, w), lambda i: (i, 0))
    full = lambda a: pl.BlockSpec(a.shape, lambda i: (0, 0))
    qn = q_norm.reshape(1, -1).astype(F32)
    kvn = kv_norm.reshape(1, -1).astype(F32)
    return pl.pallas_call(
        _mla_prep_kernel, grid=(t_rows // tm,),
        in_specs=[row(mla_in.shape[1]), full(qn), full(kvn), full(wq), full(wq_sw), full(wk), full(wv),
                  row(LANES), row(LANES), row(LANES)],
        out_specs=[row(C_HEADS * LANES)] * 3,
        out_shape=[jax.ShapeDtypeStruct((t_rows, C_HEADS * LANES), BF16)] * 3,
        compiler_params=_cparams("parallel"), name="mla_prep",
    )(mla_in, qn, kvn, wq, wq_sw, wk, wv, *tabs)


def _mla_attn_kernel(q_ref, k_ref, v_ref, o_ref, m_ref, acc_ref, s_ref, p_ref, alpha_ref, *, tq):
    i = pl.program_id(1)
    diag_ok = (lax.broadcasted_iota(jnp.int32, (tq, tq), 1) <= lax.broadcasted_iota(jnp.int32, (tq, tq), 0))
    m_ref[...] = jnp.full(m_ref.shape, NEG_INF, F32)
    acc_ref[...] = jnp.zeros(acc_ref.shape, F32)

    def block(j, masked):
        rows = pl.ds(pl.multiple_of(j * tq, tq), tq)
        for h in range(C_HEADS):
            sl = slice(h * LANES, (h + 1) * LANES)
            s = _dot_nt(q_ref[:, sl], k_ref[rows, sl])
            s_ref[h] = jnp.where(diag_ok, s, NEG_INF) if masked else s
        for h in range(C_HEADS):
            s = s_ref[h]
            m_old = m_ref[h]
            m_new = jnp.maximum(m_old, jnp.max(s, -1, keepdims=True))
            alpha_ref[h] = jnp.exp2(m_old - m_new)
            p_ref[h] = jnp.exp2(s - jnp.tile(m_new, (1, tq // LANES))).astype(BF16)
            m_ref[h] = m_new
        for h in range(C_HEADS):
            sl = slice(h * LANES, (h + 1) * LANES)
            acc_ref[h] = alpha_ref[h] * acc_ref[h] + _dot(p_ref[h], v_ref[rows, sl])

    def body(j, carry):
        block(j, False)
        return carry

    lax.fori_loop(0, i, body, 0)
    block(i, True)
    for h in range(C_HEADS):
        acc = acc_ref[h]
        o_ref[:, h * C_V:(h + 1) * C_V] = (acc[:, :C_V] / acc[:, C_V:]).astype(o_ref.dtype)


def _mla_attn(q, k, v, b, s, tq=512):
    qw, vw = C_HEADS * LANES, C_HEADS * C_V
    o = pl.pallas_call(
        functools.partial(_mla_attn_kernel, tq=tq), grid=(b, s // tq),
        in_specs=[pl.BlockSpec((None, tq, qw), lambda bi, i: (bi, i, 0)),
                  pl.BlockSpec((None, s, qw), lambda bi, i: (bi, 0, 0)),
                  pl.BlockSpec((None, s, qw), lambda bi, i: (bi, 0, 0))],
        out_specs=pl.BlockSpec((None, tq, vw), lambda bi, i: (bi, i, 0)),
        out_shape=jax.ShapeDtypeStruct((b, s, vw), BF16),
        scratch_shapes=[pltpu.VMEM((C_HEADS, tq, LANES), F32),
                        pltpu.VMEM((C_HEADS, tq, LANES), F32),
                        pltpu.VMEM((C_HEADS, tq, tq), F32),
                        pltpu.VMEM((C_HEADS, tq, tq), BF16),
                        pltpu.VMEM((C_HEADS, tq, LANES), F32)],
        compiler_params=_cparams("parallel", "parallel"), name="mla_attn",
    )(q.reshape(b, s, qw), k.reshape(b, s, qw), v.reshape(b, s, qw))
    return o.reshape(b * s, vw)


def _split3(x):
    hi = x.astype(BF16)
    r1 = x - hi.astype(F32)
    mid = r1.astype(BF16)
    lo = (r1 - mid.astype(F32)).astype(BF16)
    return hi, mid, lo


def _rwkv_kernel(*refs, vres, nchunk):
    c = D_CHUNK
    tr = nchunk * c
    if vres:
        (dp_ref, vf_ref, mu_ref, w0_ref, w2_ref, a0_ref, a2_ref, v0_ref, v2_ref,
         kk_ref, ka_ref, rk_ref, lw_ref, lb_ref, ones_ref, out_ref,
         state_ref, carry_ref, y_ref) = refs
    else:
        (dp_ref, mu_ref, w0_ref, w2_ref, a0_ref, a2_ref,
         kk_ref, ka_ref, rk_ref, lw_ref, lb_ref, ones_ref, out_ref, vf_out,
         state_ref, carry_ref, y_ref) = refs

    @pl.when(pl.program_id(1) == 0)
    def _():
        state_ref[...] = jnp.zeros_like(state_ref)
        carry_ref[...] = jnp.zeros_like(carry_ref)

    hcur = dp_ref[...].astype(F32)
    row = lax.broadcasted_iota(jnp.int32, (tr, 1), 0)
    prev = jnp.where(row == 0, carry_ref[...], pltpu.roll(hcur, 1, 0))
    carry_ref[...] = hcur[tr - 1:tr, :]
    sh = hcur + (prev - hcur) * mu_ref[...]

    w = D_WIDTH
    r = sh[:, 0:w]
    k = sh[:, w:2 * w]
    v = sh[:, 2 * w:3 * w]
    wa = sh[:, 3 * w:3 * w + LANES]
    pre_w = w0_ref[...] + _dot(jnp.tanh(wa).astype(BF16), w2_ref[...])
    logd = -math.exp(-0.5) / (1.0 + jnp.exp(-pre_w))
    pre_a = a0_ref[...] + _dot(wa.astype(BF16), a2_ref[...])
    a = 1.0 / (1.0 + jnp.exp(-pre_a))
    if vres:
        vd = sh[:, 3 * w + LANES:3 * w + 2 * LANES]
        pre_v = v0_ref[...] + _dot(vd.astype(BF16), v2_ref[...])
        v = v + (vf_ref[...].astype(F32) - v) * (1.0 / (1.0 + jnp.exp(-pre_v)))
    else:
        vf_out[...] = v.astype(vf_out.dtype)
    def head_sum(t):
        t_hi = t.astype(BF16)
        t_lo = (t - t_hi.astype(F32)).astype(BF16)
        n = t.shape[1] // LANES
        stack = jnp.concatenate([piece[:, i * LANES:(i + 1) * LANES] for i in range(n) for piece in (t_hi, t_lo)], axis=0)
        res = _dot(stack, ones_ref[...])
        rows = t.shape[0]
        return jnp.concatenate([res[2 * i * rows:(2 * i + 1) * rows] + res[(2 * i + 1) * rows:(2 * i + 2) * rows]
                                for i in range(n)], axis=1)

    kk = k * kk_ref[...]
    kk = kk / jnp.maximum(jnp.sqrt(head_sum(kk * kk)), 1e-12)
    k2 = k * (1.0 + (a - 1.0) * ka_ref[...])

    ti = lax.broadcasted_iota(jnp.int32, (tr, tr), 0)
    tj = lax.broadcasted_iota(jnp.int32, (tr, tr), 1)
    tri = jnp.where((tj <= ti) & (tj >= ti // c * c), 1.0, 0.0).astype(BF16)
    hi, mid, lo = _split3(logd)
    cum = _dot(tri, hi) + _dot(tri, mid) + _dot(tri, lo)
    p_inc = jnp.exp(cum)
    p_exc = jnp.where(row % c == 0, 1.0, pltpu.roll(p_inc, 1, 0))
    p_inv = jnp.exp(-cum)
    p_all = [p_inc[(ci + 1) * c - 1:(ci + 1) * c, :] for ci in range(nchunk)]
    p_all_rows = jnp.concatenate([jnp.broadcast_to(pa, (c, D_WIDTH)) for pa in p_all], axis=0)
    kb = kk * a * p_inv
    kt = k2 * p_inv
    at_b = (-kk * p_exc).astype(BF16)
    rt_b = (r * p_inc).astype(BF16)
    bend_b = (kb * p_all_rows).astype(BF16)
    kend_b = (kt * p_all_rows).astype(BF16)
    vb = v.astype(BF16)

    pc = lax.broadcasted_iota(jnp.int32, (c, LANES), 1) % D_HEAD
    pr = lax.broadcasted_iota(jnp.int32, (c, LANES), 0)
    strict = pc < pr
    incl = pc <= pr
    eye = jnp.where(pc == pr, 1.0, 0.0)
    lo_lane = lax.broadcasted_iota(jnp.int32, (c, LANES), 1) < D_HEAD
    sr = lax.broadcasted_iota(jnp.int32, (LANES, LANES), 0) < D_HEAD
    sc = lax.broadcasted_iota(jnp.int32, (LANES, LANES), 1) < D_HEAD
    same_head = sr == sc

    def pair_diag(t):
        return jnp.concatenate([jnp.where(lo_lane, t, 0.0), jnp.where(lo_lane, 0.0, t)], axis=0).astype(BF16)

    items = [(ci, p) for ci in range(nchunk) for p in range(D_HEADS // 2)]
    rsl = lambda ci: slice(ci * c, (ci + 1) * c)
    psl = lambda p: slice(p * LANES, (p + 1) * LANES)
    sub = lambda arr, it: arr[rsl(it[0]), psl(it[1])]
    lhs = {it: jnp.concatenate([sub(at_b, it), sub(rt_b, it)], axis=0) for it in items}
    mats = {it: _dot_nt(lhs[it], jnp.concatenate([pair_diag(sub(kb, it)), pair_diag(sub(kt, it))], axis=0))
            for it in items}
    r_b = {it: jnp.where(incl, mats[it][c:, :LANES], 0.0).astype(BF16) for it in items}
    x0 = {it: jnp.where(strict, mats[it][:c, :LANES], 0.0) for it in items}
    xp = {it: _dot(x0[it].astype(BF16), pair_diag(x0[it])) for it in items}
    tinv = {it: eye + x0[it] for it in items}
    n_levels = int(math.log2(c))
    for lvl in range(1, n_levels - 1):
        prod = {it: _dot(jnp.concatenate([xp[it], tinv[it]], axis=0).astype(BF16), pair_diag(xp[it])) for it in items}
        xp = {it: prod[it][:c] for it in items}
        tinv = {it: tinv[it] + prod[it][c:] for it in items}
    tinv = {it: (tinv[it] + _dot(tinv[it].astype(BF16), pair_diag(xp[it]))).astype(BF16) for it in items}
    m_k = {it: jnp.concatenate([jnp.where(strict, mats[it][:c, LANES:], 0.0),
                                jnp.where(incl, mats[it][c:, LANES:], 0.0)], axis=0).astype(BF16) for it in items}
    from_v = {it: _dot(m_k[it], pair_diag(sub(v, it))) for it in items}
    upd_v = {it: jnp.where(same_head, _dot_tn(sub(vb, it), sub(kend_b, it)), 0.0) for it in items}

    state = [state_ref[p] for p in range(D_HEADS // 2)]
    pairs = range(D_HEADS // 2)
    for ci in range(nchunk):
        from_state = [_dot_nt(lhs[ci, p], state[p].astype(BF16)) for p in pairs]
        u = [_dot(tinv[ci, p], pair_diag(from_state[p][:c] + from_v[ci, p][:c])) for p in pairs]
        upd_u = [_dot_tn(u[p].astype(BF16), sub(bend_b, (ci, p))) for p in pairs]
        state = [state[p] * p_all[ci][:, psl(p)] + jnp.where(same_head, upd_u[p], 0.0) + upd_v[ci, p] for p in pairs]
        for p in pairs:
            y_ref[rsl(ci), psl(p)] = from_state[p][c:] + from_v[ci, p][c:] + _dot(r_b[ci, p], pair_diag(u[p]))
    for p in pairs:
        state_ref[p] = state[p]

    y = y_ref[...]
    yc = y - head_sum(y) * (1.0 / D_HEAD)
    var = head_sum(yc * yc) * (1.0 / D_HEAD)
    yn = yc * lax.rsqrt(var + RWKV_LN_EPS) * lw_ref[...] + lb_ref[...]
    bonus = head_sum(r * k2 * rk_ref[...]) * v
    out_ref[...] = (yn + bonus).astype(out_ref.dtype)


def _rwkv(dproj, v_first, p, b, s):
    tr = D_STEP_CHUNKS * D_CHUNK
    vres = v_first is not None
    row = lambda wd: pl.BlockSpec((None, tr, wd), lambda bi, i: (bi, i, 0))
    full = lambda a: pl.BlockSpec(a.shape, lambda bi, i: (0, 0))
    ins = [dproj.reshape(b, s, D_PROJ_PAD)]
    in_specs = [row(D_PROJ_PAD)]
    if vres:
        ins.append(v_first.reshape(b, s, D_WIDTH))
        in_specs.append(row(D_WIDTH))
    names = ['mu', 'w0', 'w2', 'a0', 'a2'] + (['v0', 'v2'] if vres else []) + ['k_k', 'k_a', 'r_k', 'lnx_w', 'lnx_b']
    for nm in names:
        ins.append(p[nm])
        in_specs.append(full(p[nm]))
    head_id = np.arange(LANES) // D_HEAD
    ones_bd = jnp.asarray(head_id[:, None] == head_id[None, :], BF16)
    ins.append(ones_bd)
    in_specs.append(full(ones_bd))
    out_shape = [jax.ShapeDtypeStruct((b, s, D_WIDTH), BF16)]
    out_specs = [row(D_WIDTH)]
    if not vres:
        out_shape.append(jax.ShapeDtypeStruct((b, s, D_WIDTH), BF16))
        out_specs.append(row(D_WIDTH))
    res = pl.pallas_call(
        functools.partial(_rwkv_kernel, vres=vres, nchunk=D_STEP_CHUNKS), grid=(b, s // tr),
        in_specs=in_specs, out_specs=out_specs, out_shape=out_shape,
        scratch_shapes=[pltpu.VMEM((D_HEADS // 2, LANES, LANES), F32),
                        pltpu.VMEM((1, D_PROJ_PAD), F32),
                        pltpu.VMEM((tr, D_WIDTH), F32)],
        compiler_params=_cparams("parallel", "arbitrary"), name="rwkv7",
    )(*ins)
    d_out = res[0].reshape(b * s, D_WIDTH)
    vf = v_first if vres else res[1].reshape(b * s, D_WIDTH)
    return d_out, vf


def _rope_tables(positions, dim, theta, period, base, half):
    inv = jnp.exp(-math.log(theta) * jnp.arange(0, dim, 2, dtype=F32) / dim)
    ang = positions.astype(F32)[..., None] * inv
    cos, sin = jnp.cos(ang), jnp.sin(ang)
    lead = ang.shape[:-1]
    fill = lambda width, val: jnp.full(lead + (width,), val, F32)
    rest = period - base - 2 * half

    def lanes(before, lo, hi, after):
        one_period = jnp.concatenate([before, lo, hi, after], axis=-1)
        return jnp.tile(one_period, (1, 1, LANES // period))

    c = lanes(fill(base, 1.0), cos, cos, fill(rest, 1.0))
    s1 = lanes(fill(base, 0.0), fill(half, 0.0), sin, fill(rest, 0.0))
    s2 = lanes(fill(base, 0.0), -sin, fill(half, 0.0), fill(rest, 0.0))
    n = positions.shape[0] * positions.shape[1]
    return tuple(t.reshape(n, LANES) for t in (c, s1, s2))


def _even_layer(x2d, b, s, mem_kv, tabs_a, tabs_r, norm, w_in, w_out, final_gain):
    q_scale = HEAD_DIM ** -0.5 * LOG2E
    aw = A_WIDTH
    plan = [(0, aw, 0, 0, (0, ROT_DIM // 2, q_scale), 0), (aw, aw, 0, aw, (0, ROT_DIM // 2, 1.0), aw),
            (2 * aw, aw, 0, 2 * aw, None, 2 * aw),
            (1536, 512, 0, 1536, (1, B_QK_DIM // 2, 1.0)), (2048, 512, 0, 2048, None),
            (2560, 256, 1, 0, (None, 0, MEM_Q_SCALE)), (2816, 640, 2, 0, None), (3456, 640, 2, 640, None)]
    sub_dils = tuple(d for d in A_DILATIONS if d > 1)
    res = _norm_proj(x2d, norm, w_in.astype(BF16), plan, (2560, 256, MIX_WIDTH), (tabs_a, tabs_r),
                     dils=sub_dils, sub_width=3 * aw, seq=s)
    main, qm, gate = res[:3]
    main3 = main.reshape(b, s, 2560)
    qkv = {1: main3.reshape(b, 1, s, 2560)}
    qkv.update(zip(sub_dils, res[3:]))
    outs, lses = [], []
    for dil in A_DILATIONS:
        o, lse = _band_attn(qkv[dil], 0, dil)
        outs.append(o)
        lses.append(lse)
    r_out = _retention(main3)
    m_out = _mem_attn(qm, mem_kv, b, s)
    return _out_proj((outs, lses), r_out, m_out, gate, x2d, w_out.astype(BF16), final_gain, seq=s)


def _odd_layer(x2d, b, s, mem_kv, tabs_m, v_first, final_gain, norm, w_in, q_norm, w_qb, kv_norm, w_kvb,
               mu_shift, w0, w2, a0, a2, v0, v2, k_k, k_a, r_k, lnx_w, lnx_b, w_out):
    vres = v0 is not None
    d_cols = mu_shift.shape[0]
    k_dim = w_in.shape[0]
    z = lambda n: jnp.zeros((k_dim, n), F32)
    o = 0
    cq = w_in[:, o:o + C_Q_RANK]; o += C_Q_RANK
    ckv = w_in[:, o:o + C_KV_RANK]; o += C_KV_RANK
    kro = w_in[:, o:o + C_ROPE]; o += C_ROPE
    dp = w_in[:, o:o + d_cols]; o += d_cols
    qm_w = w_in[:, o:o + MEM_WIDTH]; o += MEM_WIDTH
    gate_w = w_in[:, o:]
    dp_pad = jnp.concatenate([dp, z(D_PROJ_PAD - d_cols)], axis=1)
    swap = lambda t: jnp.concatenate([t[..., C_ROPE // 2:], t[..., :C_ROPE // 2]], axis=-1)
    rope_slab = lambda t: jnp.concatenate([z(C_NOPE), t, z(LANES - C_NOPE - C_ROPE)], axis=1)
    w_pad = jnp.concatenate([cq, ckv, rope_slab(kro), rope_slab(swap(kro)), dp_pad, qm_w, gate_w], axis=1)
    m0 = C_Q_RANK + C_KV_RANK + 2 * LANES
    plan = [(0, m0, 0, 0, None), (m0, 896, 1, 0, None), (m0 + 896, 896, 1, 896, None),
            (m0 + 1792, 256, 2, 0, (None, 0, MEM_Q_SCALE)),
            (m0 + 2048, 640, 3, 0, None), (m0 + 2688, 640, 3, 640, None)]
    mla_in, dproj, qm, gate = _norm_proj(x2d, norm, w_pad.astype(BF16), plan, (m0, D_PROJ_PAD, 256, MIX_WIDTH))

    wq = w_qb.reshape(C_Q_RANK, C_HEADS, C_NOPE + C_ROPE)
    slab_pad = ((0, 0), (0, 0), (0, LANES - C_NOPE - C_ROPE))
    wq_sw = jnp.concatenate([wq[..., :C_NOPE], swap(wq[..., C_NOPE:])], axis=-1)
    wq_sw = jnp.pad(wq_sw, slab_pad).reshape(C_Q_RANK, C_HEADS * LANES)
    wq = jnp.pad(wq, slab_pad).reshape(C_Q_RANK, C_HEADS * LANES)
    wkv = w_kvb.reshape(C_KV_RANK, C_HEADS, C_NOPE + C_V)
    wk = jnp.pad(wkv[:, :, :C_NOPE], ((0, 0), (0, 0), (0, LANES - C_NOPE))).reshape(C_KV_RANK, C_HEADS * LANES)
    wv = jnp.pad(wkv[:, :, C_NOPE:], ((0, 0), (0, 0), (0, LANES - C_V))).reshape(C_KV_RANK, C_HEADS * LANES)
    q, k, v = _mla_prep(mla_in, q_norm, kv_norm, wq.astype(BF16), wq_sw.astype(BF16), wk.astype(BF16),
                        wv.astype(BF16), tabs_m)
    c_out = _mla_attn(q, k, v, b, s)

    row = lambda t: t.reshape(1, -1).astype(F32)
    pad_rows = lambda t, top, n: jnp.concatenate(
        [jnp.zeros((top, t.shape[1]), F32), t, jnp.zeros((n - top - t.shape[0], t.shape[1]), F32)], axis=0)
    p = dict(mu=row(jnp.concatenate([mu_shift, jnp.zeros((D_PROJ_PAD - d_cols,), F32)])),
             w0=row(w0), w2=pad_rows(w2, 0, LANES).astype(BF16),
             a0=row(a0), a2=pad_rows(a2, D_DECAY_LORA, LANES).astype(BF16),
             k_k=row(k_k), k_a=row(k_a), r_k=row(r_k), lnx_w=row(lnx_w), lnx_b=row(lnx_b))
    if vres:
        p['v0'] = row(v0)
        p['v2'] = pad_rows(v2, 0, LANES).astype(BF16)
    d_out, v_first = _rwkv(dproj, v_first, p, b, s)
    m_out = _mem_attn(qm, mem_kv, b, s)
    x_new = _out_proj(c_out, d_out, m_out, gate, x2d, w_out.astype(BF16), final_gain)
    return x_new, v_first


def kernel(x, mem, positions, mem_norm, final_norm, l0_norm, l0_w_in, l0_w_mem_kv, l0_w_out, l1_norm, l1_w_in, l1_q_norm, l1_w_qb, l1_kv_norm, l1_w_kvb, l1_mu_shift, l1_w0, l1_w2, l1_a0, l1_a2, l1_k_k, l1_k_a, l1_r_k, l1_lnx_w, l1_lnx_b, l1_w_mem_kv, l1_w_out, l2_norm, l2_w_in, l2_w_mem_kv, l2_w_out, l3_norm, l3_w_in, l3_q_norm, l3_w_qb, l3_kv_norm, l3_w_kvb, l3_mu_shift, l3_w0, l3_w2, l3_a0, l3_a2, l3_v0, l3_v2, l3_k_k, l3_k_a, l3_r_k, l3_lnx_w, l3_lnx_b, l3_w_mem_kv, l3_w_out):
    b, s, d = x.shape
    x2d = x.reshape(b * s, d)
    mem2d = mem.reshape(b * MEM_TOKENS, d)
    tabs_a = _rope_tables(positions, ROT_DIM, ROPE_THETA, HEAD_DIM, 0, ROT_DIM // 2)
    tabs_r = _rope_tables(positions, B_QK_DIM, RET_THETA, B_QK_DIM, 0, B_QK_DIM // 2)
    tabs_m = _rope_tables(positions, C_ROPE, ROPE_THETA, LANES, C_NOPE, C_ROPE // 2)

    def mem_kv(w):
        plan = [(0, 2 * MEM_WIDTH, 0, 0, None)]
        return _norm_proj(mem2d, mem_norm, w.astype(BF16), plan, (2 * MEM_WIDTH,))[0]

    x2d = _even_layer(x2d, b, s, mem_kv(l0_w_mem_kv), tabs_a, tabs_r, l0_norm, l0_w_in, l0_w_out, None)
    x2d, v_first = _odd_layer(x2d, b, s, mem_kv(l1_w_mem_kv), tabs_m, None, None, l1_norm, l1_w_in, l1_q_norm,
                              l1_w_qb, l1_kv_norm, l1_w_kvb, l1_mu_shift, l1_w0, l1_w2, l1_a0, l1_a2, None, None,
                              l1_k_k, l1_k_a, l1_r_k, l1_lnx_w, l1_lnx_b, l1_w_out)
    x2d = _even_layer(x2d, b, s, mem_kv(l2_w_mem_kv), tabs_a, tabs_r, l2_norm, l2_w_in, l2_w_out, None)
    x2d, _ = _odd_layer(x2d, b, s, mem_kv(l3_w_mem_kv), tabs_m, v_first, final_norm, l3_norm, l3_w_in, l3_q_norm,
                        l3_w_qb, l3_kv_norm, l3_w_kvb, l3_mu_shift, l3_w0, l3_w2, l3_a0, l3_a2, l3_v0, l3_v2,
                        l3_k_k, l3_k_a, l3_r_k, l3_lnx_w, l3_lnx_b, l3_w_out)
    return x2d.reshape(b, s, d)
```

```python
import functools
import math

import numpy as np
import jax
import jax.numpy as jnp
from jax import lax
from jax.experimental import pallas as pl
from jax.experimental.pallas import tpu as pltpu

F32 = jnp.float32
BF16 = jnp.bfloat16

LANES = 128
VMEM_LIMIT = 48 * 1024 * 1024

D_MODEL = 1024
HEAD_DIM = 64
ROPE_THETA = 500000.0
ROT_DIM = HEAD_DIM // 4
NORM_EPS = 1e-6
NEG_INF = -1e30
LN2 = math.log(2.0)
LOG2E = 1.0 / LN2

A_HEADS = 8
A_WIDTH = A_HEADS * HEAD_DIM
A_PATTERNS = ((128, 1), (512, 4), (2048, 16))
A_BLOCK = 128
A_DILATIONS = tuple(d for (_, d) in A_PATTERNS)
A_LSE_GROUP = HEAD_DIM // (A_HEADS // 2)

B_HEADS = 4
B_QK_DIM = 64
B_V_DIM = 128
B_CHUNK = 128
RET_THETA = 10000.0

MEM_TOKENS = 256
MEM_HEADS = 4
MEM_WIDTH = MEM_HEADS * HEAD_DIM
MEM_Q_SCALE = HEAD_DIM ** -0.5 / math.log(2.0)

C_HEADS = 8
C_Q_RANK = 256
C_KV_RANK = 128
C_NOPE = 64
C_ROPE = 32
C_V = 64

D_HEADS = 8
D_HEAD = 64
D_WIDTH = D_HEADS * D_HEAD
D_DECAY_LORA = 64
D_AAA_LORA = 64
D_MV_LORA = 32
RWKV_LN_EPS = 64e-5
D_CHUNK = 64
D_STEP_CHUNKS = 4
D_PROJ_PAD = 3 * D_WIDTH + 2 * LANES

MIX_WIDTH = 1280


def _cparams(*sem):
    return pltpu.CompilerParams(dimension_semantics=sem, vmem_limit_bytes=VMEM_LIMIT)


def _dot(a, b):
    return jnp.dot(a, b, preferred_element_type=F32)


def _dot_nt(a, b):
    return lax.dot_general(a, b, (((1,), (1,)), ((), ())), preferred_element_type=F32)


def _dot_tn(a, b):
    return lax.dot_general(a, b, (((0,), (0,)), ((), ())), preferred_element_type=F32)


def _rot_slab(x, c, s1, s2, half):
    return x * c + pltpu.roll(x, half, 1) * s1 + pltpu.roll(x, LANES - half, 1) * s2


def _rotary(x, c, s1, s2, half):
    slabs = [_rot_slab(x[:, i:i + LANES], c, s1, s2, half) for i in range(0, x.shape[1], LANES)]
    return slabs[0] if len(slabs) == 1 else jnp.concatenate(slabs, axis=1)


def _norm_proj_kernel(*refs, plan, n_tab, n_out, dils):
    x_ref, g_ref, w_ref = refs[:3]
    tabs = refs[3:3 + 3 * n_tab]
    outs = refs[3 + 3 * n_tab:3 + 3 * n_tab + n_out]
    sub_outs = refs[3 + 3 * n_tab + n_out:3 + 3 * n_tab + n_out + len(dils)]
    stage_ref = refs[-1] if dils else None
    _norm_proj_body(x_ref[...].astype(F32), g_ref, w_ref, tabs, outs, sub_outs, stage_ref, plan, dils)


def _norm_proj_body(x, g_ref, w_ref, tabs, outs, sub_outs, stage_ref, plan, dils):
    tm = x.shape[0]
    ms = jnp.mean(x * x, axis=-1, keepdims=True)
    xn = (x * lax.rsqrt(ms + NORM_EPS) * g_ref[...]).astype(BF16)
    n_staged = [0]

    def epilogue(entry, acc):
        (_, width, oi, oc, rot, sub_col) = entry
        if rot is not None:
            t, half, post_scale = rot
            if t is not None:
                acc = _rotary(acc, tabs[3 * t][...], tabs[3 * t + 1][...], tabs[3 * t + 2][...], half)
            if post_scale != 1.0:
                acc = acc * post_scale
        outs[oi][:, oc:oc + width] = acc.astype(outs[oi].dtype)
        if sub_col is not None:
            slot = n_staged[0]
            n_staged[0] += 1
            for c in range(width // LANES):
                stage_ref[slot, c] = acc[:, c * LANES:(c + 1) * LANES]
            for d, sub in zip(dils, sub_outs):
                for r in range(d):
                    for c in range(width // LANES):
                        col = sub_col + c * LANES
                        sub[r, :, col:col + LANES] = stage_ref[slot, c, pl.ds(r, tm // d, stride=d), :].astype(sub.dtype)

    pending = None
    for entry in plan:
        acc = _dot(xn, w_ref[:, entry[0]:entry[0] + entry[1]])
        if pending is not None:
            epilogue(*pending)
        pending = (entry, acc)
    epilogue(*pending)


def _norm_proj(x2d, g, w, plan, out_widths, tables=(), tm=256, dils=(), sub_width=0, seq=None):
    t_rows, k = x2d.shape
    assert t_rows % tm == 0
    proj = _proj_spec(g, w, plan, out_widths, tables, dils, sub_width)
    ins, in_specs, out_specs, out_shape, scratch = _proj_operands(proj, t_rows, tm, seq)
    kern = functools.partial(_norm_proj_kernel, plan=proj['plan'], n_tab=len(tables), n_out=len(out_widths),
                             dils=tuple(dils))
    return pl.pallas_call(
        kern, grid=(t_rows // tm,), in_specs=[pl.BlockSpec((tm, k), lambda i: (i, 0))] + in_specs,
        out_specs=out_specs, out_shape=out_shape,
        scratch_shapes=scratch, compiler_params=_cparams("parallel"), name="norm_proj",
    )(x2d, *ins)


def _proj_spec(g, w, plan, out_widths, tables=(), dils=(), sub_width=0):
    plan = tuple(tuple(p) + (None,) * (6 - len(p)) for p in plan)
    return dict(g=g, w=w, plan=plan, out_widths=tuple(out_widths), tables=tuple(tables), dils=tuple(dils),
                sub_width=sub_width)


def _proj_operands(proj, t_rows, tm, seq):
    k, n = proj['w'].shape
    flat_tabs = [t for tab in proj['tables'] for t in tab]
    ins = [proj['g'].reshape(1, k).astype(F32), proj['w']] + flat_tabs
    in_specs = [pl.BlockSpec((1, k), lambda i: (0, 0)), pl.BlockSpec((k, n), lambda i: (0, 0))]
    in_specs += [pl.BlockSpec((tm, LANES), lambda i: (i, 0)) for _ in flat_tabs]
    out_specs = [pl.BlockSpec((tm, ow), lambda i: (i, 0)) for ow in proj['out_widths']]
    out_shape = [jax.ShapeDtypeStruct((t_rows, ow), BF16) for ow in proj['out_widths']]
    scratch = []
    if proj['dils']:
        nt = seq // tm
        sub_width = proj['sub_width']
        for d in proj['dils']:
            out_specs.append(pl.BlockSpec((None, d, tm // d, sub_width), lambda i: (i // nt, 0, i % nt, 0)))
            out_shape.append(jax.ShapeDtypeStruct((t_rows // seq, d, seq // d, sub_width), BF16))
        staged = [p[1] for p in proj['plan'] if p[5] is not None]
        scratch = [pltpu.VMEM((len(staged), max(staged) // LANES, tm, LANES), F32)]
    return ins, in_specs, out_specs, out_shape, scratch


def _band_attn_kernel(q_ref, kp_ref, kc_ref, vp_ref, vc_ref, o_ref, lse_ref, s_ref, p_ref, m_ref, *, nq):
    i = pl.program_id(2)
    qi = lax.broadcasted_iota(jnp.int32, (A_BLOCK, A_BLOCK), 0)
    kj = lax.broadcasted_iota(jnp.int32, (A_BLOCK, A_BLOCK), 1)
    bias_prev = jnp.where(kj >= qi, 0.0, NEG_INF)
    bias_cur = jnp.where(kj <= qi, 0.0, NEG_INF)
    bias_first = bias_prev + jnp.where(i > 0, 0.0, NEG_INF)
    lane = lax.broadcasted_iota(jnp.int32, (A_BLOCK, LANES), 1)
    lo = lane < HEAD_DIM
    keep = (jnp.where(lo, 1.0, 0.0).astype(BF16), jnp.where(lo, 0.0, 1.0).astype(BF16))
    npair = A_HEADS // 2

    def operands(u, pair):
        sl = slice(pair * LANES, (pair + 1) * LANES)
        rows = slice(u * A_BLOCK, (u + 1) * A_BLOCK)
        if u == 0:
            return rows, sl, kp_ref[:, sl], vp_ref[:, sl], bias_first
        prow = slice((u - 1) * A_BLOCK, u * A_BLOCK)
        return rows, sl, kc_ref[prow, sl], vc_ref[prow, sl], bias_prev

    for u in range(nq):
        for pair in range(npair):
            rows, sl, kp, _, bp = operands(u, pair)
            q = q_ref[rows, sl]
            kc = kc_ref[rows, sl]
            for half in range(2):
                idx = (u * npair + pair) * 2 + half
                qh = q * keep[half]
                s_ref[idx, :, :A_BLOCK] = _dot_nt(qh, kp) + bp
                s_ref[idx, :, A_BLOCK:] = _dot_nt(qh, kc) + bias_cur
    for idx in range(nq * A_HEADS):
        s = s_ref[idx]
        m = jnp.max(s, -1, keepdims=True)
        p_ref[idx] = jnp.exp2(s - m).astype(BF16)
        m_ref[idx] = jnp.broadcast_to(m, (A_BLOCK, LANES))
    for u in range(nq):
        lse = None
        for pair in range(npair):
            rows, sl, _, vp, _ = operands(u, pair)
            vc = vc_ref[rows, sl]
            idx0 = (u * npair + pair) * 2
            o = [_dot(p_ref[idx0 + half, :, :A_BLOCK], vp * keep[half] + keep[1 - half])
                 + _dot(p_ref[idx0 + half, :, A_BLOCK:], vc * keep[half] + keep[1 - half]) for half in range(2)]
            den = pltpu.roll(jnp.where(lo, o[1], o[0]), HEAD_DIM, 1)
            o_ref[rows, sl] = (jnp.where(lo, o[0], o[1]) / den).astype(o_ref.dtype)
            lse_pair = jnp.where(lo, m_ref[idx0], m_ref[idx0 + 1]) * LN2 + jnp.log(den)
            lse = lse_pair if lse is None else jnp.where(lane % HEAD_DIM // A_LSE_GROUP == pair, lse_pair, lse)
        lse_ref[rows, :] = lse


def _single_block_attn_kernel(q_ref, k_ref, v_ref, o_ref, lse_ref, s_ref, p_ref, m_ref, *, nres):
    qi = lax.broadcasted_iota(jnp.int32, (A_BLOCK, A_BLOCK), 0)
    kj = lax.broadcasted_iota(jnp.int32, (A_BLOCK, A_BLOCK), 1)
    bias = jnp.where(kj <= qi, 0.0, NEG_INF)
    lane = lax.broadcasted_iota(jnp.int32, (A_BLOCK, LANES), 1)
    lo = lane < HEAD_DIM
    keep = (jnp.where(lo, 1.0, 0.0).astype(BF16), jnp.where(lo, 0.0, 1.0).astype(BF16))
    npair = A_HEADS // 2
    for r in range(nres):
        for pair in range(npair):
            sl = slice(pair * LANES, (pair + 1) * LANES)
            for half in range(2):
                s_ref[(r * npair + pair) * 2 + half] = _dot_nt(q_ref[r, :, sl] * keep[half], k_ref[r, :, sl]) + bias
    for idx in range(nres * A_HEADS):
        s = s_ref[idx]
        m = jnp.max(s, -1, keepdims=True)
        p_ref[idx] = jnp.exp2(s - m).astype(BF16)
        m_ref[idx] = jnp.broadcast_to(m, (A_BLOCK, LANES))
    for r in range(nres):
        lse = None
        for pair in range(npair):
            sl = slice(pair * LANES, (pair + 1) * LANES)
            idx0 = (r * npair + pair) * 2
            v = v_ref[r, :, sl]
            o = [_dot(p_ref[idx0 + half], v * keep[half] + keep[1 - half]) for half in range(2)]
            den = pltpu.roll(jnp.where(lo, o[1], o[0]), HEAD_DIM, 1)
            o_ref[r, :, sl] = (jnp.where(lo, o[0], o[1]) / den).astype(o_ref.dtype)
            lse_pair = jnp.where(lo, m_ref[idx0], m_ref[idx0 + 1]) * LN2 + jnp.log(den)
            lse = lse_pair if lse is None else jnp.where(lane % HEAD_DIM // A_LSE_GROUP == pair, lse_pair, lse)
        lse_ref[r] = lse


def _single_block_attn(qkv, col0, dilation, nres=4):
    b, d, sub_len, _ = qkv.shape
    assert sub_len == A_BLOCK and d % nres == 0
    spec = lambda col, w: pl.BlockSpec((None, nres, A_BLOCK, w), lambda bi, r: (bi, r, 0, col))
    return pl.pallas_call(
        functools.partial(_single_block_attn_kernel, nres=nres), grid=(b, d // nres),
        in_specs=[spec(col0, A_WIDTH), spec(col0 + 1, A_WIDTH), spec(col0 + 2, A_WIDTH)],
        out_specs=[spec(0, A_WIDTH), spec(0, LANES)],
        out_shape=[jax.ShapeDtypeStruct((b, d, sub_len, A_WIDTH), BF16),
                   jax.ShapeDtypeStruct((b, d, sub_len, LANES), F32)],
        scratch_shapes=[pltpu.VMEM((nres * A_HEADS, A_BLOCK, A_BLOCK), F32),
                        pltpu.VMEM((nres * A_HEADS, A_BLOCK, A_BLOCK), BF16),
                        pltpu.VMEM((nres * A_HEADS, A_BLOCK, LANES), F32)],
        compiler_params=_cparams("parallel", "parallel"), name="band_attn_d%d" % dilation,
    )(qkv, qkv, qkv)


def _band_attn(qkv, col0, dilation):
    b, d, sub_len, _ = qkv.shape
    assert d == dilation
    if sub_len == A_BLOCK:
        return _single_block_attn(qkv, col0, dilation)
    nq = 2 if sub_len % (2 * A_BLOCK) == 0 else 1
    tq = nq * A_BLOCK

    def cur(col):
        return pl.BlockSpec((None, None, tq, A_WIDTH), lambda bi, r, i: (bi, r, i, col0 + col))

    def prev(col):
        return pl.BlockSpec((None, None, A_BLOCK, A_WIDTH),
                            lambda bi, r, i: (bi, r, jnp.maximum(i * nq - 1, 0), col0 + col))

    return pl.pallas_call(
        functools.partial(_band_attn_kernel, nq=nq), grid=(b, d, sub_len // tq),
        in_specs=[cur(0), prev(1), cur(1), prev(2), cur(2)],
        out_specs=[pl.BlockSpec((None, None, tq, A_WIDTH), lambda bi, r, i: (bi, r, i, 0)),
                   pl.BlockSpec((None, None, tq, LANES), lambda bi, r, i: (bi, r, i, 0))],
        out_shape=[jax.ShapeDtypeStruct((b, d, sub_len, A_WIDTH), BF16),
                   jax.ShapeDtypeStruct((b, d, sub_len, LANES), F32)],
        scratch_shapes=[pltpu.VMEM((nq * A_HEADS, A_BLOCK, 2 * A_BLOCK), F32),
                        pltpu.VMEM((nq * A_HEADS, A_BLOCK, 2 * A_BLOCK), BF16),
                        pltpu.VMEM((nq * A_HEADS, A_BLOCK, LANES), F32)],
        compiler_params=_cparams("parallel", "parallel", "parallel"), name="band_attn_d%d" % dilation,
    )(qkv, qkv, qkv, qkv, qkv)


def _retention_kernel(qk_ref, v_ref, o_ref, state_ref, *, nchunk):
    c = B_CHUNK
    @pl.when(pl.program_id(1) == 0)
    def _():
        state_ref[...] = jnp.zeros_like(state_ref)

    scale = B_QK_DIM ** -0.5
    qi = lax.broadcasted_iota(jnp.int32, (c, c), 0)
    kj = lax.broadcasted_iota(jnp.int32, (c, c), 1)
    diff = (qi - kj).astype(F32)
    row = lax.broadcasted_iota(jnp.int32, (c, 1), 0).astype(F32)
    lo = lax.broadcasted_iota(jnp.int32, (c, LANES), 1) < B_QK_DIM
    keep = (jnp.where(lo, 1.0, 0.0).astype(BF16), jnp.where(lo, 0.0, 1.0).astype(BF16))
    log_g = [math.log1p(-2.0 ** (-5.0 - h)) for h in range(B_HEADS)]
    d_in = [jnp.where(diff >= 0, jnp.exp(jnp.maximum(diff, 0.0) * lg), 0.0) * scale for lg in log_g]
    from_start = [jnp.exp((row + 1.0) * lg) for lg in log_g]
    to_end = [jnp.exp((c - 1.0 - row) * lg) * scale for lg in log_g]
    items = [(ci, h) for ci in range(nchunk) for h in range(B_HEADS)]
    rows = lambda ci: slice(ci * c, (ci + 1) * c)
    qsl = lambda h: slice(h // 2 * LANES, (h // 2 + 1) * LANES)
    ksl = lambda h: slice(B_HEADS * B_QK_DIM + h // 2 * LANES, B_HEADS * B_QK_DIM + (h // 2 + 1) * LANES)
    vsl = lambda h: slice(h * B_V_DIM, (h + 1) * B_V_DIM)
    qm = {(ci, h): qk_ref[rows(ci), qsl(h)] * keep[h % 2] for ci, h in items}
    scores = {(ci, h): (_dot_nt(qm[ci, h], qk_ref[rows(ci), ksl(h)]) * d_in[h]).astype(BF16) for ci, h in items}
    o_in = {(ci, h): _dot(scores[ci, h], v_ref[rows(ci), vsl(h)]) for ci, h in items}
    kd = {(ci, h): ((qk_ref[rows(ci), ksl(h)] * keep[h % 2]).astype(F32) * to_end[h]).astype(BF16) for ci, h in items}
    kv = {(ci, h): _dot_tn(kd[ci, h], v_ref[rows(ci), vsl(h)]) for ci, h in items}
    state = [state_ref[h] for h in range(B_HEADS)]
    for ci in range(nchunk):
        o = [o_in[ci, h] + _dot(qm[ci, h], state[h].astype(BF16)) * from_start[h] for h in range(B_HEADS)]
        state = [state[h] * math.exp(c * log_g[h]) + kv[ci, h] for h in range(B_HEADS)]
        for h in range(B_HEADS):
            on = o[h] * lax.rsqrt(jnp.mean(o[h] * o[h], -1, keepdims=True) + NORM_EPS)
            o_ref[rows(ci), vsl(h)] = on.astype(o_ref.dtype)
    for h in range(B_HEADS):
        state_ref[h] = state[h]


def _retention(main, nchunk=2):
    b, s, _ = main.shape
    blk = (None, nchunk * B_CHUNK, 512)
    o = pl.pallas_call(
        functools.partial(_retention_kernel, nchunk=nchunk), grid=(b, s // (nchunk * B_CHUNK)),
        in_specs=[pl.BlockSpec(blk, lambda bi, i: (bi, i, 3)), pl.BlockSpec(blk, lambda bi, i: (bi, i, 4))],
        out_specs=pl.BlockSpec(blk, lambda bi, i: (bi, i, 0)),
        out_shape=jax.ShapeDtypeStruct((b, s, B_HEADS * B_V_DIM), BF16),
        scratch_shapes=[pltpu.VMEM((B_HEADS, LANES, B_V_DIM), F32)],
        compiler_params=_cparams("parallel", "arbitrary"), name="retention",
    )(main, main)
    return o.reshape(b * s, B_HEADS * B_V_DIM)


def _mem_attn_kernel(q_ref, kv_ref, o_ref, s_ref, p_ref):
    tq = q_ref.shape[0]
    lo_q = lax.broadcasted_iota(jnp.int32, (tq, LANES), 1) < HEAD_DIM
    lo_k = lax.broadcasted_iota(jnp.int32, (MEM_TOKENS, LANES), 1) < HEAD_DIM
    keep_q = (jnp.where(lo_q, 1.0, 0.0).astype(BF16), jnp.where(lo_q, 0.0, 1.0).astype(BF16))
    keep_k = (jnp.where(lo_k, 1.0, 0.0).astype(BF16), jnp.where(lo_k, 0.0, 1.0).astype(BF16))
    npair = MEM_HEADS // 2
    for pair in range(npair):
        sl = slice(pair * LANES, (pair + 1) * LANES)
        for half in range(2):
            s_ref[2 * pair + half] = _dot_nt(q_ref[:, sl] * keep_q[half], kv_ref[:, sl])
    for idx in range(MEM_HEADS):
        s = s_ref[idx]
        p_ref[idx] = jnp.exp2(s - jnp.max(s, -1, keepdims=True)).astype(BF16)
    for pair in range(npair):
        sl = slice(pair * LANES, (pair + 1) * LANES)
        v = kv_ref[:, MEM_WIDTH + pair * LANES:MEM_WIDTH + (pair + 1) * LANES]
        o = [_dot(p_ref[2 * pair + half], v * keep_k[half] + keep_k[1 - half]) for half in range(2)]
        den = pltpu.roll(jnp.where(lo_q, o[1], o[0]), HEAD_DIM, 1)
        o_ref[:, sl] = (jnp.where(lo_q, o[0], o[1]) / den).astype(o_ref.dtype)


def _mem_attn(qm, kv, b, s, tq=512):
    o = pl.pallas_call(
        _mem_attn_kernel, grid=(b, s // tq),
        in_specs=[pl.BlockSpec((None, tq, MEM_WIDTH), lambda bi, i: (bi, i, 0)),
                  pl.BlockSpec((None, MEM_TOKENS, 2 * MEM_WIDTH), lambda bi, i: (bi, 0, 0))],
        out_specs=pl.BlockSpec((None, tq, MEM_WIDTH), lambda bi, i: (bi, i, 0)),
        out_shape=jax.ShapeDtypeStruct((b, s, MEM_WIDTH), BF16),
        scratch_shapes=[pltpu.VMEM((MEM_HEADS, tq, MEM_TOKENS), F32),
                        pltpu.VMEM((MEM_HEADS, tq, MEM_TOKENS), BF16)],
        compiler_params=_cparams("parallel", "parallel"), name="mem_attn",
    )(qm.reshape(b, s, MEM_WIDTH), kv.reshape(b, MEM_TOKENS, 2 * MEM_WIDTH))
    return o.reshape(b * s, MEM_WIDTH)


def _mixture(o_refs, l_refs, e_ref, stage_o, stage_l, dils, tm):
    n_slab = A_WIDTH // LANES
    lses = []
    for pi, d in enumerate(dils):
        if d == 1:
            lses.append(l_refs[pi][0])
        else:
            for r in range(d):
                stage_l[pi, pl.ds(r, tm // d, stride=d), :] = l_refs[pi][r]
                for c in range(n_slab):
                    stage_o[pi, c, pl.ds(r, tm // d, stride=d), :] = (
                        o_refs[pi][r, :, c * LANES:(c + 1) * LANES].astype(F32))
            lses.append(stage_l[pi])
    m = functools.reduce(jnp.maximum, lses)
    es = [jnp.exp(l - m) for l in lses]
    inv = 1.0 / functools.reduce(lambda a, b: a + b, es)
    first = None
    for pi, d in enumerate(dils):
        wgt = es[pi] * inv
        w_hi = wgt.astype(BF16)
        w_lo = (wgt - w_hi.astype(F32)).astype(BF16)
        w_full = _dot(w_hi, e_ref[...]) + _dot(w_lo, e_ref[...])
        if d == 1:
            o = o_refs[pi][0].astype(F32)
        else:
            o = jnp.concatenate([stage_o[pi, c] for c in range(n_slab)], axis=1)
        first = w_full * o if first is None else first + w_full * o
    return first


def _out_proj_kernel(*refs, dils, final, nxt):
    n_mix = len(dils)
    if n_mix:
        o_refs, l_refs, e_ref = refs[:n_mix], refs[n_mix:2 * n_mix], refs[2 * n_mix]
        refs = refs[2 * n_mix + 1:]
    else:
        first = refs[0][...].astype(F32)
        refs = refs[1:]
    second, third, gate_ref, x_ref, w_ref = refs[:5]
    refs = refs[5:]
    if final:
        g_ref = refs[0]
        refs = refs[1:]
    if nxt is not None:
        n_plan, n_tab, n_out, n_dils = nxt
        ng_ref, nw_ref = refs[:2]
        n_tabs = refs[2:2 + 3 * n_tab]
        refs = refs[2 + 3 * n_tab:]
    out_ref = refs[0]
    refs = refs[1:]
    if nxt is not None:
        n_outs, n_subs = refs[:n_out], refs[n_out:n_out + len(n_dils)]
        refs = refs[n_out + len(n_dils):]
    if n_mix:
        stage_o, stage_l = refs[:2]
        refs = refs[2:]
        first = _mixture(o_refs, l_refs, e_ref, stage_o, stage_l, dils, x_ref.shape[0])
    g = gate_ref[...].astype(F32)
    sg = g * (1.0 / (1.0 + jnp.exp(-g)))
    w1 = first.shape[1]
    w2 = w1 + second.shape[1]
    y1 = (first * sg[:, :w1]).astype(BF16)
    y2 = (second[...].astype(F32) * sg[:, w1:w2]).astype(BF16)
    y3 = (third[...].astype(F32) * sg[:, w2:]).astype(BF16)
    xn = x_ref[...] + _dot(y1, w_ref[:w1, :]) + _dot(y2, w_ref[w1:w2, :]) + _dot(y3, w_ref[w2:, :])
    if final:
        xn = xn * lax.rsqrt(jnp.mean(xn * xn, -1, keepdims=True) + NORM_EPS) * g_ref[...]
    out_ref[...] = xn
    if nxt is not None:
        _norm_proj_body(xn, ng_ref, nw_ref, n_tabs, n_outs, n_subs, refs[0] if n_dils else None, n_plan, n_dils)


def _lse_expansion():
    dst = np.arange(A_WIDTH)
    head = dst // HEAD_DIM
    src = (head % 2) * HEAD_DIM + (head // 2) * A_LSE_GROUP
    return jnp.asarray(np.arange(LANES)[:, None] == src[None, :], BF16)


def _out_proj(first, second, third, gate, x2d, w_out, final_gain=None, tm=256, seq=None, next_proj=None):
    t_rows, dm = x2d.shape
    row = lambda a: pl.BlockSpec((tm, a.shape[1]), lambda i: (i, 0))
    full = lambda a: pl.BlockSpec(a.shape, lambda i: (0, 0))
    dils, scratch = (), []
    if isinstance(first, tuple):
        o_list, l_list = first
        dils = tuple(o.shape[1] for o in o_list)
        nt = seq // tm
        sub = lambda a: pl.BlockSpec((None, a.shape[1], tm // a.shape[1], a.shape[3]),
                                     lambda i: (i // nt, 0, i % nt, 0))
        expand = _lse_expansion()
        ins = list(o_list) + list(l_list) + [expand]
        in_specs = [sub(a) for a in ins[:-1]] + [full(expand)]
        scratch = [pltpu.VMEM((len(dils), A_WIDTH // LANES, tm, LANES), F32),
                   pltpu.VMEM((len(dils), tm, LANES), F32)]
    else:
        ins, in_specs = [first], [row(first)]
    for a in (second, third, gate, x2d):
        ins.append(a)
        in_specs.append(row(a))
    ins.append(w_out)
    in_specs.append(full(w_out))
    if final_gain is not None:
        ins.append(final_gain.reshape(1, dm).astype(F32))
        in_specs.append(pl.BlockSpec((1, dm), lambda i: (0, 0)))
    out_specs = [pl.BlockSpec((tm, dm), lambda i: (i, 0))]
    out_shape = [jax.ShapeDtypeStruct((t_rows, dm), F32)]
    nxt = None
    if next_proj is not None:
        n_ins, n_in_specs, n_out_specs, n_out_shape, n_scratch = _proj_operands(next_proj, t_rows, tm, seq)
        ins += n_ins
        in_specs += n_in_specs
        out_specs += n_out_specs
        out_shape += n_out_shape
        scratch = scratch + n_scratch
        nxt = (next_proj['plan'], len(next_proj['tables']), len(next_proj['out_widths']), next_proj['dils'])
    kern = functools.partial(_out_proj_kernel, dils=dils, final=final_gain is not None, nxt=nxt)
    res = pl.pallas_call(
        kern, grid=(t_rows // tm,), in_specs=in_specs, out_specs=out_specs, out_shape=out_shape,
        scratch_shapes=scratch, compiler_params=_cparams("parallel"),
        name="out_proj" if nxt is None else "out_norm_proj",
    )(*ins)
    return res[0] if nxt is None else res


def _mla_prep_kernel(in_ref, qn_ref, kvn_ref, wq_ref, wqs_ref, wk_ref, wv_ref, c_ref, s1_ref, s2_ref,
                     q_out, k_out, v_out):
    c, s = c_ref[...], s1_ref[...] + s2_ref[...]
    k0 = C_Q_RANK + C_KV_RANK
    cq = in_ref[:, :C_Q_RANK].astype(F32)
    ckv = in_ref[:, C_Q_RANK:k0].astype(F32)
    kr = in_ref[:, k0:k0 + LANES].astype(F32)
    kr_sw = in_ref[:, k0 + LANES:].astype(F32)
    cqn = (cq * lax.rsqrt(jnp.mean(cq * cq, -1, keepdims=True) + NORM_EPS) * qn_ref[...]).astype(BF16)
    ckvn = (ckv * lax.rsqrt(jnp.mean(ckv * ckv, -1, keepdims=True) + NORM_EPS) * kvn_ref[...]).astype(BF16)
    q_scale = (C_NOPE + C_ROPE) ** -0.5 * LOG2E
    c_all = jnp.tile(c * q_scale, (1, C_HEADS))
    s_all = jnp.tile(s * q_scale, (1, C_HEADS))
    q_out[...] = (_dot(cqn, wq_ref[...]) * c_all + _dot(cqn, wqs_ref[...]) * s_all).astype(q_out.dtype)
    kpe = kr * c + kr_sw * s
    kn = _dot(ckvn, wk_ref[...])
    for h in range(C_HEADS):
        k_out[:, h * LANES:(h + 1) * LANES] = (kn[:, h * LANES:(h + 1) * LANES] + kpe).astype(k_out.dtype)
    lane = lax.broadcasted_iota(jnp.int32, (1, C_HEADS * LANES), 1)
    ones = jnp.where(lane % LANES >= C_V, 1.0, 0.0)
    v_out[...] = (_dot(ckvn, wv_ref[...]) + ones).astype(v_out.dtype)


def _mla_prep(mla_in, q_norm, kv_norm, wq, wq_sw, wk, wv, tabs, tm=256):
    t_rows = mla_in.shape[0]
    row = lambda w: pl.BlockSpec((tm, w), lambda i: (i, 0))
    full = lambda a: pl.BlockSpec(a.shape, lambda i: (0, 0))
    qn = q_norm.reshape(1, -1).astype(F32)
    kvn = kv_norm.reshape(1, -1).astype(F32)
    return pl.pallas_call(
        _mla_prep_kernel, grid=(t_rows // tm,),
        in_specs=[row(mla_in.shape[1]), full(qn), full(kvn), full(wq), full(wq_sw), full(wk), full(wv),
                  row(LANES), row(LANES), row(LANES)],
        out_specs=[row(C_HEADS * LANES)] * 3,
        out_shape=[jax.ShapeDtypeStruct((t_rows, C_HEADS * LANES), BF16)] * 3,
        compiler_params=_cparams("parallel"), name="mla_prep",
    )(mla_in, qn, kvn, wq, wq_sw, wk, wv, *tabs)


def _mla_attn_kernel(q_ref, k_ref, v_ref, o_ref, m_ref, acc_ref, s_ref, p_ref, alpha_ref, *, tq):
    i = pl.program_id(1)
    diag_ok = (lax.broadcasted_iota(jnp.int32, (tq, tq), 1) <= lax.broadcasted_iota(jnp.int32, (tq, tq), 0))
    m_ref[...] = jnp.full(m_ref.shape, NEG_INF, F32)
    acc_ref[...] = jnp.zeros(acc_ref.shape, F32)

    def block(j, masked):
        rows = pl.ds(pl.multiple_of(j * tq, tq), tq)
        for h in range(C_HEADS):
            sl = slice(h * LANES, (h + 1) * LANES)
            s = _dot_nt(q_ref[:, sl], k_ref[rows, sl])
            s_ref[h] = jnp.where(diag_ok, s, NEG_INF) if masked else s
        for h in range(C_HEADS):
            s = s_ref[h]
            m_old = m_ref[h]
            m_new = jnp.maximum(m_old, jnp.max(s, -1, keepdims=True))
            alpha_ref[h] = jnp.exp2(m_old - m_new)
            p_ref[h] = jnp.exp2(s - jnp.tile(m_new, (1, tq // LANES))).astype(BF16)
            m_ref[h] = m_new
        for h in range(C_HEADS):
            sl = slice(h * LANES, (h + 1) * LANES)
            acc_ref[h] = alpha_ref[h] * acc_ref[h] + _dot(p_ref[h], v_ref[rows, sl])

    def body(j, carry):
        block(j, False)
        return carry

    lax.fori_loop(0, i, body, 0)
    block(i, True)
    for h in range(C_HEADS):
        acc = acc_ref[h]
        o_ref[:, h * C_V:(h + 1) * C_V] = (acc[:, :C_V] / acc[:, C_V:]).astype(o_ref.dtype)


def _mla_attn(q, k, v, b, s, tq=512):
    qw, vw = C_HEADS * LANES, C_HEADS * C_V
    o = pl.pallas_call(
        functools.partial(_mla_attn_kernel, tq=tq), grid=(b, s // tq),
        in_specs=[pl.BlockSpec((None, tq, qw), lambda bi, i: (bi, i, 0)),
                  pl.BlockSpec((None, s, qw), lambda bi, i: (bi, 0, 0)),
                  pl.BlockSpec((None, s, qw), lambda bi, i: (bi, 0, 0))],
        out_specs=pl.BlockSpec((None, tq, vw), lambda bi, i: (bi, i, 0)),
        out_shape=jax.ShapeDtypeStruct((b, s, vw), BF16),
        scratch_shapes=[pltpu.VMEM((C_HEADS, tq, LANES), F32),
                        pltpu.VMEM((C_HEADS, tq, LANES), F32),
                        pltpu.VMEM((C_HEADS, tq, tq), F32),
                        pltpu.VMEM((C_HEADS, tq, tq), BF16),
                        pltpu.VMEM((C_HEADS, tq, LANES), F32)],
        compiler_params=_cparams("parallel", "parallel"), name="mla_attn",
    )(q.reshape(b, s, qw), k.reshape(b, s, qw), v.reshape(b, s, qw))
    return o.reshape(b * s, vw)


def _split3(x):
    hi = x.astype(BF16)
    r1 = x - hi.astype(F32)
    mid = r1.astype(BF16)
    lo = (r1 - mid.astype(F32)).astype(BF16)
    return hi, mid, lo


def _rwkv_kernel(*refs, vres, nchunk):
    c = D_CHUNK
    tr = nchunk * c
    if vres:
        (dp_ref, vf_ref, mu_ref, w0_ref, w2_ref, a0_ref, a2_ref, v0_ref, v2_ref,
         kk_ref, ka_ref, rk_ref, lw_ref, lb_ref, ones_ref, out_ref,
         state_ref, carry_ref, y_ref) = refs
    else:
        (dp_ref, mu_ref, w0_ref, w2_ref, a0_ref, a2_ref,
         kk_ref, ka_ref, rk_ref, lw_ref, lb_ref, ones_ref, out_ref, vf_out,
         state_ref, carry_ref, y_ref) = refs

    @pl.when(pl.program_id(1) == 0)
    def _():
        state_ref[...] = jnp.zeros_like(state_ref)
        carry_ref[...] = jnp.zeros_like(carry_ref)

    hcur = dp_ref[...].astype(F32)
    row = lax.broadcasted_iota(jnp.int32, (tr, 1), 0)
    prev = jnp.where(row == 0, carry_ref[...], pltpu.roll(hcur, 1, 0))
    carry_ref[...] = hcur[tr - 1:tr, :]
    sh = hcur + (prev - hcur) * mu_ref[...]

    w = D_WIDTH
    r = sh[:, 0:w]
    k = sh[:, w:2 * w]
    v = sh[:, 2 * w:3 * w]
    wa = sh[:, 3 * w:3 * w + LANES]
    pre_w = w0_ref[...] + _dot(jnp.tanh(wa).astype(BF16), w2_ref[...])
    logd = -math.exp(-0.5) / (1.0 + jnp.exp(-pre_w))
    pre_a = a0_ref[...] + _dot(wa.astype(BF16), a2_ref[...])
    a = 1.0 / (1.0 + jnp.exp(-pre_a))
    if vres:
        vd = sh[:, 3 * w + LANES:3 * w + 2 * LANES]
        pre_v = v0_ref[...] + _dot(vd.astype(BF16), v2_ref[...])
        v = v + (vf_ref[...].astype(F32) - v) * (1.0 / (1.0 + jnp.exp(-pre_v)))
    else:
        vf_out[...] = v.astype(vf_out.dtype)
    def head_sum(t):
        t_hi = t.astype(BF16)
        t_lo = (t - t_hi.astype(F32)).astype(BF16)
        n = t.shape[1] // LANES
        stack = jnp.concatenate([piece[:, i * LANES:(i + 1) * LANES] for i in range(n) for piece in (t_hi, t_lo)], axis=0)
        res = _dot(stack, ones_ref[...])
        rows = t.shape[0]
        return jnp.concatenate([res[2 * i * rows:(2 * i + 1) * rows] + res[(2 * i + 1) * rows:(2 * i + 2) * rows]
                                for i in range(n)], axis=1)

    kk = k * kk_ref[...]
    kk = kk / jnp.maximum(jnp.sqrt(head_sum(kk * kk)), 1e-12)
    k2 = k * (1.0 + (a - 1.0) * ka_ref[...])

    ti = lax.broadcasted_iota(jnp.int32, (tr, tr), 0)
    tj = lax.broadcasted_iota(jnp.int32, (tr, tr), 1)
    tri = jnp.where((tj <= ti) & (tj >= ti // c * c), 1.0, 0.0).astype(BF16)
    hi, mid, lo = _split3(logd)
    cum = _dot(tri, hi) + _dot(tri, mid) + _dot(tri, lo)
    p_inc = jnp.exp(cum)
    p_exc = jnp.where(row % c == 0, 1.0, pltpu.roll(p_inc, 1, 0))
    p_inv = jnp.exp(-cum)
    p_all = [p_inc[(ci + 1) * c - 1:(ci + 1) * c, :] for ci in range(nchunk)]
    p_all_rows = jnp.concatenate([jnp.broadcast_to(pa, (c, D_WIDTH)) for pa in p_all], axis=0)
    kb = kk * a * p_inv
    kt = k2 * p_inv
    at_b = (-kk * p_exc).astype(BF16)
    rt_b = (r * p_inc).astype(BF16)
    bend_b = (kb * p_all_rows).astype(BF16)
    kend_b = (kt * p_all_rows).astype(BF16)
    vb = v.astype(BF16)

    pc = lax.broadcasted_iota(jnp.int32, (c, LANES), 1) % D_HEAD
    pr = lax.broadcasted_iota(jnp.int32, (c, LANES), 0)
    strict = pc < pr
    incl = pc <= pr
    eye = jnp.where(pc == pr, 1.0, 0.0)
    lo_lane = lax.broadcasted_iota(jnp.int32, (c, LANES), 1) < D_HEAD
    sr = lax.broadcasted_iota(jnp.int32, (LANES, LANES), 0) < D_HEAD
    sc = lax.broadcasted_iota(jnp.int32, (LANES, LANES), 1) < D_HEAD
    same_head = sr == sc

    def pair_diag(t):
        return jnp.concatenate([jnp.where(lo_lane, t, 0.0), jnp.where(lo_lane, 0.0, t)], axis=0).astype(BF16)

    items = [(ci, p) for ci in range(nchunk) for p in range(D_HEADS // 2)]
    rsl = lambda ci: slice(ci * c, (ci + 1) * c)
    psl = lambda p: slice(p * LANES, (p + 1) * LANES)
    sub = lambda arr, it: arr[rsl(it[0]), psl(it[1])]
    lhs = {it: jnp.concatenate([sub(at_b, it), sub(rt_b, it)], axis=0) for it in items}
    mats = {it: _dot_nt(lhs[it], jnp.concatenate([pair_diag(sub(kb, it)), pair_diag(sub(kt, it))], axis=0))
            for it in items}
    r_b = {it: jnp.where(incl, mats[it][c:, :LANES], 0.0).astype(BF16) for it in items}
    x0 = {it: jnp.where(strict, mats[it][:c, :LANES], 0.0) for it in items}
    xp = {it: _dot(x0[it].astype(BF16), pair_diag(x0[it])) for it in items}
    tinv = {it: eye + x0[it] for it in items}
    n_levels = int(math.log2(c))
    for lvl in range(1, n_levels - 1):
        prod = {it: _dot(jnp.concatenate([xp[it], tinv[it]], axis=0).astype(BF16), pair_diag(xp[it])) for it in items}
        xp = {it: prod[it][:c] for it in items}
        tinv = {it: tinv[it] + prod[it][c:] for it in items}
    tinv = {it: (tinv[it] + _dot(tinv[it].astype(BF16), pair_diag(xp[it]))).astype(BF16) for it in items}
    m_k = {it: jnp.concatenate([jnp.where(strict, mats[it][:c, LANES:], 0.0),
                                jnp.where(incl, mats[it][c:, LANES:], 0.0)], axis=0).astype(BF16) for it in items}
    from_v = {it: _dot(m_k[it], pair_diag(sub(v, it))) for it in items}
    upd_v = {it: jnp.where(same_head, _dot_tn(sub(vb, it), sub(kend_b, it)), 0.0) for it in items}

    state = [state_ref[p] for p in range(D_HEADS // 2)]
    pairs = range(D_HEADS // 2)
    for ci in range(nchunk):
        from_state = [_dot_nt(lhs[ci, p], state[p].astype(BF16)) for p in pairs]
        u = [_dot(tinv[ci, p], pair_diag(from_state[p][:c] + from_v[ci, p][:c])) for p in pairs]
        upd_u = [_dot_tn(u[p].astype(BF16), sub(bend_b, (ci, p))) for p in pairs]
        state = [state[p] * p_all[ci][:, psl(p)] + jnp.where(same_head, upd_u[p], 0.0) + upd_v[ci, p] for p in pairs]
        for p in pairs:
            y_ref[rsl(ci), psl(p)] = from_state[p][c:] + from_v[ci, p][c:] + _dot(r_b[ci, p], pair_diag(u[p]))
    for p in pairs:
        state_ref[p] = state[p]

    y = y_ref[...]
    yc = y - head_sum(y) * (1.0 / D_HEAD)
    var = head_sum(yc * yc) * (1.0 / D_HEAD)
    yn = yc * lax.rsqrt(var + RWKV_LN_EPS) * lw_ref[...] + lb_ref[...]
    bonus = head_sum(r * k2 * rk_ref[...]) * v
    out_ref[...] = (yn + bonus).astype(out_ref.dtype)


def _rwkv(dproj, v_first, p, b, s):
    tr = D_STEP_CHUNKS * D_CHUNK
    vres = v_first is not None
    row = lambda wd: pl.BlockSpec((None, tr, wd), lambda bi, i: (bi, i, 0))
    full = lambda a: pl.BlockSpec(a.shape, lambda bi, i: (0, 0))
    ins = [dproj.reshape(b, s, D_PROJ_PAD)]
    in_specs = [row(D_PROJ_PAD)]
    if vres:
        ins.append(v_first.reshape(b, s, D_WIDTH))
        in_specs.append(row(D_WIDTH))
    names = ['mu', 'w0', 'w2', 'a0', 'a2'] + (['v0', 'v2'] if vres else []) + ['k_k', 'k_a', 'r_k', 'lnx_w', 'lnx_b']
    for nm in names:
        ins.append(p[nm])
        in_specs.append(full(p[nm]))
    head_id = np.arange(LANES) // D_HEAD
    ones_bd = jnp.asarray(head_id[:, None] == head_id[None, :], BF16)
    ins.append(ones_bd)
    in_specs.append(full(ones_bd))
    out_shape = [jax.ShapeDtypeStruct((b, s, D_WIDTH), BF16)]
    out_specs = [row(D_WIDTH)]
    if not vres:
        out_shape.append(jax.ShapeDtypeStruct((b, s, D_WIDTH), BF16))
        out_specs.append(row(D_WIDTH))
    res = pl.pallas_call(
        functools.partial(_rwkv_kernel, vres=vres, nchunk=D_STEP_CHUNKS), grid=(b, s // tr),
        in_specs=in_specs, out_specs=out_specs, out_shape=out_shape,
        scratch_shapes=[pltpu.VMEM((D_HEADS // 2, LANES, LANES), F32),
                        pltpu.VMEM((1, D_PROJ_PAD), F32),
                        pltpu.VMEM((tr, D_WIDTH), F32)],
        compiler_params=_cparams("parallel", "arbitrary"), name="rwkv7",
    )(*ins)
    d_out = res[0].reshape(b * s, D_WIDTH)
    vf = v_first if vres else res[1].reshape(b * s, D_WIDTH)
    return d_out, vf


def _rope_tables(positions, dim, theta, period, base, half):
    inv = jnp.exp(-math.log(theta) * jnp.arange(0, dim, 2, dtype=F32) / dim)
    ang = positions.astype(F32)[..., None] * inv
    cos, sin = jnp.cos(ang), jnp.sin(ang)
    lead = ang.shape[:-1]
    fill = lambda width, val: jnp.full(lead + (width,), val, F32)
    rest = period - base - 2 * half

    def lanes(before, lo, hi, after):
        one_period = jnp.concatenate([before, lo, hi, after], axis=-1)
        return jnp.tile(one_period, (1, 1, LANES // period))

    c = lanes(fill(base, 1.0), cos, cos, fill(rest, 1.0))
    s1 = lanes(fill(base, 0.0), fill(half, 0.0), sin, fill(rest, 0.0))
    s2 = lanes(fill(base, 0.0), -sin, fill(half, 0.0), fill(rest, 0.0))
    n = positions.shape[0] * positions.shape[1]
    return tuple(t.reshape(n, LANES) for t in (c, s1, s2))


A_SUB_DILATIONS = tuple(d for d in A_DILATIONS if d > 1)


def _even_proj(norm, w_in, tabs_a, tabs_r):
    q_scale = HEAD_DIM ** -0.5 * LOG2E
    aw = A_WIDTH
    plan = [(0, aw, 0, 0, (0, ROT_DIM // 2, q_scale), 0), (aw, aw, 0, aw, (0, ROT_DIM // 2, 1.0), aw),
            (2 * aw, aw, 0, 2 * aw, None, 2 * aw),
            (1536, 512, 0, 1536, (1, B_QK_DIM // 2, 1.0)), (2048, 512, 0, 2048, None),
            (2560, 256, 1, 0, (None, 0, MEM_Q_SCALE)), (2816, 640, 2, 0, None), (3456, 640, 2, 640, None)]
    return _proj_spec(norm, w_in.astype(BF16), plan, (2560, 256, MIX_WIDTH), (tabs_a, tabs_r),
                      dils=A_SUB_DILATIONS, sub_width=3 * aw)


def _even_mixers(res, b, s, mem_kv):
    main, qm, gate = res[:3]
    main3 = main.reshape(b, s, 2560)
    qkv = {1: main3.reshape(b, 1, s, 2560)}
    qkv.update(zip(A_SUB_DILATIONS, res[3:]))
    outs, lses = [], []
    for dil in A_DILATIONS:
        o, lse = _band_attn(qkv[dil], 0, dil)
        outs.append(o)
        lses.append(lse)
    r_out = _retention(main3)
    m_out = _mem_attn(qm, mem_kv, b, s)
    return (outs, lses), r_out, m_out, gate


def _odd_proj(norm, w_in, d_cols):
    k_dim = w_in.shape[0]
    z = lambda n: jnp.zeros((k_dim, n), F32)
    o = 0
    cq = w_in[:, o:o + C_Q_RANK]; o += C_Q_RANK
    ckv = w_in[:, o:o + C_KV_RANK]; o += C_KV_RANK
    kro = w_in[:, o:o + C_ROPE]; o += C_ROPE
    dp = w_in[:, o:o + d_cols]; o += d_cols
    qm_w = w_in[:, o:o + MEM_WIDTH]; o += MEM_WIDTH
    gate_w = w_in[:, o:]
    dp_pad = jnp.concatenate([dp, z(D_PROJ_PAD - d_cols)], axis=1)
    swap = lambda t: jnp.concatenate([t[..., C_ROPE // 2:], t[..., :C_ROPE // 2]], axis=-1)
    rope_slab = lambda t: jnp.concatenate([z(C_NOPE), t, z(LANES - C_NOPE - C_ROPE)], axis=1)
    w_pad = jnp.concatenate([cq, ckv, rope_slab(kro), rope_slab(swap(kro)), dp_pad, qm_w, gate_w], axis=1)
    m0 = C_Q_RANK + C_KV_RANK + 2 * LANES
    plan = [(0, m0, 0, 0, None), (m0, 896, 1, 0, None), (m0 + 896, 896, 1, 896, None),
            (m0 + 1792, 256, 2, 0, (None, 0, MEM_Q_SCALE)),
            (m0 + 2048, 640, 3, 0, None), (m0 + 2688, 640, 3, 640, None)]
    return _proj_spec(norm, w_pad.astype(BF16), plan, (m0, D_PROJ_PAD, 256, MIX_WIDTH))


def _odd_mixers(res, b, s, mem_kv, tabs_m, v_first, q_norm, w_qb, kv_norm, w_kvb,
                mu_shift, w0, w2, a0, a2, v0, v2, k_k, k_a, r_k, lnx_w, lnx_b):
    mla_in, dproj, qm, gate = res
    vres = v0 is not None
    d_cols = mu_shift.shape[0]
    swap = lambda t: jnp.concatenate([t[..., C_ROPE // 2:], t[..., :C_ROPE // 2]], axis=-1)
    wq = w_qb.reshape(C_Q_RANK, C_HEADS, C_NOPE + C_ROPE)
    slab_pad = ((0, 0), (0, 0), (0, LANES - C_NOPE - C_ROPE))
    wq_sw = jnp.concatenate([wq[..., :C_NOPE], swap(wq[..., C_NOPE:])], axis=-1)
    wq_sw = jnp.pad(wq_sw, slab_pad).reshape(C_Q_RANK, C_HEADS * LANES)
    wq = jnp.pad(wq, slab_pad).reshape(C_Q_RANK, C_HEADS * LANES)
    wkv = w_kvb.reshape(C_KV_RANK, C_HEADS, C_NOPE + C_V)
    wk = jnp.pad(wkv[:, :, :C_NOPE], ((0, 0), (0, 0), (0, LANES - C_NOPE))).reshape(C_KV_RANK, C_HEADS * LANES)
    wv = jnp.pad(wkv[:, :, C_NOPE:], ((0, 0), (0, 0), (0, LANES - C_V))).reshape(C_KV_RANK, C_HEADS * LANES)
    q, k, v = _mla_prep(mla_in, q_norm, kv_norm, wq.astype(BF16), wq_sw.astype(BF16), wk.astype(BF16),
                        wv.astype(BF16), tabs_m)
    c_out = _mla_attn(q, k, v, b, s)

    row = lambda t: t.reshape(1, -1).astype(F32)
    pad_rows = lambda t, top, n: jnp.concatenate(
        [jnp.zeros((top, t.shape[1]), F32), t, jnp.zeros((n - top - t.shape[0], t.shape[1]), F32)], axis=0)
    p = dict(mu=row(jnp.concatenate([mu_shift, jnp.zeros((D_PROJ_PAD - d_cols,), F32)])),
             w0=row(w0), w2=pad_rows(w2, 0, LANES).astype(BF16),
             a0=row(a0), a2=pad_rows(a2, D_DECAY_LORA, LANES).astype(BF16),
             k_k=row(k_k), k_a=row(k_a), r_k=row(r_k), lnx_w=row(lnx_w), lnx_b=row(lnx_b))
    if vres:
        p['v0'] = row(v0)
        p['v2'] = pad_rows(v2, 0, LANES).astype(BF16)
    d_out, v_first = _rwkv(dproj, v_first, p, b, s)
    m_out = _mem_attn(qm, mem_kv, b, s)
    return (c_out, d_out, m_out, gate), v_first


def kernel(x, mem, positions, mem_norm, final_norm, l0_norm, l0_w_in, l0_w_mem_kv, l0_w_out, l1_norm, l1_w_in, l1_q_norm, l1_w_qb, l1_kv_norm, l1_w_kvb, l1_mu_shift, l1_w0, l1_w2, l1_a0, l1_a2, l1_k_k, l1_k_a, l1_r_k, l1_lnx_w, l1_lnx_b, l1_w_mem_kv, l1_w_out, l2_norm, l2_w_in, l2_w_mem_kv, l2_w_out, l3_norm, l3_w_in, l3_q_norm, l3_w_qb, l3_kv_norm, l3_w_kvb, l3_mu_shift, l3_w0, l3_w2, l3_a0, l3_a2, l3_v0, l3_v2, l3_k_k, l3_k_a, l3_r_k, l3_lnx_w, l3_lnx_b, l3_w_mem_kv, l3_w_out):
    b, s, d = x.shape
    x2d = x.reshape(b * s, d)
    mem2d = mem.reshape(b * MEM_TOKENS, d)
    tabs_a = _rope_tables(positions, ROT_DIM, ROPE_THETA, HEAD_DIM, 0, ROT_DIM // 2)
    tabs_r = _rope_tables(positions, B_QK_DIM, RET_THETA, B_QK_DIM, 0, B_QK_DIM // 2)
    tabs_m = _rope_tables(positions, C_ROPE, ROPE_THETA, LANES, C_NOPE, C_ROPE // 2)

    def mem_kv(w):
        plan = [(0, 2 * MEM_WIDTH, 0, 0, None)]
        return _norm_proj(mem2d, mem_norm, w.astype(BF16), plan, (2 * MEM_WIDTH,))[0]

    proj = [_even_proj(l0_norm, l0_w_in, tabs_a, tabs_r), _odd_proj(l1_norm, l1_w_in, l1_mu_shift.shape[0]),
            _even_proj(l2_norm, l2_w_in, tabs_a, tabs_r), _odd_proj(l3_norm, l3_w_in, l3_mu_shift.shape[0])]
    w_outs = [w.astype(BF16) for w in (l0_w_out, l1_w_out, l2_w_out, l3_w_out)]
    p0 = proj[0]
    res = _norm_proj(x2d, p0['g'], p0['w'], p0['plan'], p0['out_widths'], p0['tables'], dils=p0['dils'],
                     sub_width=p0['sub_width'], seq=s)
    ops = _even_mixers(res, b, s, mem_kv(l0_w_mem_kv))
    x2d, *res = _out_proj(*ops, x2d, w_outs[0], seq=s, next_proj=proj[1])
    ops, v_first = _odd_mixers(res, b, s, mem_kv(l1_w_mem_kv), tabs_m, None, l1_q_norm, l1_w_qb, l1_kv_norm, l1_w_kvb,
                               l1_mu_shift, l1_w0, l1_w2, l1_a0, l1_a2, None, None,
                               l1_k_k, l1_k_a, l1_r_k, l1_lnx_w, l1_lnx_b)
    x2d, *res = _out_proj(*ops, x2d, w_outs[1], seq=s, next_proj=proj[2])
    ops = _even_mixers(res, b, s, mem_kv(l2_w_mem_kv))
    x2d, *res = _out_proj(*ops, x2d, w_outs[2], seq=s, next_proj=proj[3])
    ops, _ = _odd_mixers(res, b, s, mem_kv(l3_w_mem_kv), tabs_m, v_first, l3_q_norm, l3_w_qb, l3_kv_norm, l3_w_kvb,
                         l3_mu_shift, l3_w0, l3_w2, l3_a0, l3_a2, l3_v0, l3_v2,
                         l3_k_k, l3_k_a, l3_r_k, l3_lnx_w, l3_lnx_b)
    out = _out_proj(*ops, x2d, w_outs[3], final_gain=final_norm)
    return out.reshape(b, s, d)
```

```python
import functools
import math

import numpy as np
import jax
import jax.numpy as jnp
from jax import lax
from jax.experimental import pallas as pl
from jax.experimental.pallas import tpu as pltpu

F32 = jnp.float32
BF16 = jnp.bfloat16

LANES = 128
VMEM_LIMIT = 56 * 1024 * 1024
ROW_TILE = 512

D_MODEL = 1024
HEAD_DIM = 64
ROPE_THETA = 500000.0
ROT_DIM = HEAD_DIM // 4
NORM_EPS = 1e-6
NEG_INF = -1e30
LN2 = math.log(2.0)
LOG2E = 1.0 / LN2

A_HEADS = 8
A_WIDTH = A_HEADS * HEAD_DIM
A_PATTERNS = ((128, 1), (512, 4), (2048, 16))
A_BLOCK = 128
A_DILATIONS = tuple(d for (_, d) in A_PATTERNS)
A_LSE_GROUP = HEAD_DIM // (A_HEADS // 2)

B_HEADS = 4
B_QK_DIM = 64
B_V_DIM = 128
B_CHUNK = 128
RET_THETA = 10000.0

MEM_TOKENS = 256
MEM_HEADS = 4
MEM_WIDTH = MEM_HEADS * HEAD_DIM
MEM_Q_SCALE = HEAD_DIM ** -0.5 / math.log(2.0)

C_HEADS = 8
C_Q_RANK = 256
C_KV_RANK = 128
C_NOPE = 64
C_ROPE = 32
C_V = 64

D_HEADS = 8
D_HEAD = 64
D_WIDTH = D_HEADS * D_HEAD
D_DECAY_LORA = 64
D_AAA_LORA = 64
D_MV_LORA = 32
RWKV_LN_EPS = 64e-5
D_CHUNK = 64
D_STEP_CHUNKS = 4
D_PROJ_PAD = 3 * D_WIDTH + 2 * LANES

MIX_WIDTH = 1280


def _cparams(*sem):
    return pltpu.CompilerParams(dimension_semantics=sem, vmem_limit_bytes=VMEM_LIMIT)


def _dot(a, b):
    return jnp.dot(a, b, preferred_element_type=F32)


def _dot_nt(a, b):
    return lax.dot_general(a, b, (((1,), (1,)), ((), ())), preferred_element_type=F32)


def _dot_tn(a, b):
    return lax.dot_general(a, b, (((0,), (0,)), ((), ())), preferred_element_type=F32)


def _rot_slab(x, c, s1, s2, half):
    return x * c + pltpu.roll(x, half, 1) * s1 + pltpu.roll(x, LANES - half, 1) * s2


def _rotary(x, c, s1, s2, half):
    slabs = [_rot_slab(x[:, i:i + LANES], c, s1, s2, half) for i in range(0, x.shape[1], LANES)]
    return slabs[0] if len(slabs) == 1 else jnp.concatenate(slabs, axis=1)


def _norm_proj_kernel(*refs, plan, n_tab, n_out, dils):
    x_ref, g_ref, w_ref = refs[:3]
    tabs = refs[3:3 + 3 * n_tab]
    outs = refs[3 + 3 * n_tab:3 + 3 * n_tab + n_out]
    sub_outs = refs[3 + 3 * n_tab + n_out:3 + 3 * n_tab + n_out + len(dils)]
    stage_ref = refs[-1] if dils else None
    _norm_proj_body(x_ref[...].astype(F32), g_ref, w_ref, tabs, outs, sub_outs, stage_ref, plan, dils)


def _norm_proj_body(x, g_ref, w_ref, tabs, outs, sub_outs, stage_ref, plan, dils):
    tm = x.shape[0]
    ms = jnp.mean(x * x, axis=-1, keepdims=True)
    xn = (x * lax.rsqrt(ms + NORM_EPS) * g_ref[...]).astype(BF16)
    n_staged = [0]

    def epilogue(entry, acc):
        (_, width, oi, oc, rot, sub_col) = entry
        if rot is not None:
            t, half, post_scale = rot
            if t is not None:
                acc = _rotary(acc, tabs[3 * t][...], tabs[3 * t + 1][...], tabs[3 * t + 2][...], half)
            if post_scale != 1.0:
                acc = acc * post_scale
        outs[oi][:, oc:oc + width] = acc.astype(outs[oi].dtype)
        if sub_col is not None:
            slot = n_staged[0]
            n_staged[0] += 1
            for c in range(width // LANES):
                stage_ref[slot, c] = acc[:, c * LANES:(c + 1) * LANES]
            for d, sub in zip(dils, sub_outs):
                for r in range(d):
                    for c in range(width // LANES):
                        col = sub_col + c * LANES
                        sub[r, :, col:col + LANES] = stage_ref[slot, c, pl.ds(r, tm // d, stride=d), :].astype(sub.dtype)

    pending = None
    for entry in plan:
        acc = _dot(xn, w_ref[:, entry[0]:entry[0] + entry[1]])
        if pending is not None:
            epilogue(*pending)
        pending = (entry, acc)
    epilogue(*pending)


def _norm_proj(x2d, g, w, plan, out_widths, tables=(), tm=ROW_TILE, dils=(), sub_width=0, seq=None):
    t_rows, k = x2d.shape
    assert t_rows % tm == 0
    proj = _proj_spec(g, w, plan, out_widths, tables, dils, sub_width)
    ins, in_specs, out_specs, out_shape, scratch = _proj_operands(proj, t_rows, tm, seq)
    kern = functools.partial(_norm_proj_kernel, plan=proj['plan'], n_tab=len(tables), n_out=len(out_widths),
                             dils=tuple(dils))
    return pl.pallas_call(
        kern, grid=(t_rows // tm,), in_specs=[pl.BlockSpec((tm, k), lambda i: (i, 0))] + in_specs,
        out_specs=out_specs, out_shape=out_shape,
        scratch_shapes=scratch, compiler_params=_cparams("parallel"), name="norm_proj",
    )(x2d, *ins)


def _proj_spec(g, w, plan, out_widths, tables=(), dils=(), sub_width=0):
    plan = tuple(tuple(p) + (None,) * (6 - len(p)) for p in plan)
    return dict(g=g, w=w, plan=plan, out_widths=tuple(out_widths), tables=tuple(tables), dils=tuple(dils),
                sub_width=sub_width)


def _proj_operands(proj, t_rows, tm, seq):
    k, n = proj['w'].shape
    flat_tabs = [t for tab in proj['tables'] for t in tab]
    ins = [proj['g'].reshape(1, k).astype(F32), proj['w']] + flat_tabs
    in_specs = [pl.BlockSpec((1, k), lambda i: (0, 0)),
                pl.BlockSpec((k, n), lambda i: (0, 0), pipeline_mode=pl.Buffered(1))]
    in_specs += [pl.BlockSpec((tm, LANES), lambda i: (i, 0)) for _ in flat_tabs]
    out_specs = [pl.BlockSpec((tm, ow), lambda i: (i, 0)) for ow in proj['out_widths']]
    out_shape = [jax.ShapeDtypeStruct((t_rows, ow), BF16) for ow in proj['out_widths']]
    scratch = []
    if proj['dils']:
        nt = seq // tm
        sub_width = proj['sub_width']
        for d in proj['dils']:
            out_specs.append(pl.BlockSpec((None, d, tm // d, sub_width), lambda i: (i // nt, 0, i % nt, 0)))
            out_shape.append(jax.ShapeDtypeStruct((t_rows // seq, d, seq // d, sub_width), BF16))
        staged = [p[1] for p in proj['plan'] if p[5] is not None]
        scratch = [pltpu.VMEM((len(staged), max(staged) // LANES, tm, LANES), F32)]
    return ins, in_specs, out_specs, out_shape, scratch


def _band_attn_kernel(q_ref, kp_ref, kc_ref, vp_ref, vc_ref, o_ref, lse_ref, s_ref, p_ref, m_ref, *, nq):
    i = pl.program_id(2)
    qi = lax.broadcasted_iota(jnp.int32, (A_BLOCK, A_BLOCK), 0)
    kj = lax.broadcasted_iota(jnp.int32, (A_BLOCK, A_BLOCK), 1)
    bias_prev = jnp.where(kj >= qi, 0.0, NEG_INF)
    bias_cur = jnp.where(kj <= qi, 0.0, NEG_INF)
    bias_first = bias_prev + jnp.where(i > 0, 0.0, NEG_INF)
    lane = lax.broadcasted_iota(jnp.int32, (A_BLOCK, LANES), 1)
    lo = lane < HEAD_DIM
    keep = (jnp.where(lo, 1.0, 0.0).astype(BF16), jnp.where(lo, 0.0, 1.0).astype(BF16))
    npair = A_HEADS // 2

    def operands(u, pair):
        sl = slice(pair * LANES, (pair + 1) * LANES)
        rows = slice(u * A_BLOCK, (u + 1) * A_BLOCK)
        if u == 0:
            return rows, sl, kp_ref[:, sl], vp_ref[:, sl], bias_first
        prow = slice((u - 1) * A_BLOCK, u * A_BLOCK)
        return rows, sl, kc_ref[prow, sl], vc_ref[prow, sl], bias_prev

    for u in range(nq):
        for pair in range(npair):
            rows, sl, kp, _, bp = operands(u, pair)
            q = q_ref[rows, sl]
            kc = kc_ref[rows, sl]
            for half in range(2):
                idx = (u * npair + pair) * 2 + half
                qh = q * keep[half]
                s_ref[idx, :, :A_BLOCK] = _dot_nt(qh, kp) + bp
                s_ref[idx, :, A_BLOCK:] = _dot_nt(qh, kc) + bias_cur
    for idx in range(nq * A_HEADS):
        s = s_ref[idx]
        m = jnp.max(s, -1, keepdims=True)
        p_ref[idx] = jnp.exp2(s - m).astype(BF16)
        m_ref[idx] = jnp.broadcast_to(m, (A_BLOCK, LANES))
    for u in range(nq):
        lse = None
        for pair in range(npair):
            rows, sl, _, vp, _ = operands(u, pair)
            vc = vc_ref[rows, sl]
            idx0 = (u * npair + pair) * 2
            o = [_dot(p_ref[idx0 + half, :, :A_BLOCK], vp * keep[half] + keep[1 - half])
                 + _dot(p_ref[idx0 + half, :, A_BLOCK:], vc * keep[half] + keep[1 - half]) for half in range(2)]
            den = pltpu.roll(jnp.where(lo, o[1], o[0]), HEAD_DIM, 1)
            o_ref[rows, sl] = (jnp.where(lo, o[0], o[1]) / den).astype(o_ref.dtype)
            lse_pair = jnp.where(lo, m_ref[idx0], m_ref[idx0 + 1]) * LN2 + jnp.log(den)
            lse = lse_pair if lse is None else jnp.where(lane % HEAD_DIM // A_LSE_GROUP == pair, lse_pair, lse)
        lse_ref[rows, :] = lse


def _single_block_attn_kernel(q_ref, k_ref, v_ref, o_ref, lse_ref, s_ref, p_ref, m_ref, *, nres):
    qi = lax.broadcasted_iota(jnp.int32, (A_BLOCK, A_BLOCK), 0)
    kj = lax.broadcasted_iota(jnp.int32, (A_BLOCK, A_BLOCK), 1)
    bias = jnp.where(kj <= qi, 0.0, NEG_INF)
    lane = lax.broadcasted_iota(jnp.int32, (A_BLOCK, LANES), 1)
    lo = lane < HEAD_DIM
    keep = (jnp.where(lo, 1.0, 0.0).astype(BF16), jnp.where(lo, 0.0, 1.0).astype(BF16))
    npair = A_HEADS // 2
    for r in range(nres):
        for pair in range(npair):
            sl = slice(pair * LANES, (pair + 1) * LANES)
            for half in range(2):
                s_ref[(r * npair + pair) * 2 + half] = _dot_nt(q_ref[r, :, sl] * keep[half], k_ref[r, :, sl]) + bias
    for idx in range(nres * A_HEADS):
        s = s_ref[idx]
        m = jnp.max(s, -1, keepdims=True)
        p_ref[idx] = jnp.exp2(s - m).astype(BF16)
        m_ref[idx] = jnp.broadcast_to(m, (A_BLOCK, LANES))
    for r in range(nres):
        lse = None
        for pair in range(npair):
            sl = slice(pair * LANES, (pair + 1) * LANES)
            idx0 = (r * npair + pair) * 2
            v = v_ref[r, :, sl]
            o = [_dot(p_ref[idx0 + half], v * keep[half] + keep[1 - half]) for half in range(2)]
            den = pltpu.roll(jnp.where(lo, o[1], o[0]), HEAD_DIM, 1)
            o_ref[r, :, sl] = (jnp.where(lo, o[0], o[1]) / den).astype(o_ref.dtype)
            lse_pair = jnp.where(lo, m_ref[idx0], m_ref[idx0 + 1]) * LN2 + jnp.log(den)
            lse = lse_pair if lse is None else jnp.where(lane % HEAD_DIM // A_LSE_GROUP == pair, lse_pair, lse)
        lse_ref[r] = lse


def _single_block_attn(qkv, col0, dilation, nres=4):
    b, d, sub_len, _ = qkv.shape
    assert sub_len == A_BLOCK and d % nres == 0
    spec = lambda col, w: pl.BlockSpec((None, nres, A_BLOCK, w), lambda bi, r: (bi, r, 0, col))
    return pl.pallas_call(
        functools.partial(_single_block_attn_kernel, nres=nres), grid=(b, d // nres),
        in_specs=[spec(col0, A_WIDTH), spec(col0 + 1, A_WIDTH), spec(col0 + 2, A_WIDTH)],
        out_specs=[spec(0, A_WIDTH), spec(0, LANES)],
        out_shape=[jax.ShapeDtypeStruct((b, d, sub_len, A_WIDTH), BF16),
                   jax.ShapeDtypeStruct((b, d, sub_len, LANES), F32)],
        scratch_shapes=[pltpu.VMEM((nres * A_HEADS, A_BLOCK, A_BLOCK), F32),
                        pltpu.VMEM((nres * A_HEADS, A_BLOCK, A_BLOCK), BF16),
                        pltpu.VMEM((nres * A_HEADS, A_BLOCK, LANES), F32)],
        compiler_params=_cparams("parallel", "parallel"), name="band_attn_d%d" % dilation,
    )(qkv, qkv, qkv)


def _band_attn(qkv, col0, dilation):
    b, d, sub_len, _ = qkv.shape
    assert d == dilation
    if sub_len == A_BLOCK:
        return _single_block_attn(qkv, col0, dilation)
    nq = 4 if sub_len % (4 * A_BLOCK) == 0 else 1
    tq = nq * A_BLOCK

    def cur(col):
        return pl.BlockSpec((None, None, tq, A_WIDTH), lambda bi, r, i: (bi, r, i, col0 + col))

    def prev(col):
        return pl.BlockSpec((None, None, A_BLOCK, A_WIDTH),
                            lambda bi, r, i: (bi, r, jnp.maximum(i * nq - 1, 0), col0 + col))

    return pl.pallas_call(
        functools.partial(_band_attn_kernel, nq=nq), grid=(b, d, sub_len // tq),
        in_specs=[cur(0), prev(1), cur(1), prev(2), cur(2)],
        out_specs=[pl.BlockSpec((None, None, tq, A_WIDTH), lambda bi, r, i: (bi, r, i, 0)),
                   pl.BlockSpec((None, None, tq, LANES), lambda bi, r, i: (bi, r, i, 0))],
        out_shape=[jax.ShapeDtypeStruct((b, d, sub_len, A_WIDTH), BF16),
                   jax.ShapeDtypeStruct((b, d, sub_len, LANES), F32)],
        scratch_shapes=[pltpu.VMEM((nq * A_HEADS, A_BLOCK, 2 * A_BLOCK), F32),
                        pltpu.VMEM((nq * A_HEADS, A_BLOCK, 2 * A_BLOCK), BF16),
                        pltpu.VMEM((nq * A_HEADS, A_BLOCK, LANES), F32)],
        compiler_params=_cparams("parallel", "parallel", "parallel"), name="band_attn_d%d" % dilation,
    )(qkv, qkv, qkv, qkv, qkv)


def _retention_kernel(qk_ref, v_ref, o_ref, state_ref, *, nchunk):
    c = B_CHUNK
    @pl.when(pl.program_id(1) == 0)
    def _():
        state_ref[...] = jnp.zeros_like(state_ref)

    scale = B_QK_DIM ** -0.5
    qi = lax.broadcasted_iota(jnp.int32, (c, c), 0)
    kj = lax.broadcasted_iota(jnp.int32, (c, c), 1)
    diff = (qi - kj).astype(F32)
    row = lax.broadcasted_iota(jnp.int32, (c, 1), 0).astype(F32)
    lo = lax.broadcasted_iota(jnp.int32, (c, LANES), 1) < B_QK_DIM
    keep = (jnp.where(lo, 1.0, 0.0).astype(BF16), jnp.where(lo, 0.0, 1.0).astype(BF16))
    log_g = [math.log1p(-2.0 ** (-5.0 - h)) for h in range(B_HEADS)]
    d_in = [jnp.where(diff >= 0, jnp.exp(jnp.maximum(diff, 0.0) * lg), 0.0) * scale for lg in log_g]
    from_start = [jnp.exp((row + 1.0) * lg) for lg in log_g]
    to_end = [jnp.exp((c - 1.0 - row) * lg) * scale for lg in log_g]
    items = [(ci, h) for ci in range(nchunk) for h in range(B_HEADS)]
    rows = lambda ci: slice(ci * c, (ci + 1) * c)
    qsl = lambda h: slice(h // 2 * LANES, (h // 2 + 1) * LANES)
    ksl = lambda h: slice(B_HEADS * B_QK_DIM + h // 2 * LANES, B_HEADS * B_QK_DIM + (h // 2 + 1) * LANES)
    vsl = lambda h: slice(h * B_V_DIM, (h + 1) * B_V_DIM)
    qm = {(ci, h): qk_ref[rows(ci), qsl(h)] * keep[h % 2] for ci, h in items}
    scores = {(ci, h): (_dot_nt(qm[ci, h], qk_ref[rows(ci), ksl(h)]) * d_in[h]).astype(BF16) for ci, h in items}
    o_in = {(ci, h): _dot(scores[ci, h], v_ref[rows(ci), vsl(h)]) for ci, h in items}
    kd = {(ci, h): ((qk_ref[rows(ci), ksl(h)] * keep[h % 2]).astype(F32) * to_end[h]).astype(BF16) for ci, h in items}
    kv = {(ci, h): _dot_tn(kd[ci, h], v_ref[rows(ci), vsl(h)]) for ci, h in items}
    state = [state_ref[h] for h in range(B_HEADS)]
    for ci in range(nchunk):
        o = [o_in[ci, h] + _dot(qm[ci, h], state[h].astype(BF16)) * from_start[h] for h in range(B_HEADS)]
        state = [state[h] * math.exp(c * log_g[h]) + kv[ci, h] for h in range(B_HEADS)]
        for h in range(B_HEADS):
            on = o[h] * lax.rsqrt(jnp.mean(o[h] * o[h], -1, keepdims=True) + NORM_EPS)
            o_ref[rows(ci), vsl(h)] = on.astype(o_ref.dtype)
    for h in range(B_HEADS):
        state_ref[h] = state[h]


def _retention(main, nchunk=4):
    b, s, _ = main.shape
    blk = (None, nchunk * B_CHUNK, 512)
    o = pl.pallas_call(
        functools.partial(_retention_kernel, nchunk=nchunk), grid=(b, s // (nchunk * B_CHUNK)),
        in_specs=[pl.BlockSpec(blk, lambda bi, i: (bi, i, 3)), pl.BlockSpec(blk, lambda bi, i: (bi, i, 4))],
        out_specs=pl.BlockSpec(blk, lambda bi, i: (bi, i, 0)),
        out_shape=jax.ShapeDtypeStruct((b, s, B_HEADS * B_V_DIM), BF16),
        scratch_shapes=[pltpu.VMEM((B_HEADS, LANES, B_V_DIM), F32)],
        compiler_params=_cparams("parallel", "arbitrary"), name="retention",
    )(main, main)
    return o.reshape(b * s, B_HEADS * B_V_DIM)


def _mem_attn_kernel(q_ref, kv_ref, o_ref, s_ref, p_ref):
    tq = q_ref.shape[0]
    lo_q = lax.broadcasted_iota(jnp.int32, (tq, LANES), 1) < HEAD_DIM
    lo_k = lax.broadcasted_iota(jnp.int32, (MEM_TOKENS, LANES), 1) < HEAD_DIM
    keep_q = (jnp.where(lo_q, 1.0, 0.0).astype(BF16), jnp.where(lo_q, 0.0, 1.0).astype(BF16))
    keep_k = (jnp.where(lo_k, 1.0, 0.0).astype(BF16), jnp.where(lo_k, 0.0, 1.0).astype(BF16))
    npair = MEM_HEADS // 2
    for pair in range(npair):
        sl = slice(pair * LANES, (pair + 1) * LANES)
        for half in range(2):
            s_ref[2 * pair + half] = _dot_nt(q_ref[:, sl] * keep_q[half], kv_ref[:, sl])
    for idx in range(MEM_HEADS):
        s = s_ref[idx]
        p_ref[idx] = jnp.exp2(s - jnp.max(s, -1, keepdims=True)).astype(BF16)
    for pair in range(npair):
        sl = slice(pair * LANES, (pair + 1) * LANES)
        v = kv_ref[:, MEM_WIDTH + pair * LANES:MEM_WIDTH + (pair + 1) * LANES]
        o = [_dot(p_ref[2 * pair + half], v * keep_k[half] + keep_k[1 - half]) for half in range(2)]
        den = pltpu.roll(jnp.where(lo_q, o[1], o[0]), HEAD_DIM, 1)
        o_ref[:, sl] = (jnp.where(lo_q, o[0], o[1]) / den).astype(o_ref.dtype)


def _mem_attn(qm, kv, b, s, tq=512):
    o = pl.pallas_call(
        _mem_attn_kernel, grid=(b, s // tq),
        in_specs=[pl.BlockSpec((None, tq, MEM_WIDTH), lambda bi, i: (bi, i, 0)),
                  pl.BlockSpec((None, MEM_TOKENS, 2 * MEM_WIDTH), lambda bi, i: (bi, 0, 0))],
        out_specs=pl.BlockSpec((None, tq, MEM_WIDTH), lambda bi, i: (bi, i, 0)),
        out_shape=jax.ShapeDtypeStruct((b, s, MEM_WIDTH), BF16),
        scratch_shapes=[pltpu.VMEM((MEM_HEADS, tq, MEM_TOKENS), F32),
                        pltpu.VMEM((MEM_HEADS, tq, MEM_TOKENS), BF16)],
        compiler_params=_cparams("parallel", "parallel"), name="mem_attn",
    )(qm.reshape(b, s, MEM_WIDTH), kv.reshape(b, MEM_TOKENS, 2 * MEM_WIDTH))
    return o.reshape(b * s, MEM_WIDTH)


def _mixture(o_refs, l_refs, e_ref, stage_o, stage_l, dils, tm):
    n_slab = A_WIDTH // LANES
    lses = []
    for pi, d in enumerate(dils):
        if d == 1:
            lses.append(l_refs[pi][0])
        else:
            for r in range(d):
                stage_l[pi, pl.ds(r, tm // d, stride=d), :] = l_refs[pi][r]
                for c in range(n_slab):
                    stage_o[pi, c, pl.ds(r, tm // d, stride=d), :] = (
                        o_refs[pi][r, :, c * LANES:(c + 1) * LANES].astype(F32))
            lses.append(stage_l[pi])
    m = functools.reduce(jnp.maximum, lses)
    es = [jnp.exp(l - m) for l in lses]
    inv = 1.0 / functools.reduce(lambda a, b: a + b, es)
    first = None
    for pi, d in enumerate(dils):
        wgt = es[pi] * inv
        w_hi = wgt.astype(BF16)
        w_lo = (wgt - w_hi.astype(F32)).astype(BF16)
        w_full = _dot(w_hi, e_ref[...]) + _dot(w_lo, e_ref[...])
        if d == 1:
            o = o_refs[pi][0].astype(F32)
        else:
            o = jnp.concatenate([stage_o[pi, c] for c in range(n_slab)], axis=1)
        first = w_full * o if first is None else first + w_full * o
    return first


def _out_proj_kernel(*refs, dils, final, nxt):
    n_mix = len(dils)
    if n_mix:
        o_refs, l_refs, e_ref = refs[:n_mix], refs[n_mix:2 * n_mix], refs[2 * n_mix]
        refs = refs[2 * n_mix + 1:]
    else:
        first = refs[0][...].astype(F32)
        refs = refs[1:]
    second, third, gate_ref, x_ref, w_ref = refs[:5]
    refs = refs[5:]
    if final:
        g_ref = refs[0]
        refs = refs[1:]
    if nxt is not None:
        n_plan, n_tab, n_out, n_dils = nxt
        ng_ref, nw_ref = refs[:2]
        n_tabs = refs[2:2 + 3 * n_tab]
        refs = refs[2 + 3 * n_tab:]
    out_ref = refs[0]
    refs = refs[1:]
    if nxt is not None:
        n_outs, n_subs = refs[:n_out], refs[n_out:n_out + len(n_dils)]
        refs = refs[n_out + len(n_dils):]
    if n_mix:
        stage_o, stage_l = refs[:2]
        refs = refs[2:]
        first = _mixture(o_refs, l_refs, e_ref, stage_o, stage_l, dils, x_ref.shape[0])
    g = gate_ref[...].astype(F32)
    sg = g * (1.0 / (1.0 + jnp.exp(-g)))
    w1 = first.shape[1]
    w2 = w1 + second.shape[1]
    y1 = (first * sg[:, :w1]).astype(BF16)
    y2 = (second[...].astype(F32) * sg[:, w1:w2]).astype(BF16)
    y3 = (third[...].astype(F32) * sg[:, w2:]).astype(BF16)
    xn = x_ref[...] + _dot(y1, w_ref[:w1, :]) + _dot(y2, w_ref[w1:w2, :]) + _dot(y3, w_ref[w2:, :])
    if final:
        xn = xn * lax.rsqrt(jnp.mean(xn * xn, -1, keepdims=True) + NORM_EPS) * g_ref[...]
    out_ref[...] = xn
    if nxt is not None:
        _norm_proj_body(xn, ng_ref, nw_ref, n_tabs, n_outs, n_subs, refs[0] if n_dils else None, n_plan, n_dils)


def _lse_expansion():
    dst = np.arange(A_WIDTH)
    head = dst // HEAD_DIM
    src = (head % 2) * HEAD_DIM + (head // 2) * A_LSE_GROUP
    return jnp.asarray(np.arange(LANES)[:, None] == src[None, :], BF16)


def _out_proj(first, second, third, gate, x2d, w_out, final_gain=None, tm=ROW_TILE, seq=None, next_proj=None):
    t_rows, dm = x2d.shape
    row = lambda a: pl.BlockSpec((tm, a.shape[1]), lambda i: (i, 0))
    full = lambda a: pl.BlockSpec(a.shape, lambda i: (0, 0), pipeline_mode=pl.Buffered(1))
    dils, scratch = (), []
    if isinstance(first, tuple):
        o_list, l_list = first
        dils = tuple(o.shape[1] for o in o_list)
        nt = seq // tm
        sub = lambda a: pl.BlockSpec((None, a.shape[1], tm // a.shape[1], a.shape[3]),
                                     lambda i: (i // nt, 0, i % nt, 0))
        expand = _lse_expansion()
        ins = list(o_list) + list(l_list) + [expand]
        in_specs = [sub(a) for a in ins[:-1]] + [full(expand)]
        scratch = [pltpu.VMEM((len(dils), A_WIDTH // LANES, tm, LANES), F32),
                   pltpu.VMEM((len(dils), tm, LANES), F32)]
    else:
        ins, in_specs = [first], [row(first)]
    for a in (second, third, gate, x2d):
        ins.append(a)
        in_specs.append(row(a))
    ins.append(w_out)
    in_specs.append(full(w_out))
    if final_gain is not None:
        ins.append(final_gain.reshape(1, dm).astype(F32))
        in_specs.append(pl.BlockSpec((1, dm), lambda i: (0, 0)))
    out_specs = [pl.BlockSpec((tm, dm), lambda i: (i, 0))]
    out_shape = [jax.ShapeDtypeStruct((t_rows, dm), F32)]
    nxt = None
    if next_proj is not None:
        n_ins, n_in_specs, n_out_specs, n_out_shape, n_scratch = _proj_operands(next_proj, t_rows, tm, seq)
        ins += n_ins
        in_specs += n_in_specs
        out_specs += n_out_specs
        out_shape += n_out_shape
        scratch = scratch + n_scratch
        nxt = (next_proj['plan'], len(next_proj['tables']), len(next_proj['out_widths']), next_proj['dils'])
    kern = functools.partial(_out_proj_kernel, dils=dils, final=final_gain is not None, nxt=nxt)
    res = pl.pallas_call(
        kern, grid=(t_rows // tm,), in_specs=in_specs, out_specs=out_specs, out_shape=out_shape,
        scratch_shapes=scratch, compiler_params=_cparams("parallel"),
        name="out_proj" if nxt is None else "out_norm_proj",
    )(*ins)
    return res[0] if nxt is None else res


def _mla_prep_kernel(in_ref, qn_ref, kvn_ref, wq_ref, wqs_ref, wk_ref, wv_ref, c_ref, s1_ref, s2_ref,
                     q_out, k_out, v_out):
    c, s = c_ref[...], s1_ref[...] + s2_ref[...]
    k0 = C_Q_RANK + C_KV_RANK
    cq = in_ref[:, :C_Q_RANK].astype(F32)
    ckv = in_ref[:, C_Q_RANK:k0].astype(F32)
    kr = in_ref[:, k0:k0 + LANES].astype(F32)
    kr_sw = in_ref[:, k0 + LANES:].astype(F32)
    cqn = (cq * lax.rsqrt(jnp.mean(cq * cq, -1, keepdims=True) + NORM_EPS) * qn_ref[...]).astype(BF16)
    ckvn = (ckv * lax.rsqrt(jnp.mean(ckv * ckv, -1, keepdims=True) + NORM_EPS) * kvn_ref[...]).astype(BF16)
    q_scale = (C_NOPE + C_ROPE) ** -0.5 * LOG2E
    c_all = jnp.tile(c * q_scale, (1, C_HEADS))
    s_all = jnp.tile(s * q_scale, (1, C_HEADS))
    q_out[...] = (_dot(cqn, wq_ref[...]) * c_all + _dot(cqn, wqs_ref[...]) * s_all).astype(q_out.dtype)
    kpe = kr * c + kr_sw * s
    kn = _dot(ckvn, wk_ref[...])
    for h in range(C_HEADS):
        k_out[:, h * LANES:(h + 1) * LANES] = (kn[:, h * LANES:(h + 1) * LANES] + kpe).astype(k_out.dtype)
    lane = lax.broadcasted_iota(jnp.int32, (1, C_HEADS * LANES), 1)
    ones = jnp.where(lane % LANES >= C_V, 1.0, 0.0)
    v_out[...] = (_dot(ckvn, wv_ref[...]) + ones).astype(v_out.dtype)


def _mla_prep(mla_in, q_norm, kv_norm, wq, wq_sw, wk, wv, tabs, tm=ROW_TILE):
    t_rows = mla_in.shape[0]
    row = lambda w: pl.BlockSpec((tm, w), lambda i: (i, 0))
    full = lambda a: pl.BlockSpec(a.shape, lambda i: (0, 0))
    qn = q_norm.reshape(1, -1).astype(F32)
    kvn = kv_norm.reshape(1, -1).astype(F32)
    return pl.pallas_call(
        _mla_prep_kernel, grid=(t_rows // tm,),
        in_specs=[row(mla_in.shape[1]), full(qn), full(kvn), full(wq), full(wq_sw), full(wk), full(wv),
                  row(LANES), row(LANES), row(LANES)],
        out_specs=[row(C_HEADS * LANES)] * 3,
        out_shape=[jax.ShapeDtypeStruct((t_rows, C_HEADS * LANES), BF16)] * 3,
        compiler_params=_cparams("parallel"), name="mla_prep",
    )(mla_in, qn, kvn, wq, wq_sw, wk, wv, *tabs)


def _mla_attn_kernel(q_ref, k_ref, v_ref, o_ref, m_ref, acc_ref, s_ref, p_ref, alpha_ref, *, tq):
    i = pl.program_id(1)
    diag_ok = (lax.broadcasted_iota(jnp.int32, (tq, tq), 1) <= lax.broadcasted_iota(jnp.int32, (tq, tq), 0))
    m_ref[...] = jnp.full(m_ref.shape, NEG_INF, F32)
    acc_ref[...] = jnp.zeros(acc_ref.shape, F32)

    def block(j, masked):
        rows = pl.ds(pl.multiple_of(j * tq, tq), tq)
        for h in range(C_HEADS):
            sl = slice(h * LANES, (h + 1) * LANES)
            s = _dot_nt(q_ref[:, sl], k_ref[rows, sl])
            s_ref[h] = jnp.where(diag_ok, s, NEG_INF) if masked else s
        for h in range(C_HEADS):
            s = s_ref[h]
            m_old = m_ref[h]
            m_new = jnp.maximum(m_old, jnp.max(s, -1, keepdims=True))
            alpha_ref[h] = jnp.exp2(m_old - m_new)
            p_ref[h] = jnp.exp2(s - jnp.tile(m_new, (1, tq // LANES))).astype(BF16)
            m_ref[h] = m_new
        for h in range(C_HEADS):
            sl = slice(h * LANES, (h + 1) * LANES)
            acc_ref[h] = alpha_ref[h] * acc_ref[h] + _dot(p_ref[h], v_ref[rows, sl])

    def body(j, carry):
        block(j, False)
        return carry

    lax.fori_loop(0, i, body, 0)
    block(i, True)
    for h in range(C_HEADS):
        acc = acc_ref[h]
        o_ref[:, h * C_V:(h + 1) * C_V] = (acc[:, :C_V] / acc[:, C_V:]).astype(o_ref.dtype)


def _mla_attn(q, k, v, b, s, tq=512):
    qw, vw = C_HEADS * LANES, C_HEADS * C_V
    o = pl.pallas_call(
        functools.partial(_mla_attn_kernel, tq=tq), grid=(b, s // tq),
        in_specs=[pl.BlockSpec((None, tq, qw), lambda bi, i: (bi, i, 0)),
                  pl.BlockSpec((None, s, qw), lambda bi, i: (bi, 0, 0)),
                  pl.BlockSpec((None, s, qw), lambda bi, i: (bi, 0, 0))],
        out_specs=pl.BlockSpec((None, tq, vw), lambda bi, i: (bi, i, 0)),
        out_shape=jax.ShapeDtypeStruct((b, s, vw), BF16),
        scratch_shapes=[pltpu.VMEM((C_HEADS, tq, LANES), F32),
                        pltpu.VMEM((C_HEADS, tq, LANES), F32),
                        pltpu.VMEM((C_HEADS, tq, tq), F32),
                        pltpu.VMEM((C_HEADS, tq, tq), BF16),
                        pltpu.VMEM((C_HEADS, tq, LANES), F32)],
        compiler_params=_cparams("parallel", "parallel"), name="mla_attn",
    )(q.reshape(b, s, qw), k.reshape(b, s, qw), v.reshape(b, s, qw))
    return o.reshape(b * s, vw)


def _split3(x):
    hi = x.astype(BF16)
    r1 = x - hi.astype(F32)
    mid = r1.astype(BF16)
    lo = (r1 - mid.astype(F32)).astype(BF16)
    return hi, mid, lo


def _rwkv_kernel(*refs, vres, nchunk):
    c = D_CHUNK
    tr = nchunk * c
    if vres:
        (dp_ref, vf_ref, mu_ref, w0_ref, w2_ref, a0_ref, a2_ref, v0_ref, v2_ref,
         kk_ref, ka_ref, rk_ref, lw_ref, lb_ref, ones_ref, out_ref,
         state_ref, carry_ref, y_ref) = refs
    else:
        (dp_ref, mu_ref, w0_ref, w2_ref, a0_ref, a2_ref,
         kk_ref, ka_ref, rk_ref, lw_ref, lb_ref, ones_ref, out_ref, vf_out,
         state_ref, carry_ref, y_ref) = refs

    @pl.when(pl.program_id(1) == 0)
    def _():
        state_ref[...] = jnp.zeros_like(state_ref)
        carry_ref[...] = jnp.zeros_like(carry_ref)

    hcur = dp_ref[...].astype(F32)
    row = lax.broadcasted_iota(jnp.int32, (tr, 1), 0)
    prev = jnp.where(row == 0, carry_ref[...], pltpu.roll(hcur, 1, 0))
    carry_ref[...] = hcur[tr - 1:tr, :]
    sh = hcur + (prev - hcur) * mu_ref[...]

    w = D_WIDTH
    r = sh[:, 0:w]
    k = sh[:, w:2 * w]
    v = sh[:, 2 * w:3 * w]
    wa = sh[:, 3 * w:3 * w + LANES]
    pre_w = w0_ref[...] + _dot(jnp.tanh(wa).astype(BF16), w2_ref[...])
    logd = -math.exp(-0.5) / (1.0 + jnp.exp(-pre_w))
    pre_a = a0_ref[...] + _dot(wa.astype(BF16), a2_ref[...])
    a = 1.0 / (1.0 + jnp.exp(-pre_a))
    if vres:
        vd = sh[:, 3 * w + LANES:3 * w + 2 * LANES]
        pre_v = v0_ref[...] + _dot(vd.astype(BF16), v2_ref[...])
        v = v + (vf_ref[...].astype(F32) - v) * (1.0 / (1.0 + jnp.exp(-pre_v)))
    else:
        vf_out[...] = v.astype(vf_out.dtype)
    def head_sum(t):
        t_hi = t.astype(BF16)
        t_lo = (t - t_hi.astype(F32)).astype(BF16)
        n = t.shape[1] // LANES
        stack = jnp.concatenate([piece[:, i * LANES:(i + 1) * LANES] for i in range(n) for piece in (t_hi, t_lo)], axis=0)
        res = _dot(stack, ones_ref[...])
        rows = t.shape[0]
        return jnp.concatenate([res[2 * i * rows:(2 * i + 1) * rows] + res[(2 * i + 1) * rows:(2 * i + 2) * rows]
                                for i in range(n)], axis=1)

    kk = k * kk_ref[...]
    kk = kk / jnp.maximum(jnp.sqrt(head_sum(kk * kk)), 1e-12)
    k2 = k * (1.0 + (a - 1.0) * ka_ref[...])

    ti = lax.broadcasted_iota(jnp.int32, (tr, tr), 0)
    tj = lax.broadcasted_iota(jnp.int32, (tr, tr), 1)
    tri = jnp.where((tj <= ti) & (tj >= ti // c * c), 1.0, 0.0).astype(BF16)
    hi, mid, lo = _split3(logd)
    cum = _dot(tri, hi) + _dot(tri, mid) + _dot(tri, lo)
    p_inc = jnp.exp(cum)
    p_exc = jnp.where(row % c == 0, 1.0, pltpu.roll(p_inc, 1, 0))
    p_inv = jnp.exp(-cum)
    p_all = [p_inc[(ci + 1) * c - 1:(ci + 1) * c, :] for ci in range(nchunk)]
    p_all_rows = jnp.concatenate([jnp.broadcast_to(pa, (c, D_WIDTH)) for pa in p_all], axis=0)
    kb = kk * a * p_inv
    kt = k2 * p_inv
    at_b = (-kk * p_exc).astype(BF16)
    rt_b = (r * p_inc).astype(BF16)
    bend_b = (kb * p_all_rows).astype(BF16)
    kend_b = (kt * p_all_rows).astype(BF16)
    vb = v.astype(BF16)

    pc = lax.broadcasted_iota(jnp.int32, (c, LANES), 1) % D_HEAD
    pr = lax.broadcasted_iota(jnp.int32, (c, LANES), 0)
    strict = pc < pr
    incl = pc <= pr
    eye = jnp.where(pc == pr, 1.0, 0.0)
    lo_lane = lax.broadcasted_iota(jnp.int32, (c, LANES), 1) < D_HEAD
    sr = lax.broadcasted_iota(jnp.int32, (LANES, LANES), 0) < D_HEAD
    sc = lax.broadcasted_iota(jnp.int32, (LANES, LANES), 1) < D_HEAD
    same_head = sr == sc

    def pair_diag(t):
        return jnp.concatenate([jnp.where(lo_lane, t, 0.0), jnp.where(lo_lane, 0.0, t)], axis=0).astype(BF16)

    items = [(ci, p) for ci in range(nchunk) for p in range(D_HEADS // 2)]
    rsl = lambda ci: slice(ci * c, (ci + 1) * c)
    psl = lambda p: slice(p * LANES, (p + 1) * LANES)
    sub = lambda arr, it: arr[rsl(it[0]), psl(it[1])]
    lhs = {it: jnp.concatenate([sub(at_b, it), sub(rt_b, it)], axis=0) for it in items}
    mats = {it: _dot_nt(lhs[it], jnp.concatenate([pair_diag(sub(kb, it)), pair_diag(sub(kt, it))], axis=0))
            for it in items}
    r_b = {it: jnp.where(incl, mats[it][c:, :LANES], 0.0).astype(BF16) for it in items}
    x0 = {it: jnp.where(strict, mats[it][:c, :LANES], 0.0) for it in items}
    xp = {it: _dot(x0[it].astype(BF16), pair_diag(x0[it])) for it in items}
    tinv = {it: eye + x0[it] for it in items}
    n_levels = int(math.log2(c))
    for lvl in range(1, n_levels - 1):
        prod = {it: _dot(jnp.concatenate([xp[it], tinv[it]], axis=0).astype(BF16), pair_diag(xp[it])) for it in items}
        xp = {it: prod[it][:c] for it in items}
        tinv = {it: tinv[it] + prod[it][c:] for it in items}
    tinv = {it: (tinv[it] + _dot(tinv[it].astype(BF16), pair_diag(xp[it]))).astype(BF16) for it in items}
    m_k = {it: jnp.concatenate([jnp.where(strict, mats[it][:c, LANES:], 0.0),
                                jnp.where(incl, mats[it][c:, LANES:], 0.0)], axis=0).astype(BF16) for it in items}
    from_v = {it: _dot(m_k[it], pair_diag(sub(v, it))) for it in items}
    upd_v = {it: jnp.where(same_head, _dot_tn(sub(vb, it), sub(kend_b, it)), 0.0) for it in items}

    state = [state_ref[p] for p in range(D_HEADS // 2)]
    pairs = range(D_HEADS // 2)
    for ci in range(nchunk):
        from_state = [_dot_nt(lhs[ci, p], state[p].astype(BF16)) for p in pairs]
        u = [_dot(tinv[ci, p], pair_diag(from_state[p][:c] + from_v[ci, p][:c])) for p in pairs]
        upd_u = [_dot_tn(u[p].astype(BF16), sub(bend_b, (ci, p))) for p in pairs]
        state = [state[p] * p_all[ci][:, psl(p)] + jnp.where(same_head, upd_u[p], 0.0) + upd_v[ci, p] for p in pairs]
        for p in pairs:
            y_ref[rsl(ci), psl(p)] = from_state[p][c:] + from_v[ci, p][c:] + _dot(r_b[ci, p], pair_diag(u[p]))
    for p in pairs:
        state_ref[p] = state[p]

    y = y_ref[...]
    yc = y - head_sum(y) * (1.0 / D_HEAD)
    var = head_sum(yc * yc) * (1.0 / D_HEAD)
    yn = yc * lax.rsqrt(var + RWKV_LN_EPS) * lw_ref[...] + lb_ref[...]
    bonus = head_sum(r * k2 * rk_ref[...]) * v
    out_ref[...] = (yn + bonus).astype(out_ref.dtype)


def _rwkv(dproj, v_first, p, b, s):
    tr = D_STEP_CHUNKS * D_CHUNK
    vres = v_first is not None
    row = lambda wd: pl.BlockSpec((None, tr, wd), lambda bi, i: (bi, i, 0))
    full = lambda a: pl.BlockSpec(a.shape, lambda bi, i: (0, 0))
    ins = [dproj.reshape(b, s, D_PROJ_PAD)]
    in_specs = [row(D_PROJ_PAD)]
    if vres:
        ins.append(v_first.reshape(b, s, D_WIDTH))
        in_specs.append(row(D_WIDTH))
    names = ['mu', 'w0', 'w2', 'a0', 'a2'] + (['v0', 'v2'] if vres else []) + ['k_k', 'k_a', 'r_k', 'lnx_w', 'lnx_b']
    for nm in names:
        ins.append(p[nm])
        in_specs.append(full(p[nm]))
    head_id = np.arange(LANES) // D_HEAD
    ones_bd = jnp.asarray(head_id[:, None] == head_id[None, :], BF16)
    ins.append(ones_bd)
    in_specs.append(full(ones_bd))
    out_shape = [jax.ShapeDtypeStruct((b, s, D_WIDTH), BF16)]
    out_specs = [row(D_WIDTH)]
    if not vres:
        out_shape.append(jax.ShapeDtypeStruct((b, s, D_WIDTH), BF16))
        out_specs.append(row(D_WIDTH))
    res = pl.pallas_call(
        functools.partial(_rwkv_kernel, vres=vres, nchunk=D_STEP_CHUNKS), grid=(b, s // tr),
        in_specs=in_specs, out_specs=out_specs, out_shape=out_shape,
        scratch_shapes=[pltpu.VMEM((D_HEADS // 2, LANES, LANES), F32),
                        pltpu.VMEM((1, D_PROJ_PAD), F32),
                        pltpu.VMEM((tr, D_WIDTH), F32)],
        compiler_params=_cparams("parallel", "arbitrary"), name="rwkv7",
    )(*ins)
    d_out = res[0].reshape(b * s, D_WIDTH)
    vf = v_first if vres else res[1].reshape(b * s, D_WIDTH)
    return d_out, vf


def _rope_tables(positions, dim, theta, period, base, half):
    inv = jnp.exp(-math.log(theta) * jnp.arange(0, dim, 2, dtype=F32) / dim)
    ang = positions.astype(F32)[..., None] * inv
    cos, sin = jnp.cos(ang), jnp.sin(ang)
    lead = ang.shape[:-1]
    fill = lambda width, val: jnp.full(lead + (width,), val, F32)
    rest = period - base - 2 * half

    def lanes(before, lo, hi, after):
        one_period = jnp.concatenate([before, lo, hi, after], axis=-1)
        return jnp.tile(one_period, (1, 1, LANES // period))

    c = lanes(fill(base, 1.0), cos, cos, fill(rest, 1.0))
    s1 = lanes(fill(base, 0.0), fill(half, 0.0), sin, fill(rest, 0.0))
    s2 = lanes(fill(base, 0.0), -sin, fill(half, 0.0), fill(rest, 0.0))
    n = positions.shape[0] * positions.shape[1]
    return tuple(t.reshape(n, LANES) for t in (c, s1, s2))


A_SUB_DILATIONS = tuple(d for d in A_DILATIONS if d > 1)


def _even_proj(norm, w_in, tabs_a, tabs_r):
    q_scale = HEAD_DIM ** -0.5 * LOG2E
    aw = A_WIDTH
    plan = [(0, aw, 0, 0, (0, ROT_DIM // 2, q_scale), 0), (aw, aw, 0, aw, (0, ROT_DIM // 2, 1.0), aw),
            (2 * aw, aw, 0, 2 * aw, None, 2 * aw),
            (1536, 512, 0, 1536, (1, B_QK_DIM // 2, 1.0)), (2048, 512, 0, 2048, None),
            (2560, 256, 1, 0, (None, 0, MEM_Q_SCALE)), (2816, 640, 2, 0, None), (3456, 640, 2, 640, None)]
    return _proj_spec(norm, w_in.astype(BF16), plan, (2560, 256, MIX_WIDTH), (tabs_a, tabs_r),
                      dils=A_SUB_DILATIONS, sub_width=3 * aw)


def _even_mixers(res, b, s, mem_kv):
    main, qm, gate = res[:3]
    main3 = main.reshape(b, s, 2560)
    qkv = {1: main3.reshape(b, 1, s, 2560)}
    qkv.update(zip(A_SUB_DILATIONS, res[3:]))
    outs, lses = [], []
    for dil in A_DILATIONS:
        o, lse = _band_attn(qkv[dil], 0, dil)
        outs.append(o)
        lses.append(lse)
    r_out = _retention(main3)
    m_out = _mem_attn(qm, mem_kv, b, s)
    return (outs, lses), r_out, m_out, gate


def _odd_proj(norm, w_in, d_cols):
    k_dim = w_in.shape[0]
    z = lambda n: jnp.zeros((k_dim, n), F32)
    o = 0
    cq = w_in[:, o:o + C_Q_RANK]; o += C_Q_RANK
    ckv = w_in[:, o:o + C_KV_RANK]; o += C_KV_RANK
    kro = w_in[:, o:o + C_ROPE]; o += C_ROPE
    dp = w_in[:, o:o + d_cols]; o += d_cols
    qm_w = w_in[:, o:o + MEM_WIDTH]; o += MEM_WIDTH
    gate_w = w_in[:, o:]
    dp_pad = jnp.concatenate([dp, z(D_PROJ_PAD - d_cols)], axis=1)
    swap = lambda t: jnp.concatenate([t[..., C_ROPE // 2:], t[..., :C_ROPE // 2]], axis=-1)
    rope_slab = lambda t: jnp.concatenate([z(C_NOPE), t, z(LANES - C_NOPE - C_ROPE)], axis=1)
    w_pad = jnp.concatenate([cq, ckv, rope_slab(kro), rope_slab(swap(kro)), dp_pad, qm_w, gate_w], axis=1)
    m0 = C_Q_RANK + C_KV_RANK + 2 * LANES
    plan = [(0, m0, 0, 0, None), (m0, 896, 1, 0, None), (m0 + 896, 896, 1, 896, None),
            (m0 + 1792, 256, 2, 0, (None, 0, MEM_Q_SCALE)),
            (m0 + 2048, 640, 3, 0, None), (m0 + 2688, 640, 3, 640, None)]
    return _proj_spec(norm, w_pad.astype(BF16), plan, (m0, D_PROJ_PAD, 256, MIX_WIDTH))


def _odd_mixers(res, b, s, mem_kv, tabs_m, v_first, q_norm, w_qb, kv_norm, w_kvb,
                mu_shift, w0, w2, a0, a2, v0, v2, k_k, k_a, r_k, lnx_w, lnx_b):
    mla_in, dproj, qm, gate = res
    vres = v0 is not None
    d_cols = mu_shift.shape[0]
    swap = lambda t: jnp.concatenate([t[..., C_ROPE // 2:], t[..., :C_ROPE // 2]], axis=-1)
    wq = w_qb.reshape(C_Q_RANK, C_HEADS, C_NOPE + C_ROPE)
    slab_pad = ((0, 0), (0, 0), (0, LANES - C_NOPE - C_ROPE))
    wq_sw = jnp.concatenate([wq[..., :C_NOPE], swap(wq[..., C_NOPE:])], axis=-1)
    wq_sw = jnp.pad(wq_sw, slab_pad).reshape(C_Q_RANK, C_HEADS * LANES)
    wq = jnp.pad(wq, slab_pad).reshape(C_Q_RANK, C_HEADS * LANES)
    wkv = w_kvb.reshape(C_KV_RANK, C_HEADS, C_NOPE + C_V)
    wk = jnp.pad(wkv[:, :, :C_NOPE], ((0, 0), (0, 0), (0, LANES - C_NOPE))).reshape(C_KV_RANK, C_HEADS * LANES)
    wv = jnp.pad(wkv[:, :, C_NOPE:], ((0, 0), (0, 0), (0, LANES - C_V))).reshape(C_KV_RANK, C_HEADS * LANES)
    q, k, v = _mla_prep(mla_in, q_norm, kv_norm, wq.astype(BF16), wq_sw.astype(BF16), wk.astype(BF16),
                        wv.astype(BF16), tabs_m)
    c_out = _mla_attn(q, k, v, b, s)

    row = lambda t: t.reshape(1, -1).astype(F32)
    pad_rows = lambda t, top, n: jnp.concatenate(
        [jnp.zeros((top, t.shape[1]), F32), t, jnp.zeros((n - top - t.shape[0], t.shape[1]), F32)], axis=0)
    p = dict(mu=row(jnp.concatenate([mu_shift, jnp.zeros((D_PROJ_PAD - d_cols,), F32)])),
             w0=row(w0), w2=pad_rows(w2, 0, LANES).astype(BF16),
             a0=row(a0), a2=pad_rows(a2, D_DECAY_LORA, LANES).astype(BF16),
             k_k=row(k_k), k_a=row(k_a), r_k=row(r_k), lnx_w=row(lnx_w), lnx_b=row(lnx_b))
    if vres:
        p['v0'] = row(v0)
        p['v2'] = pad_rows(v2, 0, LANES).astype(BF16)
    d_out, v_first = _rwkv(dproj, v_first, p, b, s)
    m_out = _mem_attn(qm, mem_kv, b, s)
    return (c_out, d_out, m_out, gate), v_first


def kernel(x, mem, positions, mem_norm, final_norm, l0_norm, l0_w_in, l0_w_mem_kv, l0_w_out, l1_norm, l1_w_in, l1_q_norm, l1_w_qb, l1_kv_norm, l1_w_kvb, l1_mu_shift, l1_w0, l1_w2, l1_a0, l1_a2, l1_k_k, l1_k_a, l1_r_k, l1_lnx_w, l1_lnx_b, l1_w_mem_kv, l1_w_out, l2_norm, l2_w_in, l2_w_mem_kv, l2_w_out, l3_norm, l3_w_in, l3_q_norm, l3_w_qb, l3_kv_norm, l3_w_kvb, l3_mu_shift, l3_w0, l3_w2, l3_a0, l3_a2, l3_v0, l3_v2, l3_k_k, l3_k_a, l3_r_k, l3_lnx_w, l3_lnx_b, l3_w_mem_kv, l3_w_out):
    b, s, d = x.shape
    x2d = x.reshape(b * s, d)
    mem2d = mem.reshape(b * MEM_TOKENS, d)
    tabs_a = _rope_tables(positions, ROT_DIM, ROPE_THETA, HEAD_DIM, 0, ROT_DIM // 2)
    tabs_r = _rope_tables(positions, B_QK_DIM, RET_THETA, B_QK_DIM, 0, B_QK_DIM // 2)
    tabs_m = _rope_tables(positions, C_ROPE, ROPE_THETA, LANES, C_NOPE, C_ROPE // 2)

    def mem_kv(w):
        plan = [(0, 2 * MEM_WIDTH, 0, 0, None)]
        return _norm_proj(mem2d, mem_norm, w.astype(BF16), plan, (2 * MEM_WIDTH,), tm=MEM_TOKENS)[0]

    proj = [_even_proj(l0_norm, l0_w_in, tabs_a, tabs_r), _odd_proj(l1_norm, l1_w_in, l1_mu_shift.shape[0]),
            _even_proj(l2_norm, l2_w_in, tabs_a, tabs_r), _odd_proj(l3_norm, l3_w_in, l3_mu_shift.shape[0])]
    w_outs = [w.astype(BF16) for w in (l0_w_out, l1_w_out, l2_w_out, l3_w_out)]
    p0 = proj[0]
    res = _norm_proj(x2d, p0['g'], p0['w'], p0['plan'], p0['out_widths'], p0['tables'], dils=p0['dils'],
                     sub_width=p0['sub_width'], seq=s)
    ops = _even_mixers(res, b, s, mem_kv(l0_w_mem_kv))
    x2d, *res = _out_proj(*ops, x2d, w_outs[0], seq=s, next_proj=proj[1])
    ops, v_first = _odd_mixers(res, b, s, mem_kv(l1_w_mem_kv), tabs_m, None, l1_q_norm, l1_w_qb, l1_kv_norm, l1_w_kvb,
                               l1_mu_shift, l1_w0, l1_w2, l1_a0, l1_a2, None, None,
                               l1_k_k, l1_k_a, l1_r_k, l1_lnx_w, l1_lnx_b)
    x2d, *res = _out_proj(*ops, x2d, w_outs[1], seq=s, next_proj=proj[2])
    ops = _even_mixers(res, b, s, mem_kv(l2_w_mem_kv))
    x2d, *res = _out_proj(*ops, x2d, w_outs[2], seq=s, next_proj=proj[3])
    ops, _ = _odd_mixers(res, b, s, mem_kv(l3_w_mem_kv), tabs_m, v_first, l3_q_norm, l3_w_qb, l3_kv_norm, l3_w_kvb,
                         l3_mu_shift, l3_w0, l3_w2, l3_a0, l3_a2, l3_v0, l3_v2,
                         l3_k_k, l3_k_a, l3_r_k, l3_lnx_w, l3_lnx_b)
    out = _out_proj(*ops, x2d, w_outs[3], final_gain=final_norm)
    return out.reshape(b, s, d)
```

```python
import functools
import math

import numpy as np
import jax
import jax.numpy as jnp
from jax import lax
from jax.experimental import pallas as pl
from jax.experimental.pallas import tpu as pltpu

F32 = jnp.float32
BF16 = jnp.bfloat16

LANES = 128
VMEM_LIMIT = 56 * 1024 * 1024
ROW_TILE = 512

D_MODEL = 1024
HEAD_DIM = 64
ROPE_THETA = 500000.0
ROT_DIM = HEAD_DIM // 4
NORM_EPS = 1e-6
NEG_INF = -1e30
LN2 = math.log(2.0)
LOG2E = 1.0 / LN2

A_HEADS = 8
A_WIDTH = A_HEADS * HEAD_DIM
A_PATTERNS = ((128, 1), (512, 4), (2048, 16))
A_BLOCK = 128
A_DILATIONS = tuple(d for (_, d) in A_PATTERNS)
A_LSE_GROUP = HEAD_DIM // (A_HEADS // 2)

B_HEADS = 4
B_QK_DIM = 64
B_V_DIM = 128
B_CHUNK = 128
RET_THETA = 10000.0

MEM_TOKENS = 256
MEM_HEADS = 4
MEM_WIDTH = MEM_HEADS * HEAD_DIM
MEM_Q_SCALE = HEAD_DIM ** -0.5 / math.log(2.0)

C_HEADS = 8
C_Q_RANK = 256
C_KV_RANK = 128
C_NOPE = 64
C_ROPE = 32
C_V = 64

D_HEADS = 8
D_HEAD = 64
D_WIDTH = D_HEADS * D_HEAD
D_DECAY_LORA = 64
D_AAA_LORA = 64
D_MV_LORA = 32
RWKV_LN_EPS = 64e-5
D_CHUNK = 64
D_STEP_CHUNKS = 4
D_PROJ_PAD = 3 * D_WIDTH + 2 * LANES

MIX_WIDTH = 1280


def _cparams(*sem):
    return pltpu.CompilerParams(dimension_semantics=sem, vmem_limit_bytes=VMEM_LIMIT)


def _dot(a, b):
    return jnp.dot(a, b, preferred_element_type=F32)


def _dot_nt(a, b):
    return lax.dot_general(a, b, (((1,), (1,)), ((), ())), preferred_element_type=F32)


def _dot_tn(a, b):
    return lax.dot_general(a, b, (((0,), (0,)), ((), ())), preferred_element_type=F32)


def _rot_slab(x, c, s1, s2, half):
    return x * c + pltpu.roll(x, half, 1) * s1 + pltpu.roll(x, LANES - half, 1) * s2


def _rotary(x, c, s1, s2, half):
    slabs = [_rot_slab(x[:, i:i + LANES], c, s1, s2, half) for i in range(0, x.shape[1], LANES)]
    return slabs[0] if len(slabs) == 1 else jnp.concatenate(slabs, axis=1)


def _norm_proj_kernel(*refs, plan, n_tab, n_out, dils):
    x_ref, g_ref, w_ref = refs[:3]
    tabs = refs[3:3 + 3 * n_tab]
    outs = refs[3 + 3 * n_tab:3 + 3 * n_tab + n_out]
    sub_outs = refs[3 + 3 * n_tab + n_out:3 + 3 * n_tab + n_out + len(dils)]
    stage_ref = refs[-1] if dils else None
    _norm_proj_body(x_ref[...].astype(F32), g_ref, w_ref, tabs, outs, sub_outs, stage_ref, plan, dils)


def _norm_proj_body(x, g_ref, w_ref, tabs, outs, sub_outs, stage_ref, plan, dils):
    tm = x.shape[0]
    ms = jnp.mean(x * x, axis=-1, keepdims=True)
    xn = (x * lax.rsqrt(ms + NORM_EPS) * g_ref[...]).astype(BF16)
    n_staged = [0]

    def epilogue(entry, acc):
        (_, width, oi, oc, rot, sub_col) = entry
        if rot is not None:
            t, half, post_scale = rot
            if t is not None:
                acc = _rotary(acc, tabs[3 * t][...], tabs[3 * t + 1][...], tabs[3 * t + 2][...], half)
            if post_scale != 1.0:
                acc = acc * post_scale
        outs[oi][:, oc:oc + width] = acc.astype(outs[oi].dtype)
        if sub_col is not None:
            slot = n_staged[0]
            n_staged[0] += 1
            for c in range(width // LANES):
                stage_ref[slot, c] = acc[:, c * LANES:(c + 1) * LANES]
            for d, sub in zip(dils, sub_outs):
                for r in range(d):
                    for c in range(width // LANES):
                        col = sub_col + c * LANES
                        sub[r, :, col:col + LANES] = stage_ref[slot, c, pl.ds(r, tm // d, stride=d), :].astype(sub.dtype)

    pending = None
    for entry in plan:
        acc = _dot(xn, w_ref[:, entry[0]:entry[0] + entry[1]])
        if pending is not None:
            epilogue(*pending)
        pending = (entry, acc)
    epilogue(*pending)


def _norm_proj(x2d, g, w, plan, out_widths, tables=(), tm=ROW_TILE, dils=(), sub_width=0, seq=None):
    t_rows, k = x2d.shape
    assert t_rows % tm == 0
    proj = _proj_spec(g, w, plan, out_widths, tables, dils, sub_width)
    ins, in_specs, out_specs, out_shape, scratch = _proj_operands(proj, t_rows, tm, seq)
    kern = functools.partial(_norm_proj_kernel, plan=proj['plan'], n_tab=len(tables), n_out=len(out_widths),
                             dils=tuple(dils))
    return pl.pallas_call(
        kern, grid=(t_rows // tm,), in_specs=[pl.BlockSpec((tm, k), lambda i: (i, 0))] + in_specs,
        out_specs=out_specs, out_shape=out_shape,
        scratch_shapes=scratch, compiler_params=_cparams("parallel"), name="norm_proj",
    )(x2d, *ins)


def _proj_spec(g, w, plan, out_widths, tables=(), dils=(), sub_width=0):
    plan = tuple(tuple(p) + (None,) * (6 - len(p)) for p in plan)
    return dict(g=g, w=w, plan=plan, out_widths=tuple(out_widths), tables=tuple(tables), dils=tuple(dils),
                sub_width=sub_width)


def _proj_operands(proj, t_rows, tm, seq):
    k, n = proj['w'].shape
    flat_tabs = [t for tab in proj['tables'] for t in tab]
    ins = [proj['g'].reshape(1, k).astype(F32), proj['w']] + flat_tabs
    in_specs = [pl.BlockSpec((1, k), lambda i: (0, 0)),
                pl.BlockSpec((k, n), lambda i: (0, 0), pipeline_mode=pl.Buffered(1))]
    in_specs += [pl.BlockSpec((tm, LANES), lambda i: (i, 0)) for _ in flat_tabs]
    out_specs = [pl.BlockSpec((tm, ow), lambda i: (i, 0)) for ow in proj['out_widths']]
    out_shape = [jax.ShapeDtypeStruct((t_rows, ow), BF16) for ow in proj['out_widths']]
    scratch = []
    if proj['dils']:
        nt = seq // tm
        sub_width = proj['sub_width']
        for d in proj['dils']:
            out_specs.append(pl.BlockSpec((None, d, tm // d, sub_width), lambda i: (i // nt, 0, i % nt, 0)))
            out_shape.append(jax.ShapeDtypeStruct((t_rows // seq, d, seq // d, sub_width), BF16))
        staged = [p[1] for p in proj['plan'] if p[5] is not None]
        scratch = [pltpu.VMEM((len(staged), max(staged) // LANES, tm, LANES), F32)]
    return ins, in_specs, out_specs, out_shape, scratch


def _band_attn_kernel(q_ref, kp_ref, kc_ref, vp_ref, vc_ref, o_ref, lse_ref, s_ref, p_ref, m_ref, *, nq):
    i = pl.program_id(2)
    qi = lax.broadcasted_iota(jnp.int32, (A_BLOCK, A_BLOCK), 0)
    kj = lax.broadcasted_iota(jnp.int32, (A_BLOCK, A_BLOCK), 1)
    bias_prev = jnp.where(kj >= qi, 0.0, NEG_INF)
    bias_cur = jnp.where(kj <= qi, 0.0, NEG_INF)
    bias_first = bias_prev + jnp.where(i > 0, 0.0, NEG_INF)
    lane = lax.broadcasted_iota(jnp.int32, (A_BLOCK, LANES), 1)
    lo = lane < HEAD_DIM
    keep = (jnp.where(lo, 1.0, 0.0).astype(BF16), jnp.where(lo, 0.0, 1.0).astype(BF16))
    npair = A_HEADS // 2

    def operands(u, pair):
        sl = slice(pair * LANES, (pair + 1) * LANES)
        rows = slice(u * A_BLOCK, (u + 1) * A_BLOCK)
        if u == 0:
            return rows, sl, kp_ref[:, sl], vp_ref[:, sl], bias_first
        prow = slice((u - 1) * A_BLOCK, u * A_BLOCK)
        return rows, sl, kc_ref[prow, sl], vc_ref[prow, sl], bias_prev

    for u in range(nq):
        for pair in range(npair):
            rows, sl, kp, _, bp = operands(u, pair)
            q = q_ref[rows, sl]
            kc = kc_ref[rows, sl]
            for half in range(2):
                idx = (u * npair + pair) * 2 + half
                qh = q * keep[half]
                s_ref[idx, :, :A_BLOCK] = _dot_nt(qh, kp) + bp
                s_ref[idx, :, A_BLOCK:] = _dot_nt(qh, kc) + bias_cur
    for idx in range(nq * A_HEADS):
        s = s_ref[idx]
        m = jnp.max(s, -1, keepdims=True)
        p_ref[idx] = jnp.exp2(s - m).astype(BF16)
        m_ref[idx] = jnp.broadcast_to(m, (A_BLOCK, LANES))
    for u in range(nq):
        lse = None
        for pair in range(npair):
            rows, sl, _, vp, _ = operands(u, pair)
            vc = vc_ref[rows, sl]
            idx0 = (u * npair + pair) * 2
            o = [_dot(p_ref[idx0 + half, :, :A_BLOCK], vp * keep[half] + keep[1 - half])
                 + _dot(p_ref[idx0 + half, :, A_BLOCK:], vc * keep[half] + keep[1 - half]) for half in range(2)]
            den = pltpu.roll(jnp.where(lo, o[1], o[0]), HEAD_DIM, 1)
            o_ref[rows, sl] = (jnp.where(lo, o[0], o[1]) / den).astype(o_ref.dtype)
            lse_pair = jnp.where(lo, m_ref[idx0], m_ref[idx0 + 1]) * LN2 + jnp.log(den)
            lse = lse_pair if lse is None else jnp.where(lane % HEAD_DIM // A_LSE_GROUP == pair, lse_pair, lse)
        lse_ref[rows, :] = lse


def _single_block_attn_kernel(q_ref, k_ref, v_ref, o_ref, lse_ref, s_ref, p_ref, m_ref, *, nres):
    qi = lax.broadcasted_iota(jnp.int32, (A_BLOCK, A_BLOCK), 0)
    kj = lax.broadcasted_iota(jnp.int32, (A_BLOCK, A_BLOCK), 1)
    bias = jnp.where(kj <= qi, 0.0, NEG_INF)
    lane = lax.broadcasted_iota(jnp.int32, (A_BLOCK, LANES), 1)
    lo = lane < HEAD_DIM
    keep = (jnp.where(lo, 1.0, 0.0).astype(BF16), jnp.where(lo, 0.0, 1.0).astype(BF16))
    npair = A_HEADS // 2
    for r in range(nres):
        for pair in range(npair):
            sl = slice(pair * LANES, (pair + 1) * LANES)
            for half in range(2):
                s_ref[(r * npair + pair) * 2 + half] = _dot_nt(q_ref[r, :, sl] * keep[half], k_ref[r, :, sl]) + bias
    for idx in range(nres * A_HEADS):
        s = s_ref[idx]
        m = jnp.max(s, -1, keepdims=True)
        p_ref[idx] = jnp.exp2(s - m).astype(BF16)
        m_ref[idx] = jnp.broadcast_to(m, (A_BLOCK, LANES))
    for r in range(nres):
        lse = None
        for pair in range(npair):
            sl = slice(pair * LANES, (pair + 1) * LANES)
            idx0 = (r * npair + pair) * 2
            v = v_ref[r, :, sl]
            o = [_dot(p_ref[idx0 + half], v * keep[half] + keep[1 - half]) for half in range(2)]
            den = pltpu.roll(jnp.where(lo, o[1], o[0]), HEAD_DIM, 1)
            o_ref[r, :, sl] = (jnp.where(lo, o[0], o[1]) / den).astype(o_ref.dtype)
            lse_pair = jnp.where(lo, m_ref[idx0], m_ref[idx0 + 1]) * LN2 + jnp.log(den)
            lse = lse_pair if lse is None else jnp.where(lane % HEAD_DIM // A_LSE_GROUP == pair, lse_pair, lse)
        lse_ref[r] = lse


def _single_block_attn(qkv, col0, dilation, nres=4):
    b, d, sub_len, _ = qkv.shape
    assert sub_len == A_BLOCK and d % nres == 0
    spec = lambda col, w: pl.BlockSpec((None, nres, A_BLOCK, w), lambda bi, r: (bi, r, 0, col))
    return pl.pallas_call(
        functools.partial(_single_block_attn_kernel, nres=nres), grid=(b, d // nres),
        in_specs=[spec(col0, A_WIDTH), spec(col0 + 1, A_WIDTH), spec(col0 + 2, A_WIDTH)],
        out_specs=[spec(0, A_WIDTH), spec(0, LANES)],
        out_shape=[jax.ShapeDtypeStruct((b, d, sub_len, A_WIDTH), BF16),
                   jax.ShapeDtypeStruct((b, d, sub_len, LANES), F32)],
        scratch_shapes=[pltpu.VMEM((nres * A_HEADS, A_BLOCK, A_BLOCK), F32),
                        pltpu.VMEM((nres * A_HEADS, A_BLOCK, A_BLOCK), BF16),
                        pltpu.VMEM((nres * A_HEADS, A_BLOCK, LANES), F32)],
        compiler_params=_cparams("parallel", "parallel"), name="band_attn_d%d" % dilation,
    )(qkv, qkv, qkv)


def _band_attn(qkv, col0, dilation):
    b, d, sub_len, _ = qkv.shape
    assert d == dilation
    if sub_len == A_BLOCK:
        return _single_block_attn(qkv, col0, dilation)
    nq = 4 if sub_len % (4 * A_BLOCK) == 0 else 1
    tq = nq * A_BLOCK

    def cur(col):
        return pl.BlockSpec((None, None, tq, A_WIDTH), lambda bi, r, i: (bi, r, i, col0 + col))

    def prev(col):
        return pl.BlockSpec((None, None, A_BLOCK, A_WIDTH),
                            lambda bi, r, i: (bi, r, jnp.maximum(i * nq - 1, 0), col0 + col))

    return pl.pallas_call(
        functools.partial(_band_attn_kernel, nq=nq), grid=(b, d, sub_len // tq),
        in_specs=[cur(0), prev(1), cur(1), prev(2), cur(2)],
        out_specs=[pl.BlockSpec((None, None, tq, A_WIDTH), lambda bi, r, i: (bi, r, i, 0)),
                   pl.BlockSpec((None, None, tq, LANES), lambda bi, r, i: (bi, r, i, 0))],
        out_shape=[jax.ShapeDtypeStruct((b, d, sub_len, A_WIDTH), BF16),
                   jax.ShapeDtypeStruct((b, d, sub_len, LANES), F32)],
        scratch_shapes=[pltpu.VMEM((nq * A_HEADS, A_BLOCK, 2 * A_BLOCK), F32),
                        pltpu.VMEM((nq * A_HEADS, A_BLOCK, 2 * A_BLOCK), BF16),
                        pltpu.VMEM((nq * A_HEADS, A_BLOCK, LANES), F32)],
        compiler_params=_cparams("parallel", "parallel", "parallel"), name="band_attn_d%d" % dilation,
    )(qkv, qkv, qkv, qkv, qkv)


def _retention_kernel(qk_ref, v_ref, o_ref, state_ref, *, nchunk):
    c = B_CHUNK
    @pl.when(pl.program_id(1) == 0)
    def _():
        state_ref[...] = jnp.zeros_like(state_ref)

    scale = B_QK_DIM ** -0.5
    qi = lax.broadcasted_iota(jnp.int32, (c, c), 0)
    kj = lax.broadcasted_iota(jnp.int32, (c, c), 1)
    diff = (qi - kj).astype(F32)
    row = lax.broadcasted_iota(jnp.int32, (c, 1), 0).astype(F32)
    lo = lax.broadcasted_iota(jnp.int32, (c, LANES), 1) < B_QK_DIM
    keep = (jnp.where(lo, 1.0, 0.0).astype(BF16), jnp.where(lo, 0.0, 1.0).astype(BF16))
    log_g = [math.log1p(-2.0 ** (-5.0 - h)) for h in range(B_HEADS)]
    d_in = [jnp.where(diff >= 0, jnp.exp(jnp.maximum(diff, 0.0) * lg), 0.0) * scale for lg in log_g]
    from_start = [jnp.exp((row + 1.0) * lg) for lg in log_g]
    to_end = [jnp.exp((c - 1.0 - row) * lg) * scale for lg in log_g]
    items = [(ci, h) for ci in range(nchunk) for h in range(B_HEADS)]
    rows = lambda ci: slice(ci * c, (ci + 1) * c)
    qsl = lambda h: slice(h // 2 * LANES, (h // 2 + 1) * LANES)
    ksl = lambda h: slice(B_HEADS * B_QK_DIM + h // 2 * LANES, B_HEADS * B_QK_DIM + (h // 2 + 1) * LANES)
    vsl = lambda h: slice(h * B_V_DIM, (h + 1) * B_V_DIM)
    qm = {(ci, h): qk_ref[rows(ci), qsl(h)] * keep[h % 2] for ci, h in items}
    scores = {(ci, h): (_dot_nt(qm[ci, h], qk_ref[rows(ci), ksl(h)]) * d_in[h]).astype(BF16) for ci, h in items}
    o_in = {(ci, h): _dot(scores[ci, h], v_ref[rows(ci), vsl(h)]) for ci, h in items}
    kd = {(ci, h): ((qk_ref[rows(ci), ksl(h)] * keep[h % 2]).astype(F32) * to_end[h]).astype(BF16) for ci, h in items}
    kv = {(ci, h): _dot_tn(kd[ci, h], v_ref[rows(ci), vsl(h)]) for ci, h in items}
    state = [state_ref[h] for h in range(B_HEADS)]
    for ci in range(nchunk):
        o = [o_in[ci, h] + _dot(qm[ci, h], state[h].astype(BF16)) * from_start[h] for h in range(B_HEADS)]
        state = [state[h] * math.exp(c * log_g[h]) + kv[ci, h] for h in range(B_HEADS)]
        for h in range(B_HEADS):
            on = o[h] * lax.rsqrt(jnp.mean(o[h] * o[h], -1, keepdims=True) + NORM_EPS)
            o_ref[rows(ci), vsl(h)] = on.astype(o_ref.dtype)
    for h in range(B_HEADS):
        state_ref[h] = state[h]


def _retention(main, nchunk=4):
    b, s, _ = main.shape
    blk = (None, nchunk * B_CHUNK, 512)
    o = pl.pallas_call(
        functools.partial(_retention_kernel, nchunk=nchunk), grid=(b, s // (nchunk * B_CHUNK)),
        in_specs=[pl.BlockSpec(blk, lambda bi, i: (bi, i, 3)), pl.BlockSpec(blk, lambda bi, i: (bi, i, 4))],
        out_specs=pl.BlockSpec(blk, lambda bi, i: (bi, i, 0)),
        out_shape=jax.ShapeDtypeStruct((b, s, B_HEADS * B_V_DIM), BF16),
        scratch_shapes=[pltpu.VMEM((B_HEADS, LANES, B_V_DIM), F32)],
        compiler_params=_cparams("parallel", "arbitrary"), name="retention",
    )(main, main)
    return o.reshape(b * s, B_HEADS * B_V_DIM)


def _mem_attn_kernel(q_ref, kv_ref, o_ref, s_ref, p_ref):
    tq = q_ref.shape[0]
    lo_q = lax.broadcasted_iota(jnp.int32, (tq, LANES), 1) < HEAD_DIM
    lo_k = lax.broadcasted_iota(jnp.int32, (MEM_TOKENS, LANES), 1) < HEAD_DIM
    keep_q = (jnp.where(lo_q, 1.0, 0.0).astype(BF16), jnp.where(lo_q, 0.0, 1.0).astype(BF16))
    keep_k = (jnp.where(lo_k, 1.0, 0.0).astype(BF16), jnp.where(lo_k, 0.0, 1.0).astype(BF16))
    npair = MEM_HEADS // 2
    for pair in range(npair):
        sl = slice(pair * LANES, (pair + 1) * LANES)
        for half in range(2):
            s_ref[2 * pair + half] = _dot_nt(q_ref[:, sl] * keep_q[half], kv_ref[:, sl])
    for idx in range(MEM_HEADS):
        s = s_ref[idx]
        p_ref[idx] = jnp.exp2(s - jnp.max(s, -1, keepdims=True)).astype(BF16)
    for pair in range(npair):
        sl = slice(pair * LANES, (pair + 1) * LANES)
        v = kv_ref[:, MEM_WIDTH + pair * LANES:MEM_WIDTH + (pair + 1) * LANES]
        o = [_dot(p_ref[2 * pair + half], v * keep_k[half] + keep_k[1 - half]) for half in range(2)]
        den = pltpu.roll(jnp.where(lo_q, o[1], o[0]), HEAD_DIM, 1)
        o_ref[:, sl] = (jnp.where(lo_q, o[0], o[1]) / den).astype(o_ref.dtype)


def _mem_attn(qm, kv, b, s, tq=512):
    o = pl.pallas_call(
        _mem_attn_kernel, grid=(b, s // tq),
        in_specs=[pl.BlockSpec((None, tq, MEM_WIDTH), lambda bi, i: (bi, i, 0)),
                  pl.BlockSpec((None, MEM_TOKENS, 2 * MEM_WIDTH), lambda bi, i: (bi, 0, 0))],
        out_specs=pl.BlockSpec((None, tq, MEM_WIDTH), lambda bi, i: (bi, i, 0)),
        out_shape=jax.ShapeDtypeStruct((b, s, MEM_WIDTH), BF16),
        scratch_shapes=[pltpu.VMEM((MEM_HEADS, tq, MEM_TOKENS), F32),
                        pltpu.VMEM((MEM_HEADS, tq, MEM_TOKENS), BF16)],
        compiler_params=_cparams("parallel", "parallel"), name="mem_attn",
    )(qm.reshape(b, s, MEM_WIDTH), kv.reshape(b, MEM_TOKENS, 2 * MEM_WIDTH))
    return o.reshape(b * s, MEM_WIDTH)


def _mixture(o_refs, l_refs, e_ref, stage_o, stage_l, dils, tm):
    n_slab = A_WIDTH // LANES
    lses = []
    for pi, d in enumerate(dils):
        if d == 1:
            lses.append(l_refs[pi][0])
        else:
            for r in range(d):
                stage_l[pi, pl.ds(r, tm // d, stride=d), :] = l_refs[pi][r]
                for c in range(n_slab):
                    stage_o[pi, c, pl.ds(r, tm // d, stride=d), :] = (
                        o_refs[pi][r, :, c * LANES:(c + 1) * LANES].astype(F32))
            lses.append(stage_l[pi])
    m = functools.reduce(jnp.maximum, lses)
    es = [jnp.exp(l - m) for l in lses]
    inv = 1.0 / functools.reduce(lambda a, b: a + b, es)
    first = None
    for pi, d in enumerate(dils):
        wgt = es[pi] * inv
        w_hi = wgt.astype(BF16)
        w_lo = (wgt - w_hi.astype(F32)).astype(BF16)
        w_full = _dot(w_hi, e_ref[...]) + _dot(w_lo, e_ref[...])
        if d == 1:
            o = o_refs[pi][0].astype(F32)
        else:
            o = jnp.concatenate([stage_o[pi, c] for c in range(n_slab)], axis=1)
        first = w_full * o if first is None else first + w_full * o
    return first


def _out_proj_kernel(*refs, dils, final, nxt):
    n_mix = len(dils)
    if n_mix:
        o_refs, l_refs, e_ref = refs[:n_mix], refs[n_mix:2 * n_mix], refs[2 * n_mix]
        refs = refs[2 * n_mix + 1:]
    else:
        first = refs[0][...].astype(F32)
        refs = refs[1:]
    second, third, gate_ref, x_ref, w_ref = refs[:5]
    refs = refs[5:]
    if final:
        g_ref = refs[0]
        refs = refs[1:]
    if nxt is not None:
        n_plan, n_tab, n_out, n_dils = nxt
        ng_ref, nw_ref = refs[:2]
        n_tabs = refs[2:2 + 3 * n_tab]
        refs = refs[2 + 3 * n_tab:]
    out_ref = refs[0]
    refs = refs[1:]
    if nxt is not None:
        n_outs, n_subs = refs[:n_out], refs[n_out:n_out + len(n_dils)]
        refs = refs[n_out + len(n_dils):]
    if n_mix:
        stage_o, stage_l = refs[:2]
        refs = refs[2:]
        first = _mixture(o_refs, l_refs, e_ref, stage_o, stage_l, dils, x_ref.shape[0])
    g = gate_ref[...].astype(F32)
    sg = g * (1.0 / (1.0 + jnp.exp(-g)))
    w1 = first.shape[1]
    w2 = w1 + second.shape[1]
    y1 = (first * sg[:, :w1]).astype(BF16)
    y2 = (second[...].astype(F32) * sg[:, w1:w2]).astype(BF16)
    y3 = (third[...].astype(F32) * sg[:, w2:]).astype(BF16)
    xn = x_ref[...] + _dot(y1, w_ref[:w1, :]) + _dot(y2, w_ref[w1:w2, :]) + _dot(y3, w_ref[w2:, :])
    if final:
        xn = xn * lax.rsqrt(jnp.mean(xn * xn, -1, keepdims=True) + NORM_EPS) * g_ref[...]
    out_ref[...] = xn
    if nxt is not None:
        _norm_proj_body(xn, ng_ref, nw_ref, n_tabs, n_outs, n_subs, refs[0] if n_dils else None, n_plan, n_dils)


def _lse_expansion():
    dst = np.arange(A_WIDTH)
    head = dst // HEAD_DIM
    src = (head % 2) * HEAD_DIM + (head // 2) * A_LSE_GROUP
    return jnp.asarray(np.arange(LANES)[:, None] == src[None, :], BF16)


def _out_proj(first, second, third, gate, x2d, w_out, final_gain=None, tm=ROW_TILE, seq=None, next_proj=None):
    t_rows, dm = x2d.shape
    row = lambda a: pl.BlockSpec((tm, a.shape[1]), lambda i: (i, 0))
    full = lambda a: pl.BlockSpec(a.shape, lambda i: (0, 0), pipeline_mode=pl.Buffered(1))
    dils, scratch = (), []
    if isinstance(first, tuple):
        o_list, l_list = first
        dils = tuple(o.shape[1] for o in o_list)
        nt = seq // tm
        sub = lambda a: pl.BlockSpec((None, a.shape[1], tm // a.shape[1], a.shape[3]),
                                     lambda i: (i // nt, 0, i % nt, 0))
        expand = _lse_expansion()
        ins = list(o_list) + list(l_list) + [expand]
        in_specs = [sub(a) for a in ins[:-1]] + [full(expand)]
        scratch = [pltpu.VMEM((len(dils), A_WIDTH // LANES, tm, LANES), F32),
                   pltpu.VMEM((len(dils), tm, LANES), F32)]
    else:
        ins, in_specs = [first], [row(first)]
    for a in (second, third, gate, x2d):
        ins.append(a)
        in_specs.append(row(a))
    ins.append(w_out)
    in_specs.append(full(w_out))
    if final_gain is not None:
        ins.append(final_gain.reshape(1, dm).astype(F32))
        in_specs.append(pl.BlockSpec((1, dm), lambda i: (0, 0)))
    out_specs = [pl.BlockSpec((tm, dm), lambda i: (i, 0))]
    out_shape = [jax.ShapeDtypeStruct((t_rows, dm), F32)]
    nxt = None
    if next_proj is not None:
        n_ins, n_in_specs, n_out_specs, n_out_shape, n_scratch = _proj_operands(next_proj, t_rows, tm, seq)
        ins += n_ins
        in_specs += n_in_specs
        out_specs += n_out_specs
        out_shape += n_out_shape
        scratch = scratch + n_scratch
        nxt = (next_proj['plan'], len(next_proj['tables']), len(next_proj['out_widths']), next_proj['dils'])
    kern = functools.partial(_out_proj_kernel, dils=dils, final=final_gain is not None, nxt=nxt)
    res = pl.pallas_call(
        kern, grid=(t_rows // tm,), in_specs=in_specs, out_specs=out_specs, out_shape=out_shape,
        scratch_shapes=scratch, compiler_params=_cparams("parallel"),
        name="out_proj" if nxt is None else "out_norm_proj",
    )(*ins)
    return res[0] if nxt is None else res


def _mla_prep_kernel(in_ref, qn_ref, kvn_ref, wq_ref, wqs_ref, wk_ref, wv_ref, c_ref, s1_ref, s2_ref,
                     q_out, k_out, v_out):
    c, s = c_ref[...], s1_ref[...] + s2_ref[...]
    k0 = C_Q_RANK + C_KV_RANK
    cq = in_ref[:, :C_Q_RANK].astype(F32)
    ckv = in_ref[:, C_Q_RANK:k0].astype(F32)
    kr = in_ref[:, k0:k0 + LANES].astype(F32)
    kr_sw = in_ref[:, k0 + LANES:].astype(F32)
    cqn = (cq * lax.rsqrt(jnp.mean(cq * cq, -1, keepdims=True) + NORM_EPS) * qn_ref[...]).astype(BF16)
    ckvn = (ckv * lax.rsqrt(jnp.mean(ckv * ckv, -1, keepdims=True) + NORM_EPS) * kvn_ref[...]).astype(BF16)
    q_scale = (C_NOPE + C_ROPE) ** -0.5 * LOG2E
    c_all = jnp.tile(c * q_scale, (1, C_HEADS))
    s_all = jnp.tile(s * q_scale, (1, C_HEADS))
    q_out[...] = (_dot(cqn, wq_ref[...]) * c_all + _dot(cqn, wqs_ref[...]) * s_all).astype(q_out.dtype)
    kpe = kr * c + kr_sw * s
    kn = _dot(ckvn, wk_ref[...])
    for h in range(C_HEADS):
        k_out[:, h * LANES:(h + 1) * LANES] = (kn[:, h * LANES:(h + 1) * LANES] + kpe).astype(k_out.dtype)
    lane = lax.broadcasted_iota(jnp.int32, (1, C_HEADS * LANES), 1)
    ones = jnp.where(lane % LANES >= C_V, 1.0, 0.0)
    v_out[...] = (_dot(ckvn, wv_ref[...]) + ones).astype(v_out.dtype)


def _mla_prep(mla_in, q_norm, kv_norm, wq, wq_sw, wk, wv, tabs, tm=ROW_TILE):
    t_rows = mla_in.shape[0]
    row = lambda w: pl.BlockSpec((tm, w), lambda i: (i, 0))
    full = lambda a: pl.BlockSpec(a.shape, lambda i: (0, 0))
    qn = q_norm.reshape(1, -1).astype(F32)
    kvn = kv_norm.reshape(1, -1).astype(F32)
    return pl.pallas_call(
        _mla_prep_kernel, grid=(t_rows // tm,),
        in_specs=[row(mla_in.shape[1]), full(qn), full(kvn), full(wq), full(wq_sw), full(wk), full(wv),
                  row(LANES), row(LANES), row(LANES)],
        out_specs=[row(C_HEADS * LANES)] * 3,
        out_shape=[jax.ShapeDtypeStruct((t_rows, C_HEADS * LANES), BF16)] * 3,
        compiler_params=_cparams("parallel"), name="mla_prep",
    )(mla_in, qn, kvn, wq, wq_sw, wk, wv, *tabs)


def _mla_attn_kernel(q_ref, k_ref, v_ref, o_ref, m_ref, acc_ref, p_ref, alpha_ref, *, tq):
    i = pl.program_id(1)
    diag_ok = (lax.broadcasted_iota(jnp.int32, (tq, tq), 1) <= lax.broadcasted_iota(jnp.int32, (tq, tq), 0))
    m_ref[...] = jnp.full(m_ref.shape, NEG_INF, F32)
    acc_ref[...] = jnp.zeros(acc_ref.shape, F32)

    def block(j, masked):
        rows = pl.ds(pl.multiple_of(j * tq, tq), tq)
        for h in range(C_HEADS):
            sl = slice(h * LANES, (h + 1) * LANES)
            s = _dot_nt(q_ref[:, sl], k_ref[rows, sl])
            if masked:
                s = jnp.where(diag_ok, s, NEG_INF)
            m_old = m_ref[h]
            m_new = jnp.maximum(m_old, jnp.max(s, -1, keepdims=True))
            alpha_ref[h] = jnp.exp2(m_old - m_new)
            p_ref[h] = jnp.exp2(s - jnp.tile(m_new, (1, tq // LANES))).astype(BF16)
            m_ref[h] = m_new
        for h in range(C_HEADS):
            sl = slice(h * LANES, (h + 1) * LANES)
            acc_ref[h] = alpha_ref[h] * acc_ref[h] + _dot(p_ref[h], v_ref[rows, sl])

    def body(j, carry):
        block(j, False)
        return carry

    lax.fori_loop(0, i, body, 0)
    block(i, True)
    for h in range(C_HEADS):
        acc = acc_ref[h]
        o_ref[:, h * C_V:(h + 1) * C_V] = (acc[:, :C_V] / acc[:, C_V:]).astype(o_ref.dtype)


def _mla_attn(q, k, v, b, s, tq=512):
    qw, vw = C_HEADS * LANES, C_HEADS * C_V
    o = pl.pallas_call(
        functools.partial(_mla_attn_kernel, tq=tq), grid=(b, s // tq),
        in_specs=[pl.BlockSpec((None, tq, qw), lambda bi, i: (bi, i, 0)),
                  pl.BlockSpec((None, s, qw), lambda bi, i: (bi, 0, 0)),
                  pl.BlockSpec((None, s, qw), lambda bi, i: (bi, 0, 0))],
        out_specs=pl.BlockSpec((None, tq, vw), lambda bi, i: (bi, i, 0)),
        out_shape=jax.ShapeDtypeStruct((b, s, vw), BF16),
        scratch_shapes=[pltpu.VMEM((C_HEADS, tq, LANES), F32),
                        pltpu.VMEM((C_HEADS, tq, LANES), F32),
                        pltpu.VMEM((C_HEADS, tq, tq), BF16),
                        pltpu.VMEM((C_HEADS, tq, LANES), F32)],
        compiler_params=_cparams("parallel", "parallel"), name="mla_attn",
    )(q.reshape(b, s, qw), k.reshape(b, s, qw), v.reshape(b, s, qw))
    return o.reshape(b * s, vw)


def _split3(x):
    hi = x.astype(BF16)
    r1 = x - hi.astype(F32)
    mid = r1.astype(BF16)
    lo = (r1 - mid.astype(F32)).astype(BF16)
    return hi, mid, lo


def _rwkv_kernel(*refs, vres, nchunk):
    c = D_CHUNK
    tr = nchunk * c
    if vres:
        (dp_ref, vf_ref, mu_ref, w0_ref, w2_ref, a0_ref, a2_ref, v0_ref, v2_ref,
         kk_ref, ka_ref, rk_ref, lw_ref, lb_ref, ones_ref, out_ref,
         state_ref, carry_ref, y_ref) = refs
    else:
        (dp_ref, mu_ref, w0_ref, w2_ref, a0_ref, a2_ref,
         kk_ref, ka_ref, rk_ref, lw_ref, lb_ref, ones_ref, out_ref, vf_out,
         state_ref, carry_ref, y_ref) = refs

    @pl.when(pl.program_id(1) == 0)
    def _():
        state_ref[...] = jnp.zeros_like(state_ref)
        carry_ref[...] = jnp.zeros_like(carry_ref)

    hcur = dp_ref[...].astype(F32)
    row = lax.broadcasted_iota(jnp.int32, (tr, 1), 0)
    prev = jnp.where(row == 0, carry_ref[...], pltpu.roll(hcur, 1, 0))
    carry_ref[...] = hcur[tr - 1:tr, :]
    sh = hcur + (prev - hcur) * mu_ref[...]

    w = D_WIDTH
    r = sh[:, 0:w]
    k = sh[:, w:2 * w]
    v = sh[:, 2 * w:3 * w]
    wa = sh[:, 3 * w:3 * w + LANES]
    pre_w = w0_ref[...] + _dot(jnp.tanh(wa).astype(BF16), w2_ref[...])
    logd = -math.exp(-0.5) / (1.0 + jnp.exp(-pre_w))
    pre_a = a0_ref[...] + _dot(wa.astype(BF16), a2_ref[...])
    a = 1.0 / (1.0 + jnp.exp(-pre_a))
    if vres:
        vd = sh[:, 3 * w + LANES:3 * w + 2 * LANES]
        pre_v = v0_ref[...] + _dot(vd.astype(BF16), v2_ref[...])
        v = v + (vf_ref[...].astype(F32) - v) * (1.0 / (1.0 + jnp.exp(-pre_v)))
    else:
        vf_out[...] = v.astype(vf_out.dtype)
    def head_sum(t):
        t_hi = t.astype(BF16)
        t_lo = (t - t_hi.astype(F32)).astype(BF16)
        n = t.shape[1] // LANES
        stack = jnp.concatenate([piece[:, i * LANES:(i + 1) * LANES] for i in range(n) for piece in (t_hi, t_lo)], axis=0)
        res = _dot(stack, ones_ref[...])
        rows = t.shape[0]
        return jnp.concatenate([res[2 * i * rows:(2 * i + 1) * rows] + res[(2 * i + 1) * rows:(2 * i + 2) * rows]
                                for i in range(n)], axis=1)

    kk = k * kk_ref[...]
    kk = kk / jnp.maximum(jnp.sqrt(head_sum(kk * kk)), 1e-12)
    k2 = k * (1.0 + (a - 1.0) * ka_ref[...])

    ti = lax.broadcasted_iota(jnp.int32, (tr, tr), 0)
    tj = lax.broadcasted_iota(jnp.int32, (tr, tr), 1)
    tri = jnp.where((tj <= ti) & (tj >= ti // c * c), 1.0, 0.0).astype(BF16)
    hi, mid, lo = _split3(logd)
    cum = _dot(tri, hi) + _dot(tri, mid) + _dot(tri, lo)
    p_inc = jnp.exp(cum)
    p_exc = jnp.where(row % c == 0, 1.0, pltpu.roll(p_inc, 1, 0))
    p_inv = jnp.exp(-cum)
    p_all = [p_inc[(ci + 1) * c - 1:(ci + 1) * c, :] for ci in range(nchunk)]
    p_all_rows = jnp.concatenate([jnp.broadcast_to(pa, (c, D_WIDTH)) for pa in p_all], axis=0)
    kb = kk * a * p_inv
    kt = k2 * p_inv
    at_b = (-kk * p_exc).astype(BF16)
    rt_b = (r * p_inc).astype(BF16)
    bend_b = (kb * p_all_rows).astype(BF16)
    kend_b = (kt * p_all_rows).astype(BF16)
    vb = v.astype(BF16)

    pc = lax.broadcasted_iota(jnp.int32, (c, LANES), 1) % D_HEAD
    pr = lax.broadcasted_iota(jnp.int32, (c, LANES), 0)
    strict = pc < pr
    incl = pc <= pr
    eye = jnp.where(pc == pr, 1.0, 0.0)
    lo_lane = lax.broadcasted_iota(jnp.int32, (c, LANES), 1) < D_HEAD
    sr = lax.broadcasted_iota(jnp.int32, (LANES, LANES), 0) < D_HEAD
    sc = lax.broadcasted_iota(jnp.int32, (LANES, LANES), 1) < D_HEAD
    same_head = sr == sc

    def pair_diag(t):
        return jnp.concatenate([jnp.where(lo_lane, t, 0.0), jnp.where(lo_lane, 0.0, t)], axis=0).astype(BF16)

    items = [(ci, p) for ci in range(nchunk) for p in range(D_HEADS // 2)]
    rsl = lambda ci: slice(ci * c, (ci + 1) * c)
    psl = lambda p: slice(p * LANES, (p + 1) * LANES)
    sub = lambda arr, it: arr[rsl(it[0]), psl(it[1])]
    lhs = {it: jnp.concatenate([sub(at_b, it), sub(rt_b, it)], axis=0) for it in items}
    mats = {it: _dot_nt(lhs[it], jnp.concatenate([pair_diag(sub(kb, it)), pair_diag(sub(kt, it))], axis=0))
            for it in items}
    r_b = {it: jnp.where(incl, mats[it][c:, :LANES], 0.0).astype(BF16) for it in items}
    x0 = {it: jnp.where(strict, mats[it][:c, :LANES], 0.0) for it in items}
    xp = {it: _dot(x0[it].astype(BF16), pair_diag(x0[it])) for it in items}
    tinv = {it: eye + x0[it] for it in items}
    n_levels = int(math.log2(c))
    for lvl in range(1, n_levels - 1):
        prod = {it: _dot(jnp.concatenate([xp[it], tinv[it]], axis=0).astype(BF16), pair_diag(xp[it])) for it in items}
        xp = {it: prod[it][:c] for it in items}
        tinv = {it: tinv[it] + prod[it][c:] for it in items}
    tinv = {it: (tinv[it] + _dot(tinv[it].astype(BF16), pair_diag(xp[it]))).astype(BF16) for it in items}
    m_k = {it: jnp.concatenate([jnp.where(strict, mats[it][:c, LANES:], 0.0),
                                jnp.where(incl, mats[it][c:, LANES:], 0.0)], axis=0).astype(BF16) for it in items}
    from_v = {it: _dot(m_k[it], pair_diag(sub(v, it))) for it in items}
    upd_v = {it: jnp.where(same_head, _dot_tn(sub(vb, it), sub(kend_b, it)), 0.0) for it in items}

    state = [state_ref[p] for p in range(D_HEADS // 2)]
    pairs = range(D_HEADS // 2)
    for ci in range(nchunk):
        from_state = [_dot_nt(lhs[ci, p], state[p].astype(BF16)) for p in pairs]
        u = [_dot(tinv[ci, p], pair_diag(from_state[p][:c] + from_v[ci, p][:c])) for p in pairs]
        upd_u = [_dot_tn(u[p].astype(BF16), sub(bend_b, (ci, p))) for p in pairs]
        state = [state[p] * p_all[ci][:, psl(p)] + jnp.where(same_head, upd_u[p], 0.0) + upd_v[ci, p] for p in pairs]
        for p in pairs:
            y_ref[rsl(ci), psl(p)] = from_state[p][c:] + from_v[ci, p][c:] + _dot(r_b[ci, p], pair_diag(u[p]))
    for p in pairs:
        state_ref[p] = state[p]

    y = y_ref[...]
    yc = y - head_sum(y) * (1.0 / D_HEAD)
    var = head_sum(yc * yc) * (1.0 / D_HEAD)
    yn = yc * lax.rsqrt(var + RWKV_LN_EPS) * lw_ref[...] + lb_ref[...]
    bonus = head_sum(r * k2 * rk_ref[...]) * v
    out_ref[...] = (yn + bonus).astype(out_ref.dtype)


def _rwkv(dproj, v_first, p, b, s):
    tr = D_STEP_CHUNKS * D_CHUNK
    vres = v_first is not None
    row = lambda wd: pl.BlockSpec((None, tr, wd), lambda bi, i: (bi, i, 0))
    full = lambda a: pl.BlockSpec(a.shape, lambda bi, i: (0, 0))
    ins = [dproj.reshape(b, s, D_PROJ_PAD)]
    in_specs = [row(D_PROJ_PAD)]
    if vres:
        ins.append(v_first.reshape(b, s, D_WIDTH))
        in_specs.append(row(D_WIDTH))
    names = ['mu', 'w0', 'w2', 'a0', 'a2'] + (['v0', 'v2'] if vres else []) + ['k_k', 'k_a', 'r_k', 'lnx_w', 'lnx_b']
    for nm in names:
        ins.append(p[nm])
        in_specs.append(full(p[nm]))
    head_id = np.arange(LANES) // D_HEAD
    ones_bd = jnp.asarray(head_id[:, None] == head_id[None, :], BF16)
    ins.append(ones_bd)
    in_specs.append(full(ones_bd))
    out_shape = [jax.ShapeDtypeStruct((b, s, D_WIDTH), BF16)]
    out_specs = [row(D_WIDTH)]
    if not vres:
        out_shape.append(jax.ShapeDtypeStruct((b, s, D_WIDTH), BF16))
        out_specs.append(row(D_WIDTH))
    res = pl.pallas_call(
        functools.partial(_rwkv_kernel, vres=vres, nchunk=D_STEP_CHUNKS), grid=(b, s // tr),
        in_specs=in_specs, out_specs=out_specs, out_shape=out_shape,
        scratch_shapes=[pltpu.VMEM((D_HEADS // 2, LANES, LANES), F32),
                        pltpu.VMEM((1, D_PROJ_PAD), F32),
                        pltpu.VMEM((tr, D_WIDTH), F32)],
        compiler_params=_cparams("parallel", "arbitrary"), name="rwkv7",
    )(*ins)
    d_out = res[0].reshape(b * s, D_WIDTH)
    vf = v_first if vres else res[1].reshape(b * s, D_WIDTH)
    return d_out, vf


def _rope_tables(positions, dim, theta, period, base, half):
    inv = jnp.exp(-math.log(theta) * jnp.arange(0, dim, 2, dtype=F32) / dim)
    ang = positions.astype(F32)[..., None] * inv
    n = positions.shape[0] * positions.shape[1]
    cos, sin = jnp.cos(ang).reshape(n, half), jnp.sin(ang).reshape(n, half)
    lane = np.arange(LANES) % period - base
    lo = (lane >= 0) & (lane < half)
    hi = (lane >= half) & (lane < 2 * half)
    src = np.where(lo, lane, np.where(hi, lane - half, -1))
    hit = np.arange(half)[:, None] == src[None, :]
    spread = lambda t, m: jnp.dot(t, jnp.asarray(m, F32), precision=lax.Precision.HIGHEST)
    c = spread(cos, hit.astype(np.float32)) + jnp.asarray(~(lo | hi), F32)
    s1 = spread(sin, (hit & hi).astype(np.float32))
    s2 = spread(sin, -(hit & lo).astype(np.float32))
    return c, s1, s2


A_SUB_DILATIONS = tuple(d for d in A_DILATIONS if d > 1)


def _even_proj(norm, w_in, tabs_a, tabs_r):
    q_scale = HEAD_DIM ** -0.5 * LOG2E
    aw = A_WIDTH
    plan = [(0, aw, 0, 0, (0, ROT_DIM // 2, q_scale), 0), (aw, aw, 0, aw, (0, ROT_DIM // 2, 1.0), aw),
            (2 * aw, aw, 0, 2 * aw, None, 2 * aw),
            (1536, 512, 0, 1536, (1, B_QK_DIM // 2, 1.0)), (2048, 512, 0, 2048, None),
            (2560, 256, 1, 0, (None, 0, MEM_Q_SCALE)), (2816, 640, 2, 0, None), (3456, 640, 2, 640, None)]
    return _proj_spec(norm, w_in.astype(BF16), plan, (2560, 256, MIX_WIDTH), (tabs_a, tabs_r),
                      dils=A_SUB_DILATIONS, sub_width=3 * aw)


def _even_mixers(res, b, s, mem_kv):
    main, qm, gate = res[:3]
    main3 = main.reshape(b, s, 2560)
    qkv = {1: main3.reshape(b, 1, s, 2560)}
    qkv.update(zip(A_SUB_DILATIONS, res[3:]))
    outs, lses = [], []
    for dil in A_DILATIONS:
        o, lse = _band_attn(qkv[dil], 0, dil)
        outs.append(o)
        lses.append(lse)
    r_out = _retention(main3)
    m_out = _mem_attn(qm, mem_kv, b, s)
    return (outs, lses), r_out, m_out, gate


def _odd_proj(norm, w_in, d_cols):
    k_dim = w_in.shape[0]
    w_in = w_in.astype(BF16)
    z = lambda n: jnp.zeros((k_dim, n), BF16)
    o = 0
    cq = w_in[:, o:o + C_Q_RANK]; o += C_Q_RANK
    ckv = w_in[:, o:o + C_KV_RANK]; o += C_KV_RANK
    kro = w_in[:, o:o + C_ROPE]; o += C_ROPE
    dp = w_in[:, o:o + d_cols]; o += d_cols
    qm_w = w_in[:, o:o + MEM_WIDTH]; o += MEM_WIDTH
    gate_w = w_in[:, o:]
    dp_pad = jnp.concatenate([dp, z(D_PROJ_PAD - d_cols)], axis=1)
    swap = lambda t: jnp.concatenate([t[..., C_ROPE // 2:], t[..., :C_ROPE // 2]], axis=-1)
    rope_slab = lambda t: jnp.concatenate([z(C_NOPE), t, z(LANES - C_NOPE - C_ROPE)], axis=1)
    w_pad = jnp.concatenate([cq, ckv, rope_slab(kro), rope_slab(swap(kro)), dp_pad, qm_w, gate_w], axis=1)
    m0 = C_Q_RANK + C_KV_RANK + 2 * LANES
    plan = [(0, m0, 0, 0, None), (m0, 896, 1, 0, None), (m0 + 896, 896, 1, 896, None),
            (m0 + 1792, 256, 2, 0, (None, 0, MEM_Q_SCALE)),
            (m0 + 2048, 640, 3, 0, None), (m0 + 2688, 640, 3, 640, None)]
    return _proj_spec(norm, w_pad.astype(BF16), plan, (m0, D_PROJ_PAD, 256, MIX_WIDTH))


def _odd_mixers(res, b, s, mem_kv, tabs_m, v_first, q_norm, w_qb, kv_norm, w_kvb,
                mu_shift, w0, w2, a0, a2, v0, v2, k_k, k_a, r_k, lnx_w, lnx_b):
    mla_in, dproj, qm, gate = res
    vres = v0 is not None
    d_cols = mu_shift.shape[0]
    swap = lambda t: jnp.concatenate([t[..., C_ROPE // 2:], t[..., :C_ROPE // 2]], axis=-1)
    wq = w_qb.reshape(C_Q_RANK, C_HEADS, C_NOPE + C_ROPE)
    slab_pad = ((0, 0), (0, 0), (0, LANES - C_NOPE - C_ROPE))
    wq_sw = jnp.concatenate([wq[..., :C_NOPE], swap(wq[..., C_NOPE:])], axis=-1)
    wq_sw = jnp.pad(wq_sw, slab_pad).reshape(C_Q_RANK, C_HEADS * LANES)
    wq = jnp.pad(wq, slab_pad).reshape(C_Q_RANK, C_HEADS * LANES)
    wkv = w_kvb.reshape(C_KV_RANK, C_HEADS, C_NOPE + C_V)
    wk = jnp.pad(wkv[:, :, :C_NOPE], ((0, 0), (0, 0), (0, LANES - C_NOPE))).reshape(C_KV_RANK, C_HEADS * LANES)
    wv = jnp.pad(wkv[:, :, C_NOPE:], ((0, 0), (0, 0), (0, LANES - C_V))).reshape(C_KV_RANK, C_HEADS * LANES)
    q, k, v = _mla_prep(mla_in, q_norm, kv_norm, wq.astype(BF16), wq_sw.astype(BF16), wk.astype(BF16),
                        wv.astype(BF16), tabs_m)
    c_out = _mla_attn(q, k, v, b, s)

    row = lambda t: t.reshape(1, -1).astype(F32)
    pad_rows = lambda t, top, n: jnp.concatenate(
        [jnp.zeros((top, t.shape[1]), F32), t, jnp.zeros((n - top - t.shape[0], t.shape[1]), F32)], axis=0)
    p = dict(mu=row(jnp.concatenate([mu_shift, jnp.zeros((D_PROJ_PAD - d_cols,), F32)])),
             w0=row(w0), w2=pad_rows(w2, 0, LANES).astype(BF16),
             a0=row(a0), a2=pad_rows(a2, D_DECAY_LORA, LANES).astype(BF16),
             k_k=row(k_k), k_a=row(k_a), r_k=row(r_k), lnx_w=row(lnx_w), lnx_b=row(lnx_b))
    if vres:
        p['v0'] = row(v0)
        p['v2'] = pad_rows(v2, 0, LANES).astype(BF16)
    d_out, v_first = _rwkv(dproj, v_first, p, b, s)
    m_out = _mem_attn(qm, mem_kv, b, s)
    return (c_out, d_out, m_out, gate), v_first


def kernel(x, mem, positions, mem_norm, final_norm, l0_norm, l0_w_in, l0_w_mem_kv, l0_w_out, l1_norm, l1_w_in, l1_q_norm, l1_w_qb, l1_kv_norm, l1_w_kvb, l1_mu_shift, l1_w0, l1_w2, l1_a0, l1_a2, l1_k_k, l1_k_a, l1_r_k, l1_lnx_w, l1_lnx_b, l1_w_mem_kv, l1_w_out, l2_norm, l2_w_in, l2_w_mem_kv, l2_w_out, l3_norm, l3_w_in, l3_q_norm, l3_w_qb, l3_kv_norm, l3_w_kvb, l3_mu_shift, l3_w0, l3_w2, l3_a0, l3_a2, l3_v0, l3_v2, l3_k_k, l3_k_a, l3_r_k, l3_lnx_w, l3_lnx_b, l3_w_mem_kv, l3_w_out):
    b, s, d = x.shape
    x2d = x.reshape(b * s, d)
    mem2d = mem.reshape(b * MEM_TOKENS, d)
    tabs_a = _rope_tables(positions, ROT_DIM, ROPE_THETA, HEAD_DIM, 0, ROT_DIM // 2)
    tabs_r = _rope_tables(positions, B_QK_DIM, RET_THETA, B_QK_DIM, 0, B_QK_DIM // 2)
    tabs_m = _rope_tables(positions, C_ROPE, ROPE_THETA, LANES, C_NOPE, C_ROPE // 2)

    def mem_kv(w):
        plan = [(0, 2 * MEM_WIDTH, 0, 0, None)]
        return _norm_proj(mem2d, mem_norm, w.astype(BF16), plan, (2 * MEM_WIDTH,), tm=MEM_TOKENS)[0]

    proj = [_even_proj(l0_norm, l0_w_in, tabs_a, tabs_r), _odd_proj(l1_norm, l1_w_in, l1_mu_shift.shape[0]),
            _even_proj(l2_norm, l2_w_in, tabs_a, tabs_r), _odd_proj(l3_norm, l3_w_in, l3_mu_shift.shape[0])]
    w_outs = [w.astype(BF16) for w in (l0_w_out, l1_w_out, l2_w_out, l3_w_out)]
    p0 = proj[0]
    res = _norm_proj(x2d, p0['g'], p0['w'], p0['plan'], p0['out_widths'], p0['tables'], dils=p0['dils'],
                     sub_width=p0['sub_width'], seq=s)
    ops = _even_mixers(res, b, s, mem_kv(l0_w_mem_kv))
    x2d, *res = _out_proj(*ops, x2d, w_outs[0], seq=s, next_proj=proj[1])
    ops, v_first = _odd_mixers(res, b, s, mem_kv(l1_w_mem_kv), tabs_m, None, l1_q_norm, l1_w_qb, l1_kv_norm, l1_w_kvb,
                               l1_mu_shift, l1_w0, l1_w2, l1_a0, l1_a2, None, None,
                               l1_k_k, l1_k_a, l1_r_k, l1_lnx_w, l1_lnx_b)
    x2d, *res = _out_proj(*ops, x2d, w_outs[1], seq=s, next_proj=proj[2])
    ops = _even_mixers(res, b, s, mem_kv(l2_w_mem_kv))
    x2d, *res = _out_proj(*ops, x2d, w_outs[2], seq=s, next_proj=proj[3])
    ops, _ = _odd_mixers(res, b, s, mem_kv(l3_w_mem_kv), tabs_m, v_first, l3_q_norm, l3_w_qb, l3_kv_norm, l3_w_kvb,
                         l3_mu_shift, l3_w0, l3_w2, l3_a0, l3_a2, l3_v0, l3_v2,
                         l3_k_k, l3_k_a, l3_r_k, l3_lnx_w, l3_lnx_b)
    out = _out_proj(*ops, x2d, w_outs[3], final_gain=final_norm)
    return out.reshape(b, s, d)
```

```python
import functools
import math

import numpy as np
import jax
import jax.numpy as jnp
from jax import lax
from jax.experimental import pallas as pl
from jax.experimental.pallas import tpu as pltpu

F32 = jnp.float32
BF16 = jnp.bfloat16

LANES = 128
VMEM_LIMIT = 56 * 1024 * 1024
ROW_TILE = 512

D_MODEL = 1024
HEAD_DIM = 64
ROPE_THETA = 500000.0
ROT_DIM = HEAD_DIM // 4
NORM_EPS = 1e-6
NEG_INF = -1e30
LN2 = math.log(2.0)
LOG2E = 1.0 / LN2

A_HEADS = 8
A_WIDTH = A_HEADS * HEAD_DIM
A_PATTERNS = ((128, 1), (512, 4), (2048, 16))
A_BLOCK = 128
A_DILATIONS = tuple(d for (_, d) in A_PATTERNS)
A_LSE_GROUP = HEAD_DIM // (A_HEADS // 2)

B_HEADS = 4
B_QK_DIM = 64
B_V_DIM = 128
B_CHUNK = 128
RET_THETA = 10000.0

MEM_TOKENS = 256
MEM_HEADS = 4
MEM_WIDTH = MEM_HEADS * HEAD_DIM
MEM_Q_SCALE = HEAD_DIM ** -0.5 / math.log(2.0)

C_HEADS = 8
C_Q_RANK = 256
C_KV_RANK = 128
C_NOPE = 64
C_ROPE = 32
C_V = 64

D_HEADS = 8
D_HEAD = 64
D_WIDTH = D_HEADS * D_HEAD
D_DECAY_LORA = 64
D_AAA_LORA = 64
D_MV_LORA = 32
RWKV_LN_EPS = 64e-5
D_CHUNK = 64
D_STEP_CHUNKS = 4
D_PROJ_PAD = 3 * D_WIDTH + 2 * LANES

MIX_WIDTH = 1280


def _cparams(*sem):
    return pltpu.CompilerParams(dimension_semantics=sem, vmem_limit_bytes=VMEM_LIMIT)


def _dot(a, b):
    return jnp.dot(a, b, preferred_element_type=F32)


def _dot_nt(a, b):
    return lax.dot_general(a, b, (((1,), (1,)), ((), ())), preferred_element_type=F32)


def _dot_tn(a, b):
    return lax.dot_general(a, b, (((0,), (0,)), ((), ())), preferred_element_type=F32)


def _rot_slab(x, c, s1, s2, half):
    return x * c + pltpu.roll(x, half, 1) * s1 + pltpu.roll(x, LANES - half, 1) * s2


def _rotary(x, c, s1, s2, half):
    slabs = [_rot_slab(x[:, i:i + LANES], c, s1, s2, half) for i in range(0, x.shape[1], LANES)]
    return slabs[0] if len(slabs) == 1 else jnp.concatenate(slabs, axis=1)


def _norm_proj_kernel(*refs, plan, n_tab, n_out, dils):
    x_ref, g_ref, w_ref = refs[:3]
    tabs = refs[3:3 + 3 * n_tab]
    outs = refs[3 + 3 * n_tab:3 + 3 * n_tab + n_out]
    sub_outs = refs[3 + 3 * n_tab + n_out:3 + 3 * n_tab + n_out + len(dils)]
    stage_ref = refs[-1] if dils else None
    _norm_proj_body(x_ref[...].astype(F32), g_ref, w_ref, tabs, outs, sub_outs, stage_ref, plan, dils)


def _norm_proj_body(x, g_ref, w_ref, tabs, outs, sub_outs, stage_ref, plan, dils):
    tm = x.shape[0]
    ms = jnp.mean(x * x, axis=-1, keepdims=True)
    xn = (x * lax.rsqrt(ms + NORM_EPS) * g_ref[...]).astype(BF16)
    n_staged = [0]

    def epilogue(entry, acc):
        (_, width, oi, oc, rot, sub_col) = entry
        if rot is not None:
            t, half, post_scale = rot
            if t is not None:
                acc = _rotary(acc, tabs[3 * t][...], tabs[3 * t + 1][...], tabs[3 * t + 2][...], half)
            if post_scale != 1.0:
                acc = acc * post_scale
        outs[oi][:, oc:oc + width] = acc.astype(outs[oi].dtype)
        if sub_col is not None:
            slot = n_staged[0]
            n_staged[0] += 1
            for c in range(width // LANES):
                stage_ref[slot, c] = acc[:, c * LANES:(c + 1) * LANES]
            for d, sub in zip(dils, sub_outs):
                for r in range(d):
                    for c in range(width // LANES):
                        col = sub_col + c * LANES
                        sub[r, :, col:col + LANES] = stage_ref[slot, c, pl.ds(r, tm // d, stride=d), :].astype(sub.dtype)

    pending = None
    for entry in plan:
        acc = _dot(xn, w_ref[:, entry[0]:entry[0] + entry[1]])
        if pending is not None:
            epilogue(*pending)
        pending = (entry, acc)
    epilogue(*pending)


def _norm_proj(x2d, g, w, plan, out_widths, tables=(), tm=ROW_TILE, dils=(), sub_width=0, seq=None):
    t_rows, k = x2d.shape
    assert t_rows % tm == 0
    proj = _proj_spec(g, w, plan, out_widths, tables, dils, sub_width)
    ins, in_specs, out_specs, out_shape, scratch = _proj_operands(proj, t_rows, tm, seq)
    kern = functools.partial(_norm_proj_kernel, plan=proj['plan'], n_tab=len(tables), n_out=len(out_widths),
                             dils=tuple(dils))
    return pl.pallas_call(
        kern, grid=(t_rows // tm,), in_specs=[pl.BlockSpec((tm, k), lambda i: (i, 0))] + in_specs,
        out_specs=out_specs, out_shape=out_shape,
        scratch_shapes=scratch, compiler_params=_cparams("parallel"), name="norm_proj",
    )(x2d, *ins)


def _proj_spec(g, w, plan, out_widths, tables=(), dils=(), sub_width=0):
    plan = tuple(tuple(p) + (None,) * (6 - len(p)) for p in plan)
    return dict(g=g, w=w, plan=plan, out_widths=tuple(out_widths), tables=tuple(tables), dils=tuple(dils),
                sub_width=sub_width)


def _proj_operands(proj, t_rows, tm, seq):
    k, n = proj['w'].shape
    flat_tabs = [t for tab in proj['tables'] for t in tab]
    ins = [proj['g'].reshape(1, k).astype(F32), proj['w']] + flat_tabs
    in_specs = [pl.BlockSpec((1, k), lambda i: (0, 0)),
                pl.BlockSpec((k, n), lambda i: (0, 0), pipeline_mode=pl.Buffered(1))]
    in_specs += [pl.BlockSpec((tm, LANES), lambda i: (i, 0)) for _ in flat_tabs]
    out_specs = [pl.BlockSpec((tm, ow), lambda i: (i, 0)) for ow in proj['out_widths']]
    out_shape = [jax.ShapeDtypeStruct((t_rows, ow), BF16) for ow in proj['out_widths']]
    scratch = []
    if proj['dils']:
        nt = seq // tm
        sub_width = proj['sub_width']
        for d in proj['dils']:
            out_specs.append(pl.BlockSpec((None, d, tm // d, sub_width), lambda i: (i // nt, 0, i % nt, 0)))
            out_shape.append(jax.ShapeDtypeStruct((t_rows // seq, d, seq // d, sub_width), BF16))
        staged = [p[1] for p in proj['plan'] if p[5] is not None]
        scratch = [pltpu.VMEM((len(staged), max(staged) // LANES, tm, LANES), F32)]
    return ins, in_specs, out_specs, out_shape, scratch


def _band_attn_kernel(q_ref, kp_ref, kc_ref, vp_ref, vc_ref, o_ref, lse_ref, s_ref, p_ref, m_ref, *, nq):
    i = pl.program_id(2)
    qi = lax.broadcasted_iota(jnp.int32, (A_BLOCK, A_BLOCK), 0)
    kj = lax.broadcasted_iota(jnp.int32, (A_BLOCK, A_BLOCK), 1)
    bias_prev = jnp.where(kj >= qi, 0.0, NEG_INF)
    bias_cur = jnp.where(kj <= qi, 0.0, NEG_INF)
    bias_first = bias_prev + jnp.where(i > 0, 0.0, NEG_INF)
    lane = lax.broadcasted_iota(jnp.int32, (A_BLOCK, LANES), 1)
    lo = lane < HEAD_DIM
    keep = (jnp.where(lo, 1.0, 0.0).astype(BF16), jnp.where(lo, 0.0, 1.0).astype(BF16))
    npair = A_HEADS // 2

    def operands(u, pair):
        sl = slice(pair * LANES, (pair + 1) * LANES)
        rows = slice(u * A_BLOCK, (u + 1) * A_BLOCK)
        if u == 0:
            return rows, sl, kp_ref[:, sl], vp_ref[:, sl], bias_first
        prow = slice((u - 1) * A_BLOCK, u * A_BLOCK)
        return rows, sl, kc_ref[prow, sl], vc_ref[prow, sl], bias_prev

    for u in range(nq):
        for pair in range(npair):
            rows, sl, kp, _, bp = operands(u, pair)
            q = q_ref[rows, sl]
            kc = kc_ref[rows, sl]
            for half in range(2):
                idx = (u * npair + pair) * 2 + half
                qh = q * keep[half]
                s_ref[idx, :, :A_BLOCK] = _dot_nt(qh, kp) + bp
                s_ref[idx, :, A_BLOCK:] = _dot_nt(qh, kc) + bias_cur
    for idx in range(nq * A_HEADS):
        s = s_ref[idx]
        m = jnp.max(s, -1, keepdims=True)
        p_ref[idx] = jnp.exp2(s - m).astype(BF16)
        m_ref[idx] = jnp.broadcast_to(m, (A_BLOCK, LANES))
    for u in range(nq):
        lse = None
        for pair in range(npair):
            rows, sl, _, vp, _ = operands(u, pair)
            vc = vc_ref[rows, sl]
            idx0 = (u * npair + pair) * 2
            o = [_dot(p_ref[idx0 + half, :, :A_BLOCK], vp * keep[half] + keep[1 - half])
                 + _dot(p_ref[idx0 + half, :, A_BLOCK:], vc * keep[half] + keep[1 - half]) for half in range(2)]
            den = pltpu.roll(jnp.where(lo, o[1], o[0]), HEAD_DIM, 1)
            o_ref[rows, sl] = (jnp.where(lo, o[0], o[1]) / den).astype(o_ref.dtype)
            lse_pair = jnp.where(lo, m_ref[idx0], m_ref[idx0 + 1]) * LN2 + jnp.log(den)
            lse = lse_pair if lse is None else jnp.where(lane % HEAD_DIM // A_LSE_GROUP == pair, lse_pair, lse)
        lse_ref[rows, :] = lse


def _single_block_attn_kernel(q_ref, k_ref, v_ref, o_ref, lse_ref, s_ref, p_ref, m_ref, *, nres):
    qi = lax.broadcasted_iota(jnp.int32, (A_BLOCK, A_BLOCK), 0)
    kj = lax.broadcasted_iota(jnp.int32, (A_BLOCK, A_BLOCK), 1)
    bias = jnp.where(kj <= qi, 0.0, NEG_INF)
    lane = lax.broadcasted_iota(jnp.int32, (A_BLOCK, LANES), 1)
    lo = lane < HEAD_DIM
    keep = (jnp.where(lo, 1.0, 0.0).astype(BF16), jnp.where(lo, 0.0, 1.0).astype(BF16))
    npair = A_HEADS // 2
    for r in range(nres):
        for pair in range(npair):
            sl = slice(pair * LANES, (pair + 1) * LANES)
            for half in range(2):
                s_ref[(r * npair + pair) * 2 + half] = _dot_nt(q_ref[r, :, sl] * keep[half], k_ref[r, :, sl]) + bias
    for idx in range(nres * A_HEADS):
        s = s_ref[idx]
        m = jnp.max(s, -1, keepdims=True)
        p_ref[idx] = jnp.exp2(s - m).astype(BF16)
        m_ref[idx] = jnp.broadcast_to(m, (A_BLOCK, LANES))
    for r in range(nres):
        lse = None
        for pair in range(npair):
            sl = slice(pair * LANES, (pair + 1) * LANES)
            idx0 = (r * npair + pair) * 2
            v = v_ref[r, :, sl]
            o = [_dot(p_ref[idx0 + half], v * keep[half] + keep[1 - half]) for half in range(2)]
            den = pltpu.roll(jnp.where(lo, o[1], o[0]), HEAD_DIM, 1)
            o_ref[r, :, sl] = (jnp.where(lo, o[0], o[1]) / den).astype(o_ref.dtype)
            lse_pair = jnp.where(lo, m_ref[idx0], m_ref[idx0 + 1]) * LN2 + jnp.log(den)
            lse = lse_pair if lse is None else jnp.where(lane % HEAD_DIM // A_LSE_GROUP == pair, lse_pair, lse)
        lse_ref[r] = lse


def _single_block_attn(qkv, col0, dilation, nres=4):
    b, d, sub_len, _ = qkv.shape
    assert sub_len == A_BLOCK and d % nres == 0
    spec = lambda col, w: pl.BlockSpec((None, nres, A_BLOCK, w), lambda bi, r: (bi, r, 0, col))
    return pl.pallas_call(
        functools.partial(_single_block_attn_kernel, nres=nres), grid=(b, d // nres),
        in_specs=[spec(col0, A_WIDTH), spec(col0 + 1, A_WIDTH), spec(col0 + 2, A_WIDTH)],
        out_specs=[spec(0, A_WIDTH), spec(0, LANES)],
        out_shape=[jax.ShapeDtypeStruct((b, d, sub_len, A_WIDTH), BF16),
                   jax.ShapeDtypeStruct((b, d, sub_len, LANES), F32)],
        scratch_shapes=[pltpu.VMEM((nres * A_HEADS, A_BLOCK, A_BLOCK), F32),
                        pltpu.VMEM((nres * A_HEADS, A_BLOCK, A_BLOCK), BF16),
                        pltpu.VMEM((nres * A_HEADS, A_BLOCK, LANES), F32)],
        compiler_params=_cparams("parallel", "parallel"), name="band_attn_d%d" % dilation,
    )(qkv, qkv, qkv)


def _band_attn(qkv, col0, dilation):
    b, d, sub_len, _ = qkv.shape
    assert d == dilation
    if sub_len == A_BLOCK:
        return _single_block_attn(qkv, col0, dilation)
    nq = 4 if sub_len % (4 * A_BLOCK) == 0 else 1
    tq = nq * A_BLOCK

    def cur(col):
        return pl.BlockSpec((None, None, tq, A_WIDTH), lambda bi, r, i: (bi, r, i, col0 + col))

    def prev(col):
        return pl.BlockSpec((None, None, A_BLOCK, A_WIDTH),
                            lambda bi, r, i: (bi, r, jnp.maximum(i * nq - 1, 0), col0 + col))

    return pl.pallas_call(
        functools.partial(_band_attn_kernel, nq=nq), grid=(b, d, sub_len // tq),
        in_specs=[cur(0), prev(1), cur(1), prev(2), cur(2)],
        out_specs=[pl.BlockSpec((None, None, tq, A_WIDTH), lambda bi, r, i: (bi, r, i, 0)),
                   pl.BlockSpec((None, None, tq, LANES), lambda bi, r, i: (bi, r, i, 0))],
        out_shape=[jax.ShapeDtypeStruct((b, d, sub_len, A_WIDTH), BF16),
                   jax.ShapeDtypeStruct((b, d, sub_len, LANES), F32)],
        scratch_shapes=[pltpu.VMEM((nq * A_HEADS, A_BLOCK, 2 * A_BLOCK), F32),
                        pltpu.VMEM((nq * A_HEADS, A_BLOCK, 2 * A_BLOCK), BF16),
                        pltpu.VMEM((nq * A_HEADS, A_BLOCK, LANES), F32)],
        compiler_params=_cparams("parallel", "parallel", "parallel"), name="band_attn_d%d" % dilation,
    )(qkv, qkv, qkv, qkv, qkv)


def _retention_kernel(qk_ref, v_ref, o_ref, state_ref, *, nchunk):
    c = B_CHUNK
    @pl.when(pl.program_id(1) == 0)
    def _():
        state_ref[...] = jnp.zeros_like(state_ref)

    scale = B_QK_DIM ** -0.5
    qi = lax.broadcasted_iota(jnp.int32, (c, c), 0)
    kj = lax.broadcasted_iota(jnp.int32, (c, c), 1)
    diff = (qi - kj).astype(F32)
    row = lax.broadcasted_iota(jnp.int32, (c, 1), 0).astype(F32)
    lo = lax.broadcasted_iota(jnp.int32, (c, LANES), 1) < B_QK_DIM
    keep = (jnp.where(lo, 1.0, 0.0).astype(BF16), jnp.where(lo, 0.0, 1.0).astype(BF16))
    log_g = [math.log1p(-2.0 ** (-5.0 - h)) for h in range(B_HEADS)]
    d_in = [jnp.where(diff >= 0, jnp.exp(jnp.maximum(diff, 0.0) * lg), 0.0) * scale for lg in log_g]
    from_start = [jnp.exp((row + 1.0) * lg) for lg in log_g]
    to_end = [jnp.exp((c - 1.0 - row) * lg) * scale for lg in log_g]
    items = [(ci, h) for ci in range(nchunk) for h in range(B_HEADS)]
    rows = lambda ci: slice(ci * c, (ci + 1) * c)
    qsl = lambda h: slice(h // 2 * LANES, (h // 2 + 1) * LANES)
    ksl = lambda h: slice(B_HEADS * B_QK_DIM + h // 2 * LANES, B_HEADS * B_QK_DIM + (h // 2 + 1) * LANES)
    vsl = lambda h: slice(h * B_V_DIM, (h + 1) * B_V_DIM)
    qm = {(ci, h): qk_ref[rows(ci), qsl(h)] * keep[h % 2] for ci, h in items}
    scores = {(ci, h): (_dot_nt(qm[ci, h], qk_ref[rows(ci), ksl(h)]) * d_in[h]).astype(BF16) for ci, h in items}
    o_in = {(ci, h): _dot(scores[ci, h], v_ref[rows(ci), vsl(h)]) for ci, h in items}
    kd = {(ci, h): ((qk_ref[rows(ci), ksl(h)] * keep[h % 2]).astype(F32) * to_end[h]).astype(BF16) for ci, h in items}
    kv = {(ci, h): _dot_tn(kd[ci, h], v_ref[rows(ci), vsl(h)]) for ci, h in items}
    state = [state_ref[h] for h in range(B_HEADS)]
    for ci in range(nchunk):
        o = [o_in[ci, h] + _dot(qm[ci, h], state[h].astype(BF16)) * from_start[h] for h in range(B_HEADS)]
        state = [state[h] * math.exp(c * log_g[h]) + kv[ci, h] for h in range(B_HEADS)]
        for h in range(B_HEADS):
            on = o[h] * lax.rsqrt(jnp.mean(o[h] * o[h], -1, keepdims=True) + NORM_EPS)
            o_ref[rows(ci), vsl(h)] = on.astype(o_ref.dtype)
    for h in range(B_HEADS):
        state_ref[h] = state[h]


def _retention(main, nchunk=4):
    b, s, _ = main.shape
    blk = (None, nchunk * B_CHUNK, 512)
    o = pl.pallas_call(
        functools.partial(_retention_kernel, nchunk=nchunk), grid=(b, s // (nchunk * B_CHUNK)),
        in_specs=[pl.BlockSpec(blk, lambda bi, i: (bi, i, 3)), pl.BlockSpec(blk, lambda bi, i: (bi, i, 4))],
        out_specs=pl.BlockSpec(blk, lambda bi, i: (bi, i, 0)),
        out_shape=jax.ShapeDtypeStruct((b, s, B_HEADS * B_V_DIM), BF16),
        scratch_shapes=[pltpu.VMEM((B_HEADS, LANES, B_V_DIM), F32)],
        compiler_params=_cparams("parallel", "arbitrary"), name="retention",
    )(main, main)
    return o.reshape(b * s, B_HEADS * B_V_DIM)


def _mem_attn_kernel(q_ref, kv_ref, o_ref, s_ref, p_ref):
    tq = q_ref.shape[0]
    lo_q = lax.broadcasted_iota(jnp.int32, (tq, LANES), 1) < HEAD_DIM
    lo_k = lax.broadcasted_iota(jnp.int32, (MEM_TOKENS, LANES), 1) < HEAD_DIM
    keep_q = (jnp.where(lo_q, 1.0, 0.0).astype(BF16), jnp.where(lo_q, 0.0, 1.0).astype(BF16))
    keep_k = (jnp.where(lo_k, 1.0, 0.0).astype(BF16), jnp.where(lo_k, 0.0, 1.0).astype(BF16))
    npair = MEM_HEADS // 2
    for pair in range(npair):
        sl = slice(pair * LANES, (pair + 1) * LANES)
        for half in range(2):
            s_ref[2 * pair + half] = _dot_nt(q_ref[:, sl] * keep_q[half], kv_ref[:, sl])
    for idx in range(MEM_HEADS):
        s = s_ref[idx]
        p_ref[idx] = jnp.exp2(s - jnp.max(s, -1, keepdims=True)).astype(BF16)
    for pair in range(npair):
        sl = slice(pair * LANES, (pair + 1) * LANES)
        v = kv_ref[:, MEM_WIDTH + pair * LANES:MEM_WIDTH + (pair + 1) * LANES]
        o = [_dot(p_ref[2 * pair + half], v * keep_k[half] + keep_k[1 - half]) for half in range(2)]
        den = pltpu.roll(jnp.where(lo_q, o[1], o[0]), HEAD_DIM, 1)
        o_ref[:, sl] = (jnp.where(lo_q, o[0], o[1]) / den).astype(o_ref.dtype)


def _mem_attn(qm, kv, b, s, tq=512):
    o = pl.pallas_call(
        _mem_attn_kernel, grid=(b, s // tq),
        in_specs=[pl.BlockSpec((None, tq, MEM_WIDTH), lambda bi, i: (bi, i, 0)),
                  pl.BlockSpec((None, MEM_TOKENS, 2 * MEM_WIDTH), lambda bi, i: (bi, 0, 0))],
        out_specs=pl.BlockSpec((None, tq, MEM_WIDTH), lambda bi, i: (bi, i, 0)),
        out_shape=jax.ShapeDtypeStruct((b, s, MEM_WIDTH), BF16),
        scratch_shapes=[pltpu.VMEM((MEM_HEADS, tq, MEM_TOKENS), F32),
                        pltpu.VMEM((MEM_HEADS, tq, MEM_TOKENS), BF16)],
        compiler_params=_cparams("parallel", "parallel"), name="mem_attn",
    )(qm.reshape(b, s, MEM_WIDTH), kv.reshape(b, MEM_TOKENS, 2 * MEM_WIDTH))
    return o.reshape(b * s, MEM_WIDTH)


def _mixture(o_refs, l_refs, e_ref, stage_o, stage_l, dils, tm):
    n_slab = A_WIDTH // LANES
    lses = []
    for pi, d in enumerate(dils):
        if d == 1:
            lses.append(l_refs[pi][0])
        else:
            for r in range(d):
                stage_l[pi, pl.ds(r, tm // d, stride=d), :] = l_refs[pi][r]
                for c in range(n_slab):
                    stage_o[pi, c, pl.ds(r, tm // d, stride=d), :] = (
                        o_refs[pi][r, :, c * LANES:(c + 1) * LANES].astype(F32))
            lses.append(stage_l[pi])
    m = functools.reduce(jnp.maximum, lses)
    es = [jnp.exp(l - m) for l in lses]
    inv = 1.0 / functools.reduce(lambda a, b: a + b, es)
    first = None
    for pi, d in enumerate(dils):
        wgt = es[pi] * inv
        w_hi = wgt.astype(BF16)
        w_lo = (wgt - w_hi.astype(F32)).astype(BF16)
        w_full = _dot(w_hi, e_ref[...]) + _dot(w_lo, e_ref[...])
        if d == 1:
            o = o_refs[pi][0].astype(F32)
        else:
            o = jnp.concatenate([stage_o[pi, c] for c in range(n_slab)], axis=1)
        first = w_full * o if first is None else first + w_full * o
    return first


def _out_proj_kernel(*refs, dils, final, nxt):
    n_mix = len(dils)
    if n_mix:
        o_refs, l_refs, e_ref = refs[:n_mix], refs[n_mix:2 * n_mix], refs[2 * n_mix]
        refs = refs[2 * n_mix + 1:]
    else:
        first = refs[0][...].astype(F32)
        refs = refs[1:]
    second, third, gate_ref, x_ref, w_ref = refs[:5]
    refs = refs[5:]
    if final:
        g_ref = refs[0]
        refs = refs[1:]
    if nxt is not None:
        n_plan, n_tab, n_out, n_dils = nxt
        ng_ref, nw_ref = refs[:2]
        n_tabs = refs[2:2 + 3 * n_tab]
        refs = refs[2 + 3 * n_tab:]
    out_ref = refs[0]
    refs = refs[1:]
    if nxt is not None:
        n_outs, n_subs = refs[:n_out], refs[n_out:n_out + len(n_dils)]
        refs = refs[n_out + len(n_dils):]
    if n_mix:
        stage_o, stage_l = refs[:2]
        refs = refs[2:]
        first = _mixture(o_refs, l_refs, e_ref, stage_o, stage_l, dils, x_ref.shape[0])
    g = gate_ref[...].astype(F32)
    sg = g * (1.0 / (1.0 + jnp.exp(-g)))
    w1 = first.shape[1]
    w2 = w1 + second.shape[1]
    y1 = (first * sg[:, :w1]).astype(BF16)
    y2 = (second[...].astype(F32) * sg[:, w1:w2]).astype(BF16)
    y3 = (third[...].astype(F32) * sg[:, w2:]).astype(BF16)
    xn = x_ref[...] + _dot(y1, w_ref[:w1, :]) + _dot(y2, w_ref[w1:w2, :]) + _dot(y3, w_ref[w2:, :])
    if final:
        xn = xn * lax.rsqrt(jnp.mean(xn * xn, -1, keepdims=True) + NORM_EPS) * g_ref[...]
    out_ref[...] = xn
    if nxt is not None:
        _norm_proj_body(xn, ng_ref, nw_ref, n_tabs, n_outs, n_subs, refs[0] if n_dils else None, n_plan, n_dils)


def _lse_expansion():
    dst = np.arange(A_WIDTH)
    head = dst // HEAD_DIM
    src = (head % 2) * HEAD_DIM + (head // 2) * A_LSE_GROUP
    return jnp.asarray(np.arange(LANES)[:, None] == src[None, :], BF16)


def _out_proj(first, second, third, gate, x2d, w_out, final_gain=None, tm=ROW_TILE, seq=None, next_proj=None):
    t_rows, dm = x2d.shape
    row = lambda a: pl.BlockSpec((tm, a.shape[1]), lambda i: (i, 0))
    full = lambda a: pl.BlockSpec(a.shape, lambda i: (0, 0), pipeline_mode=pl.Buffered(1))
    dils, scratch = (), []
    if isinstance(first, tuple):
        o_list, l_list = first
        dils = tuple(o.shape[1] for o in o_list)
        nt = seq // tm
        sub = lambda a: pl.BlockSpec((None, a.shape[1], tm // a.shape[1], a.shape[3]),
                                     lambda i: (i // nt, 0, i % nt, 0))
        expand = _lse_expansion()
        ins = list(o_list) + list(l_list) + [expand]
        in_specs = [sub(a) for a in ins[:-1]] + [full(expand)]
        scratch = [pltpu.VMEM((len(dils), A_WIDTH // LANES, tm, LANES), F32),
                   pltpu.VMEM((len(dils), tm, LANES), F32)]
    else:
        ins, in_specs = [first], [row(first)]
    for a in (second, third, gate, x2d):
        ins.append(a)
        in_specs.append(row(a))
    ins.append(w_out)
    in_specs.append(full(w_out))
    if final_gain is not None:
        ins.append(final_gain.reshape(1, dm).astype(F32))
        in_specs.append(pl.BlockSpec((1, dm), lambda i: (0, 0)))
    out_specs = [pl.BlockSpec((tm, dm), lambda i: (i, 0))]
    out_shape = [jax.ShapeDtypeStruct((t_rows, dm), F32)]
    nxt = None
    if next_proj is not None:
        n_ins, n_in_specs, n_out_specs, n_out_shape, n_scratch = _proj_operands(next_proj, t_rows, tm, seq)
        ins += n_ins
        in_specs += n_in_specs
        out_specs += n_out_specs
        out_shape += n_out_shape
        scratch = scratch + n_scratch
        nxt = (next_proj['plan'], len(next_proj['tables']), len(next_proj['out_widths']), next_proj['dils'])
    kern = functools.partial(_out_proj_kernel, dils=dils, final=final_gain is not None, nxt=nxt)
    res = pl.pallas_call(
        kern, grid=(t_rows // tm,), in_specs=in_specs, out_specs=out_specs, out_shape=out_shape,
        scratch_shapes=scratch, compiler_params=_cparams("parallel"),
        name="out_proj" if nxt is None else "out_norm_proj",
    )(*ins)
    return res[0] if nxt is None else res


def _mla_prep_kernel(in_ref, qn_ref, kvn_ref, wq_ref, wqs_ref, wk_ref, wv_ref, c_ref, s1_ref, s2_ref,
                     q_out, k_out, v_out):
    c, s = c_ref[...], s1_ref[...] + s2_ref[...]
    k0 = C_Q_RANK + C_KV_RANK
    cq = in_ref[:, :C_Q_RANK].astype(F32)
    ckv = in_ref[:, C_Q_RANK:k0].astype(F32)
    kr = in_ref[:, k0:k0 + LANES].astype(F32)
    kr_sw = in_ref[:, k0 + LANES:].astype(F32)
    cqn = (cq * lax.rsqrt(jnp.mean(cq * cq, -1, keepdims=True) + NORM_EPS) * qn_ref[...]).astype(BF16)
    ckvn = (ckv * lax.rsqrt(jnp.mean(ckv * ckv, -1, keepdims=True) + NORM_EPS) * kvn_ref[...]).astype(BF16)
    q_scale = (C_NOPE + C_ROPE) ** -0.5 * LOG2E
    c_all = jnp.tile(c * q_scale, (1, C_HEADS))
    s_all = jnp.tile(s * q_scale, (1, C_HEADS))
    q_out[...] = (_dot(cqn, wq_ref[...]) * c_all + _dot(cqn, wqs_ref[...]) * s_all).astype(q_out.dtype)
    kpe = kr * c + kr_sw * s
    kn = _dot(ckvn, wk_ref[...])
    for h in range(C_HEADS):
        k_out[:, h * LANES:(h + 1) * LANES] = (kn[:, h * LANES:(h + 1) * LANES] + kpe).astype(k_out.dtype)
    lane = lax.broadcasted_iota(jnp.int32, (1, C_HEADS * LANES), 1)
    ones = jnp.where(lane % LANES >= C_V, 1.0, 0.0)
    v_out[...] = (_dot(ckvn, wv_ref[...]) + ones).astype(v_out.dtype)


def _mla_prep(mla_in, q_norm, kv_norm, wq, wq_sw, wk, wv, tabs, tm=ROW_TILE):
    t_rows = mla_in.shape[0]
    row = lambda w: pl.BlockSpec((tm, w), lambda i: (i, 0))
    full = lambda a: pl.BlockSpec(a.shape, lambda i: (0, 0))
    qn = q_norm.reshape(1, -1).astype(F32)
    kvn = kv_norm.reshape(1, -1).astype(F32)
    return pl.pallas_call(
        _mla_prep_kernel, grid=(t_rows // tm,),
        in_specs=[row(mla_in.shape[1]), full(qn), full(kvn), full(wq), full(wq_sw), full(wk), full(wv),
                  row(LANES), row(LANES), row(LANES)],
        out_specs=[row(C_HEADS * LANES)] * 3,
        out_shape=[jax.ShapeDtypeStruct((t_rows, C_HEADS * LANES), BF16)] * 3,
        compiler_params=_cparams("parallel"), name="mla_prep",
    )(mla_in, qn, kvn, wq, wq_sw, wk, wv, *tabs)


def _mla_attn_kernel(q_ref, k_ref, v_ref, o_ref, m_ref, acc_ref, p_ref, alpha_ref, *, tq):
    i = pl.program_id(1)
    diag_ok = (lax.broadcasted_iota(jnp.int32, (tq, tq), 1) <= lax.broadcasted_iota(jnp.int32, (tq, tq), 0))
    m_ref[...] = jnp.full(m_ref.shape, NEG_INF, F32)
    acc_ref[...] = jnp.zeros(acc_ref.shape, F32)

    def block(j, masked):
        rows = pl.ds(pl.multiple_of(j * tq, tq), tq)
        for h in range(C_HEADS):
            sl = slice(h * LANES, (h + 1) * LANES)
            s = _dot_nt(q_ref[:, sl], k_ref[rows, sl])
            if masked:
                s = jnp.where(diag_ok, s, NEG_INF)
            m_old = m_ref[h]
            m_new = jnp.maximum(m_old, jnp.max(s, -1, keepdims=True))
            alpha_ref[h] = jnp.exp2(m_old - m_new)
            p_ref[h] = jnp.exp2(s - jnp.tile(m_new, (1, tq // LANES))).astype(BF16)
            m_ref[h] = m_new
        for h in range(C_HEADS):
            sl = slice(h * LANES, (h + 1) * LANES)
            acc_ref[h] = alpha_ref[h] * acc_ref[h] + _dot(p_ref[h], v_ref[rows, sl])

    def body(j, carry):
        block(j, False)
        return carry

    lax.fori_loop(0, i, body, 0)
    block(i, True)
    for h in range(C_HEADS):
        acc = acc_ref[h]
        o_ref[:, h * C_V:(h + 1) * C_V] = (acc[:, :C_V] / acc[:, C_V:]).astype(o_ref.dtype)


def _mla_attn(q, k, v, b, s, tq=512):
    qw, vw = C_HEADS * LANES, C_HEADS * C_V
    o = pl.pallas_call(
        functools.partial(_mla_attn_kernel, tq=tq), grid=(b, s // tq),
        in_specs=[pl.BlockSpec((None, tq, qw), lambda bi, i: (bi, i, 0)),
                  pl.BlockSpec((None, s, qw), lambda bi, i: (bi, 0, 0)),
                  pl.BlockSpec((None, s, qw), lambda bi, i: (bi, 0, 0))],
        out_specs=pl.BlockSpec((None, tq, vw), lambda bi, i: (bi, i, 0)),
        out_shape=jax.ShapeDtypeStruct((b, s, vw), BF16),
        scratch_shapes=[pltpu.VMEM((C_HEADS, tq, LANES), F32),
                        pltpu.VMEM((C_HEADS, tq, LANES), F32),
                        pltpu.VMEM((C_HEADS, tq, tq), BF16),
                        pltpu.VMEM((C_HEADS, tq, LANES), F32)],
        compiler_params=_cparams("parallel", "parallel"), name="mla_attn",
    )(q.reshape(b, s, qw), k.reshape(b, s, qw), v.reshape(b, s, qw))
    return o.reshape(b * s, vw)


def _split3(x):
    hi = x.astype(BF16)
    r1 = x - hi.astype(F32)
    mid = r1.astype(BF16)
    lo = (r1 - mid.astype(F32)).astype(BF16)
    return hi, mid, lo


def _rwkv_kernel(*refs, vres, nchunk):
    c = D_CHUNK
    tr = nchunk * c
    if vres:
        (dp_ref, vf_ref, mu_ref, w0_ref, w2_ref, a0_ref, a2_ref, v0_ref, v2_ref,
         kk_ref, ka_ref, rk_ref, lw_ref, lb_ref, ones_ref, out_ref,
         state_ref, carry_ref, y_ref) = refs
    else:
        (dp_ref, mu_ref, w0_ref, w2_ref, a0_ref, a2_ref,
         kk_ref, ka_ref, rk_ref, lw_ref, lb_ref, ones_ref, out_ref, vf_out,
         state_ref, carry_ref, y_ref) = refs

    @pl.when(pl.program_id(1) == 0)
    def _():
        state_ref[...] = jnp.zeros_like(state_ref)
        carry_ref[...] = jnp.zeros_like(carry_ref)

    hcur = dp_ref[...].astype(F32)
    row = lax.broadcasted_iota(jnp.int32, (tr, 1), 0)
    prev = jnp.where(row == 0, carry_ref[...], pltpu.roll(hcur, 1, 0))
    carry_ref[...] = hcur[tr - 1:tr, :]
    sh = hcur + (prev - hcur) * mu_ref[...]

    w = D_WIDTH
    r = sh[:, 0:w]
    k = sh[:, w:2 * w]
    v = sh[:, 2 * w:3 * w]
    wa = sh[:, 3 * w:3 * w + LANES]
    pre_w = w0_ref[...] + _dot(jnp.tanh(wa).astype(BF16), w2_ref[...])
    logd = -math.exp(-0.5) / (1.0 + jnp.exp(-pre_w))
    pre_a = a0_ref[...] + _dot(wa.astype(BF16), a2_ref[...])
    a = 1.0 / (1.0 + jnp.exp(-pre_a))
    if vres:
        vd = sh[:, 3 * w + LANES:3 * w + 2 * LANES]
        pre_v = v0_ref[...] + _dot(vd.astype(BF16), v2_ref[...])
        v = v + (vf_ref[...].astype(F32) - v) * (1.0 / (1.0 + jnp.exp(-pre_v)))
    else:
        vf_out[...] = v.astype(vf_out.dtype)
    def head_sum(t):
        t_hi = t.astype(BF16)
        t_lo = (t - t_hi.astype(F32)).astype(BF16)
        n = t.shape[1] // LANES
        stack = jnp.concatenate([piece[:, i * LANES:(i + 1) * LANES] for i in range(n) for piece in (t_hi, t_lo)], axis=0)
        res = _dot(stack, ones_ref[...])
        rows = t.shape[0]
        return jnp.concatenate([res[2 * i * rows:(2 * i + 1) * rows] + res[(2 * i + 1) * rows:(2 * i + 2) * rows]
                                for i in range(n)], axis=1)

    kk = k * kk_ref[...]
    kk = kk / jnp.maximum(jnp.sqrt(head_sum(kk * kk)), 1e-12)
    k2 = k * (1.0 + (a - 1.0) * ka_ref[...])

    ti = lax.broadcasted_iota(jnp.int32, (tr, tr), 0)
    tj = lax.broadcasted_iota(jnp.int32, (tr, tr), 1)
    tri = jnp.where((tj <= ti) & (tj >= ti // c * c), 1.0, 0.0).astype(BF16)
    hi, mid, lo = _split3(logd)
    cum = _dot(tri, hi) + _dot(tri, mid) + _dot(tri, lo)
    p_inc = jnp.exp(cum)
    p_exc = jnp.where(row % c == 0, 1.0, pltpu.roll(p_inc, 1, 0))
    p_inv = jnp.exp(-cum)
    p_all = [p_inc[(ci + 1) * c - 1:(ci + 1) * c, :] for ci in range(nchunk)]
    p_all_rows = jnp.concatenate([jnp.broadcast_to(pa, (c, D_WIDTH)) for pa in p_all], axis=0)
    kb = kk * a * p_inv
    kt = k2 * p_inv
    at_b = (-kk * p_exc).astype(BF16)
    rt_b = (r * p_inc).astype(BF16)
    bend_b = (kb * p_all_rows).astype(BF16)
    kend_b = (kt * p_all_rows).astype(BF16)
    vb = v.astype(BF16)

    pc = lax.broadcasted_iota(jnp.int32, (c, LANES), 1) % D_HEAD
    pr = lax.broadcasted_iota(jnp.int32, (c, LANES), 0)
    strict = pc < pr
    incl = pc <= pr
    eye = jnp.where(pc == pr, 1.0, 0.0)
    lo_lane = lax.broadcasted_iota(jnp.int32, (c, LANES), 1) < D_HEAD
    sr = lax.broadcasted_iota(jnp.int32, (LANES, LANES), 0) < D_HEAD
    sc = lax.broadcasted_iota(jnp.int32, (LANES, LANES), 1) < D_HEAD
    same_head = sr == sc

    def pair_diag(t):
        return jnp.concatenate([jnp.where(lo_lane, t, 0.0), jnp.where(lo_lane, 0.0, t)], axis=0).astype(BF16)

    items = [(ci, p) for ci in range(nchunk) for p in range(D_HEADS // 2)]
    rsl = lambda ci: slice(ci * c, (ci + 1) * c)
    psl = lambda p: slice(p * LANES, (p + 1) * LANES)
    sub = lambda arr, it: arr[rsl(it[0]), psl(it[1])]
    lhs = {it: jnp.concatenate([sub(at_b, it), sub(rt_b, it)], axis=0) for it in items}
    mats = {it: _dot_nt(lhs[it], jnp.concatenate([pair_diag(sub(kb, it)), pair_diag(sub(kt, it))], axis=0))
            for it in items}
    r_b = {it: jnp.where(incl, mats[it][c:, :LANES], 0.0).astype(BF16) for it in items}
    x0 = {it: jnp.where(strict, mats[it][:c, :LANES], 0.0) for it in items}
    xp = {it: _dot(x0[it].astype(BF16), pair_diag(x0[it])) for it in items}
    tinv = {it: eye + x0[it] for it in items}
    n_levels = int(math.log2(c))
    for lvl in range(1, n_levels - 1):
        prod = {it: _dot(jnp.concatenate([xp[it], tinv[it]], axis=0).astype(BF16), pair_diag(xp[it])) for it in items}
        xp = {it: prod[it][:c] for it in items}
        tinv = {it: tinv[it] + prod[it][c:] for it in items}
    tinv = {it: (tinv[it] + _dot(tinv[it].astype(BF16), pair_diag(xp[it]))).astype(BF16) for it in items}
    m_k = {it: jnp.concatenate([jnp.where(strict, mats[it][:c, LANES:], 0.0),
                                jnp.where(incl, mats[it][c:, LANES:], 0.0)], axis=0).astype(BF16) for it in items}
    from_v = {it: _dot(m_k[it], pair_diag(sub(v, it))) for it in items}
    upd_v = {it: jnp.where(same_head, _dot_tn(sub(vb, it), sub(kend_b, it)), 0.0) for it in items}

    state = [state_ref[p] for p in range(D_HEADS // 2)]
    pairs = range(D_HEADS // 2)
    for ci in range(nchunk):
        from_state = [_dot_nt(lhs[ci, p], state[p].astype(BF16)) for p in pairs]
        u = [_dot(tinv[ci, p], pair_diag(from_state[p][:c] + from_v[ci, p][:c])) for p in pairs]
        upd_u = [_dot_tn(u[p].astype(BF16), sub(bend_b, (ci, p))) for p in pairs]
        state = [state[p] * p_all[ci][:, psl(p)] + jnp.where(same_head, upd_u[p], 0.0) + upd_v[ci, p] for p in pairs]
        for p in pairs:
            y_ref[rsl(ci), psl(p)] = from_state[p][c:] + from_v[ci, p][c:] + _dot(r_b[ci, p], pair_diag(u[p]))
    for p in pairs:
        state_ref[p] = state[p]

    y = y_ref[...]
    yc = y - head_sum(y) * (1.0 / D_HEAD)
    var = head_sum(yc * yc) * (1.0 / D_HEAD)
    yn = yc * lax.rsqrt(var + RWKV_LN_EPS) * lw_ref[...] + lb_ref[...]
    bonus = head_sum(r * k2 * rk_ref[...]) * v
    out_ref[...] = (yn + bonus).astype(out_ref.dtype)


def _rwkv(dproj, v_first, p, b, s):
    tr = D_STEP_CHUNKS * D_CHUNK
    vres = v_first is not None
    row = lambda wd: pl.BlockSpec((None, tr, wd), lambda bi, i: (bi, i, 0))
    full = lambda a: pl.BlockSpec(a.shape, lambda bi, i: (0, 0))
    ins = [dproj.reshape(b, s, D_PROJ_PAD)]
    in_specs = [row(D_PROJ_PAD)]
    if vres:
        ins.append(v_first.reshape(b, s, D_WIDTH))
        in_specs.append(row(D_WIDTH))
    names = ['mu', 'w0', 'w2', 'a0', 'a2'] + (['v0', 'v2'] if vres else []) + ['k_k', 'k_a', 'r_k', 'lnx_w', 'lnx_b']
    for nm in names:
        ins.append(p[nm])
        in_specs.append(full(p[nm]))
    head_id = np.arange(LANES) // D_HEAD
    ones_bd = jnp.asarray(head_id[:, None] == head_id[None, :], BF16)
    ins.append(ones_bd)
    in_specs.append(full(ones_bd))
    out_shape = [jax.ShapeDtypeStruct((b, s, D_WIDTH), BF16)]
    out_specs = [row(D_WIDTH)]
    if not vres:
        out_shape.append(jax.ShapeDtypeStruct((b, s, D_WIDTH), BF16))
        out_specs.append(row(D_WIDTH))
    res = pl.pallas_call(
        functools.partial(_rwkv_kernel, vres=vres, nchunk=D_STEP_CHUNKS), grid=(b, s // tr),
        in_specs=in_specs, out_specs=out_specs, out_shape=out_shape,
        scratch_shapes=[pltpu.VMEM((D_HEADS // 2, LANES, LANES), F32),
                        pltpu.VMEM((1, D_PROJ_PAD), F32),
                        pltpu.VMEM((tr, D_WIDTH), F32)],
        compiler_params=_cparams("parallel", "arbitrary"), name="rwkv7",
    )(*ins)
    d_out = res[0].reshape(b * s, D_WIDTH)
    vf = v_first if vres else res[1].reshape(b * s, D_WIDTH)
    return d_out, vf


def _rope_tables(positions, dim, theta, period, base, half):
    inv = jnp.exp(-math.log(theta) * jnp.arange(0, dim, 2, dtype=F32) / dim)
    n = positions.shape[0] * positions.shape[1]
    lane = np.arange(LANES) % period - base
    lo = (lane >= 0) & (lane < half)
    hi = (lane >= half) & (lane < 2 * half)
    src = np.where(lo, lane, np.where(hi, lane - half, 0))
    ang = positions.astype(F32).reshape(n, 1) * inv[src][None, :]
    c = jnp.where(jnp.asarray(lo | hi), jnp.cos(ang), 1.0)
    s1 = jnp.where(jnp.asarray(hi), jnp.sin(ang), 0.0)
    s2 = jnp.where(jnp.asarray(lo), -jnp.sin(ang), 0.0)
    return c, s1, s2


A_SUB_DILATIONS = tuple(d for d in A_DILATIONS if d > 1)


def _even_proj(norm, w_in, tabs_a, tabs_r):
    q_scale = HEAD_DIM ** -0.5 * LOG2E
    aw = A_WIDTH
    plan = [(0, aw, 0, 0, (0, ROT_DIM // 2, q_scale), 0), (aw, aw, 0, aw, (0, ROT_DIM // 2, 1.0), aw),
            (2 * aw, aw, 0, 2 * aw, None, 2 * aw),
            (1536, 512, 0, 1536, (1, B_QK_DIM // 2, 1.0)), (2048, 512, 0, 2048, None),
            (2560, 256, 1, 0, (None, 0, MEM_Q_SCALE)), (2816, 640, 2, 0, None), (3456, 640, 2, 640, None)]
    return _proj_spec(norm, w_in.astype(BF16), plan, (2560, 256, MIX_WIDTH), (tabs_a, tabs_r),
                      dils=A_SUB_DILATIONS, sub_width=3 * aw)


def _even_mixers(res, b, s, mem_kv):
    main, qm, gate = res[:3]
    main3 = main.reshape(b, s, 2560)
    qkv = {1: main3.reshape(b, 1, s, 2560)}
    qkv.update(zip(A_SUB_DILATIONS, res[3:]))
    outs, lses = [], []
    for dil in A_DILATIONS:
        o, lse = _band_attn(qkv[dil], 0, dil)
        outs.append(o)
        lses.append(lse)
    r_out = _retention(main3)
    m_out = _mem_attn(qm, mem_kv, b, s)
    return (outs, lses), r_out, m_out, gate


def _odd_proj(norm, w_in, d_cols):
    k_dim = w_in.shape[0]
    w_in = w_in.astype(BF16)
    z = lambda n: jnp.zeros((k_dim, n), BF16)
    o = 0
    cq = w_in[:, o:o + C_Q_RANK]; o += C_Q_RANK
    ckv = w_in[:, o:o + C_KV_RANK]; o += C_KV_RANK
    kro = w_in[:, o:o + C_ROPE]; o += C_ROPE
    dp = w_in[:, o:o + d_cols]; o += d_cols
    qm_w = w_in[:, o:o + MEM_WIDTH]; o += MEM_WIDTH
    gate_w = w_in[:, o:]
    dp_pad = jnp.concatenate([dp, z(D_PROJ_PAD - d_cols)], axis=1)
    swap = lambda t: jnp.concatenate([t[..., C_ROPE // 2:], t[..., :C_ROPE // 2]], axis=-1)
    rope_slab = lambda t: jnp.concatenate([z(C_NOPE), t, z(LANES - C_NOPE - C_ROPE)], axis=1)
    w_pad = jnp.concatenate([cq, ckv, rope_slab(kro), rope_slab(swap(kro)), dp_pad, qm_w, gate_w], axis=1)
    m0 = C_Q_RANK + C_KV_RANK + 2 * LANES
    plan = [(0, m0, 0, 0, None), (m0, 896, 1, 0, None), (m0 + 896, 896, 1, 896, None),
            (m0 + 1792, 256, 2, 0, (None, 0, MEM_Q_SCALE)),
            (m0 + 2048, 640, 3, 0, None), (m0 + 2688, 640, 3, 640, None)]
    return _proj_spec(norm, w_pad.astype(BF16), plan, (m0, D_PROJ_PAD, 256, MIX_WIDTH))


def _odd_mixers(res, b, s, mem_kv, tabs_m, v_first, q_norm, w_qb, kv_norm, w_kvb,
                mu_shift, w0, w2, a0, a2, v0, v2, k_k, k_a, r_k, lnx_w, lnx_b):
    mla_in, dproj, qm, gate = res
    vres = v0 is not None
    d_cols = mu_shift.shape[0]
    swap = lambda t: jnp.concatenate([t[..., C_ROPE // 2:], t[..., :C_ROPE // 2]], axis=-1)
    wq = w_qb.reshape(C_Q_RANK, C_HEADS, C_NOPE + C_ROPE)
    slab_pad = ((0, 0), (0, 0), (0, LANES - C_NOPE - C_ROPE))
    wq_sw = jnp.concatenate([wq[..., :C_NOPE], swap(wq[..., C_NOPE:])], axis=-1)
    wq_sw = jnp.pad(wq_sw, slab_pad).reshape(C_Q_RANK, C_HEADS * LANES)
    wq = jnp.pad(wq, slab_pad).reshape(C_Q_RANK, C_HEADS * LANES)
    wkv = w_kvb.reshape(C_KV_RANK, C_HEADS, C_NOPE + C_V)
    wk = jnp.pad(wkv[:, :, :C_NOPE], ((0, 0), (0, 0), (0, LANES - C_NOPE))).reshape(C_KV_RANK, C_HEADS * LANES)
    wv = jnp.pad(wkv[:, :, C_NOPE:], ((0, 0), (0, 0), (0, LANES - C_V))).reshape(C_KV_RANK, C_HEADS * LANES)
    q, k, v = _mla_prep(mla_in, q_norm, kv_norm, wq.astype(BF16), wq_sw.astype(BF16), wk.astype(BF16),
                        wv.astype(BF16), tabs_m)
    c_out = _mla_attn(q, k, v, b, s)

    row = lambda t: t.reshape(1, -1).astype(F32)
    pad_rows = lambda t, top, n: jnp.concatenate(
        [jnp.zeros((top, t.shape[1]), F32), t, jnp.zeros((n - top - t.shape[0], t.shape[1]), F32)], axis=0)
    p = dict(mu=row(jnp.concatenate([mu_shift, jnp.zeros((D_PROJ_PAD - d_cols,), F32)])),
             w0=row(w0), w2=pad_rows(w2, 0, LANES).astype(BF16),
             a0=row(a0), a2=pad_rows(a2, D_DECAY_LORA, LANES).astype(BF16),
             k_k=row(k_k), k_a=row(k_a), r_k=row(r_k), lnx_w=row(lnx_w), lnx_b=row(lnx_b))
    if vres:
        p['v0'] = row(v0)
        p['v2'] = pad_rows(v2, 0, LANES).astype(BF16)
    d_out, v_first = _rwkv(dproj, v_first, p, b, s)
    m_out = _mem_attn(qm, mem_kv, b, s)
    return (c_out, d_out, m_out, gate), v_first


def kernel(x, mem, positions, mem_norm, final_norm, l0_norm, l0_w_in, l0_w_mem_kv, l0_w_out, l1_norm, l1_w_in, l1_q_norm, l1_w_qb, l1_kv_norm, l1_w_kvb, l1_mu_shift, l1_w0, l1_w2, l1_a0, l1_a2, l1_k_k, l1_k_a, l1_r_k, l1_lnx_w, l1_lnx_b, l1_w_mem_kv, l1_w_out, l2_norm, l2_w_in, l2_w_mem_kv, l2_w_out, l3_norm, l3_w_in, l3_q_norm, l3_w_qb, l3_kv_norm, l3_w_kvb, l3_mu_shift, l3_w0, l3_w2, l3_a0, l3_a2, l3_v0, l3_v2, l3_k_k, l3_k_a, l3_r_k, l3_lnx_w, l3_lnx_b, l3_w_mem_kv, l3_w_out):
    b, s, d = x.shape
    x2d = x.reshape(b * s, d)
    mem2d = mem.reshape(b * MEM_TOKENS, d)
    tabs_a = _rope_tables(positions, ROT_DIM, ROPE_THETA, HEAD_DIM, 0, ROT_DIM // 2)
    tabs_r = _rope_tables(positions, B_QK_DIM, RET_THETA, B_QK_DIM, 0, B_QK_DIM // 2)
    tabs_m = _rope_tables(positions, C_ROPE, ROPE_THETA, LANES, C_NOPE, C_ROPE // 2)

    kvw = 2 * MEM_WIDTH
    w_mem = jnp.concatenate([w.astype(BF16) for w in (l0_w_mem_kv, l1_w_mem_kv, l2_w_mem_kv, l3_w_mem_kv)], axis=1)
    mem_kvs = _norm_proj(mem2d, mem_norm, w_mem, [(kvw * l, kvw, l, 0, None) for l in range(4)], (kvw,) * 4,
                         tm=MEM_TOKENS)

    proj = [_even_proj(l0_norm, l0_w_in, tabs_a, tabs_r), _odd_proj(l1_norm, l1_w_in, l1_mu_shift.shape[0]),
            _even_proj(l2_norm, l2_w_in, tabs_a, tabs_r), _odd_proj(l3_norm, l3_w_in, l3_mu_shift.shape[0])]
    w_outs = [w.astype(BF16) for w in (l0_w_out, l1_w_out, l2_w_out, l3_w_out)]
    p0 = proj[0]
    res = _norm_proj(x2d, p0['g'], p0['w'], p0['plan'], p0['out_widths'], p0['tables'], dils=p0['dils'],
                     sub_width=p0['sub_width'], seq=s)
    ops = _even_mixers(res, b, s, mem_kvs[0])
    x2d, *res = _out_proj(*ops, x2d, w_outs[0], seq=s, next_proj=proj[1])
    ops, v_first = _odd_mixers(res, b, s, mem_kvs[1], tabs_m, None, l1_q_norm, l1_w_qb, l1_kv_norm, l1_w_kvb,
                               l1_mu_shift, l1_w0, l1_w2, l1_a0, l1_a2, None, None,
                               l1_k_k, l1_k_a, l1_r_k, l1_lnx_w, l1_lnx_b)
    x2d, *res = _out_proj(*ops, x2d, w_outs[1], seq=s, next_proj=proj[2])
    ops = _even_mixers(res, b, s, mem_kvs[2])
    x2d, *res = _out_proj(*ops, x2d, w_outs[2], seq=s, next_proj=proj[3])
    ops, _ = _odd_mixers(res, b, s, mem_kvs[3], tabs_m, v_first, l3_q_norm, l3_w_qb, l3_kv_norm, l3_w_kvb,
                         l3_mu_shift, l3_w0, l3_w2, l3_a0, l3_a2, l3_v0, l3_v2,
                         l3_k_k, l3_k_a, l3_r_k, l3_lnx_w, l3_lnx_b)
    out = _out_proj(*ops, x2d, w_outs[3], final_gain=final_norm)
    return out.reshape(b, s, d)
```

```python
import functools
import math

import numpy as np
import jax
import jax.numpy as jnp
from jax import lax
from jax.experimental import pallas as pl
from jax.experimental.pallas import tpu as pltpu

F32 = jnp.float32
BF16 = jnp.bfloat16

LANES = 128
VMEM_LIMIT = 56 * 1024 * 1024
ROW_TILE = 512

D_MODEL = 1024
HEAD_DIM = 64
ROPE_THETA = 500000.0
ROT_DIM = HEAD_DIM // 4
NORM_EPS = 1e-6
NEG_INF = -1e30
LN2 = math.log(2.0)
LOG2E = 1.0 / LN2

A_HEADS = 8
A_WIDTH = A_HEADS * HEAD_DIM
A_PATTERNS = ((128, 1), (512, 4), (2048, 16))
A_BLOCK = 128
A_DILATIONS = tuple(d for (_, d) in A_PATTERNS)
A_LSE_GROUP = HEAD_DIM // (A_HEADS // 2)

B_HEADS = 4
B_QK_DIM = 64
B_V_DIM = 128
B_CHUNK = 128
RET_THETA = 10000.0

MEM_TOKENS = 256
MEM_HEADS = 4
MEM_WIDTH = MEM_HEADS * HEAD_DIM
MEM_Q_SCALE = HEAD_DIM ** -0.5 / math.log(2.0)

C_HEADS = 8
C_Q_RANK = 256
C_KV_RANK = 128
C_NOPE = 64
C_ROPE = 32
C_V = 64

D_HEADS = 8
D_HEAD = 64
D_WIDTH = D_HEADS * D_HEAD
D_DECAY_LORA = 64
D_AAA_LORA = 64
D_MV_LORA = 32
RWKV_LN_EPS = 64e-5
D_CHUNK = 64
D_STEP_CHUNKS = 4
D_PROJ_PAD = 3 * D_WIDTH + 2 * LANES

MIX_WIDTH = 1280


def _cparams(*sem):
    return pltpu.CompilerParams(dimension_semantics=sem, vmem_limit_bytes=VMEM_LIMIT)


def _dot(a, b):
    return jnp.dot(a, b, preferred_element_type=F32)


def _dot_nt(a, b):
    return lax.dot_general(a, b, (((1,), (1,)), ((), ())), preferred_element_type=F32)


def _dot_tn(a, b):
    return lax.dot_general(a, b, (((0,), (0,)), ((), ())), preferred_element_type=F32)


def _rot_slab(x, c, s1, s2, half):
    return x * c + pltpu.roll(x, half, 1) * s1 + pltpu.roll(x, LANES - half, 1) * s2


def _rotary(x, c, s1, s2, half):
    slabs = [_rot_slab(x[:, i:i + LANES], c, s1, s2, half) for i in range(0, x.shape[1], LANES)]
    return slabs[0] if len(slabs) == 1 else jnp.concatenate(slabs, axis=1)


def _norm_proj_kernel(*refs, plan, n_tab, n_out, dils):
    x_ref, g_ref, w_ref = refs[:3]
    tabs = refs[3:3 + 3 * n_tab]
    outs = refs[3 + 3 * n_tab:3 + 3 * n_tab + n_out]
    sub_outs = refs[3 + 3 * n_tab + n_out:3 + 3 * n_tab + n_out + len(dils)]
    stage_ref = refs[-1] if dils else None
    _norm_proj_body(x_ref[...].astype(F32), g_ref, w_ref, tabs, outs, sub_outs, stage_ref, plan, dils)


def _norm_proj_body(x, g_ref, w_ref, tabs, outs, sub_outs, stage_ref, plan, dils):
    tm = x.shape[0]
    ms = jnp.mean(x * x, axis=-1, keepdims=True)
    xn = (x * lax.rsqrt(ms + NORM_EPS) * g_ref[...]).astype(BF16)
    n_staged = [0]

    def epilogue(entry, acc):
        (_, width, oi, oc, rot, sub_col) = entry
        if rot is not None:
            t, half, post_scale = rot
            if t is not None:
                acc = _rotary(acc, tabs[3 * t][...], tabs[3 * t + 1][...], tabs[3 * t + 2][...], half)
            if post_scale != 1.0:
                acc = acc * post_scale
        outs[oi][:, oc:oc + width] = acc.astype(outs[oi].dtype)
        if sub_col is not None:
            slot = n_staged[0]
            n_staged[0] += 1
            for c in range(width // LANES):
                stage_ref[slot, c] = acc[:, c * LANES:(c + 1) * LANES]
            for d, sub in zip(dils, sub_outs):
                for r in range(d):
                    for c in range(width // LANES):
                        col = sub_col + c * LANES
                        sub[r, :, col:col + LANES] = stage_ref[slot, c, pl.ds(r, tm // d, stride=d), :].astype(sub.dtype)

    pending = None
    for entry in plan:
        acc = _dot(xn, w_ref[:, entry[0]:entry[0] + entry[1]])
        if pending is not None:
            epilogue(*pending)
        pending = (entry, acc)
    epilogue(*pending)


def _norm_proj(x2d, g, w, plan, out_widths, tables=(), tm=ROW_TILE, dils=(), sub_width=0, seq=None):
    t_rows, k = x2d.shape
    assert t_rows % tm == 0
    proj = _proj_spec(g, w, plan, out_widths, tables, dils, sub_width)
    ins, in_specs, out_specs, out_shape, scratch = _proj_operands(proj, t_rows, tm, seq)
    kern = functools.partial(_norm_proj_kernel, plan=proj['plan'], n_tab=len(tables), n_out=len(out_widths),
                             dils=tuple(dils))
    return pl.pallas_call(
        kern, grid=(t_rows // tm,), in_specs=[pl.BlockSpec((tm, k), lambda i: (i, 0))] + in_specs,
        out_specs=out_specs, out_shape=out_shape,
        scratch_shapes=scratch, compiler_params=_cparams("parallel"), name="norm_proj",
    )(x2d, *ins)


def _proj_spec(g, w, plan, out_widths, tables=(), dils=(), sub_width=0):
    plan = tuple(tuple(p) + (None,) * (6 - len(p)) for p in plan)
    return dict(g=g, w=w, plan=plan, out_widths=tuple(out_widths), tables=tuple(tables), dils=tuple(dils),
                sub_width=sub_width)


def _proj_operands(proj, t_rows, tm, seq):
    k, n = proj['w'].shape
    flat_tabs = [t for tab in proj['tables'] for t in tab]
    ins = [proj['g'].reshape(1, k).astype(F32), proj['w']] + flat_tabs
    in_specs = [pl.BlockSpec((1, k), lambda i: (0, 0)),
                pl.BlockSpec((k, n), lambda i: (0, 0), pipeline_mode=pl.Buffered(1))]
    in_specs += [pl.BlockSpec((tm, LANES), lambda i: (i, 0)) for _ in flat_tabs]
    out_specs = [pl.BlockSpec((tm, ow), lambda i: (i, 0)) for ow in proj['out_widths']]
    out_shape = [jax.ShapeDtypeStruct((t_rows, ow), BF16) for ow in proj['out_widths']]
    scratch = []
    if proj['dils']:
        nt = seq // tm
        sub_width = proj['sub_width']
        for d in proj['dils']:
            out_specs.append(pl.BlockSpec((None, d, tm // d, sub_width), lambda i: (i // nt, 0, i % nt, 0)))
            out_shape.append(jax.ShapeDtypeStruct((t_rows // seq, d, seq // d, sub_width), BF16))
        staged = [p[1] for p in proj['plan'] if p[5] is not None]
        scratch = [pltpu.VMEM((len(staged), max(staged) // LANES, tm, LANES), F32)]
    return ins, in_specs, out_specs, out_shape, scratch


def _band_attn_kernel(q_ref, kp_ref, kc_ref, vp_ref, vc_ref, o_ref, lse_ref, s_ref, p_ref, m_ref, *, nq):
    i = pl.program_id(2)
    qi = lax.broadcasted_iota(jnp.int32, (A_BLOCK, A_BLOCK), 0)
    kj = lax.broadcasted_iota(jnp.int32, (A_BLOCK, A_BLOCK), 1)
    bias_prev = jnp.where(kj >= qi, 0.0, NEG_INF)
    bias_cur = jnp.where(kj <= qi, 0.0, NEG_INF)
    bias_first = bias_prev + jnp.where(i > 0, 0.0, NEG_INF)
    lane = lax.broadcasted_iota(jnp.int32, (A_BLOCK, LANES), 1)
    lo = lane < HEAD_DIM
    keep = (jnp.where(lo, 1.0, 0.0).astype(BF16), jnp.where(lo, 0.0, 1.0).astype(BF16))
    npair = A_HEADS // 2

    def operands(u, pair):
        sl = slice(pair * LANES, (pair + 1) * LANES)
        rows = slice(u * A_BLOCK, (u + 1) * A_BLOCK)
        if u == 0:
            return rows, sl, kp_ref[:, sl], vp_ref[:, sl], bias_first
        prow = slice((u - 1) * A_BLOCK, u * A_BLOCK)
        return rows, sl, kc_ref[prow, sl], vc_ref[prow, sl], bias_prev

    for u in range(nq):
        for pair in range(npair):
            rows, sl, kp, _, bp = operands(u, pair)
            q = q_ref[rows, sl]
            kc = kc_ref[rows, sl]
            for half in range(2):
                idx = (u * npair + pair) * 2 + half
                qh = q * keep[half]
                s_ref[idx, :, :A_BLOCK] = _dot_nt(qh, kp) + bp
                s_ref[idx, :, A_BLOCK:] = _dot_nt(qh, kc) + bias_cur
    for idx in range(nq * A_HEADS):
        s = s_ref[idx]
        m = jnp.max(s, -1, keepdims=True)
        p_ref[idx] = jnp.exp2(s - m).astype(BF16)
        m_ref[idx] = jnp.broadcast_to(m, (A_BLOCK, LANES))
    for u in range(nq):
        lse = None
        for pair in range(npair):
            rows, sl, _, vp, _ = operands(u, pair)
            vc = vc_ref[rows, sl]
            idx0 = (u * npair + pair) * 2
            o = [_dot(p_ref[idx0 + half, :, :A_BLOCK], vp * keep[half] + keep[1 - half])
                 + _dot(p_ref[idx0 + half, :, A_BLOCK:], vc * keep[half] + keep[1 - half]) for half in range(2)]
            den = pltpu.roll(jnp.where(lo, o[1], o[0]), HEAD_DIM, 1)
            o_ref[rows, sl] = (jnp.where(lo, o[0], o[1]) / den).astype(o_ref.dtype)
            lse_pair = jnp.where(lo, m_ref[idx0], m_ref[idx0 + 1]) * LN2 + jnp.log(den)
            lse = lse_pair if lse is None else jnp.where(lane % HEAD_DIM // A_LSE_GROUP == pair, lse_pair, lse)
        lse_ref[rows, :] = lse


def _single_block_attn_kernel(q_ref, k_ref, v_ref, o_ref, lse_ref, s_ref, p_ref, m_ref, *, nres):
    qi = lax.broadcasted_iota(jnp.int32, (A_BLOCK, A_BLOCK), 0)
    kj = lax.broadcasted_iota(jnp.int32, (A_BLOCK, A_BLOCK), 1)
    bias = jnp.where(kj <= qi, 0.0, NEG_INF)
    lane = lax.broadcasted_iota(jnp.int32, (A_BLOCK, LANES), 1)
    lo = lane < HEAD_DIM
    keep = (jnp.where(lo, 1.0, 0.0).astype(BF16), jnp.where(lo, 0.0, 1.0).astype(BF16))
    npair = A_HEADS // 2
    for r in range(nres):
        for pair in range(npair):
            sl = slice(pair * LANES, (pair + 1) * LANES)
            for half in range(2):
                s_ref[(r * npair + pair) * 2 + half] = _dot_nt(q_ref[r, :, sl] * keep[half], k_ref[r, :, sl]) + bias
    for idx in range(nres * A_HEADS):
        s = s_ref[idx]
        m = jnp.max(s, -1, keepdims=True)
        p_ref[idx] = jnp.exp2(s - m).astype(BF16)
        m_ref[idx] = jnp.broadcast_to(m, (A_BLOCK, LANES))
    for r in range(nres):
        lse = None
        for pair in range(npair):
            sl = slice(pair * LANES, (pair + 1) * LANES)
            idx0 = (r * npair + pair) * 2
            v = v_ref[r, :, sl]
            o = [_dot(p_ref[idx0 + half], v * keep[half] + keep[1 - half]) for half in range(2)]
            den = pltpu.roll(jnp.where(lo, o[1], o[0]), HEAD_DIM, 1)
            o_ref[r, :, sl] = (jnp.where(lo, o[0], o[1]) / den).astype(o_ref.dtype)
            lse_pair = jnp.where(lo, m_ref[idx0], m_ref[idx0 + 1]) * LN2 + jnp.log(den)
            lse = lse_pair if lse is None else jnp.where(lane % HEAD_DIM // A_LSE_GROUP == pair, lse_pair, lse)
        lse_ref[r] = lse


def _single_block_attn(qkv, col0, dilation, nres=8):
    b, d, sub_len, _ = qkv.shape
    assert sub_len == A_BLOCK and d % nres == 0
    spec = lambda col, w: pl.BlockSpec((None, nres, A_BLOCK, w), lambda bi, r: (bi, r, 0, col))
    return pl.pallas_call(
        functools.partial(_single_block_attn_kernel, nres=nres), grid=(b, d // nres),
        in_specs=[spec(col0, A_WIDTH), spec(col0 + 1, A_WIDTH), spec(col0 + 2, A_WIDTH)],
        out_specs=[spec(0, A_WIDTH), spec(0, LANES)],
        out_shape=[jax.ShapeDtypeStruct((b, d, sub_len, A_WIDTH), BF16),
                   jax.ShapeDtypeStruct((b, d, sub_len, LANES), F32)],
        scratch_shapes=[pltpu.VMEM((nres * A_HEADS, A_BLOCK, A_BLOCK), F32),
                        pltpu.VMEM((nres * A_HEADS, A_BLOCK, A_BLOCK), BF16),
                        pltpu.VMEM((nres * A_HEADS, A_BLOCK, LANES), F32)],
        compiler_params=_cparams("parallel", "parallel"), name="band_attn_d%d" % dilation,
    )(qkv, qkv, qkv)


def _band_attn(qkv, col0, dilation):
    b, d, sub_len, _ = qkv.shape
    assert d == dilation
    if sub_len == A_BLOCK:
        return _single_block_attn(qkv, col0, dilation)
    nq = 4 if sub_len % (4 * A_BLOCK) == 0 else 1
    tq = nq * A_BLOCK

    def cur(col):
        return pl.BlockSpec((None, None, tq, A_WIDTH), lambda bi, r, i: (bi, r, i, col0 + col))

    def prev(col):
        return pl.BlockSpec((None, None, A_BLOCK, A_WIDTH),
                            lambda bi, r, i: (bi, r, jnp.maximum(i * nq - 1, 0), col0 + col))

    return pl.pallas_call(
        functools.partial(_band_attn_kernel, nq=nq), grid=(b, d, sub_len // tq),
        in_specs=[cur(0), prev(1), cur(1), prev(2), cur(2)],
        out_specs=[pl.BlockSpec((None, None, tq, A_WIDTH), lambda bi, r, i: (bi, r, i, 0)),
                   pl.BlockSpec((None, None, tq, LANES), lambda bi, r, i: (bi, r, i, 0))],
        out_shape=[jax.ShapeDtypeStruct((b, d, sub_len, A_WIDTH), BF16),
                   jax.ShapeDtypeStruct((b, d, sub_len, LANES), F32)],
        scratch_shapes=[pltpu.VMEM((nq * A_HEADS, A_BLOCK, 2 * A_BLOCK), F32),
                        pltpu.VMEM((nq * A_HEADS, A_BLOCK, 2 * A_BLOCK), BF16),
                        pltpu.VMEM((nq * A_HEADS, A_BLOCK, LANES), F32)],
        compiler_params=_cparams("parallel", "parallel", "parallel"), name="band_attn_d%d" % dilation,
    )(qkv, qkv, qkv, qkv, qkv)


def _retention_kernel(qk_ref, v_ref, o_ref, state_ref, *, nchunk):
    c = B_CHUNK
    @pl.when(pl.program_id(1) == 0)
    def _():
        state_ref[...] = jnp.zeros_like(state_ref)

    scale = B_QK_DIM ** -0.5
    qi = lax.broadcasted_iota(jnp.int32, (c, c), 0)
    kj = lax.broadcasted_iota(jnp.int32, (c, c), 1)
    diff = (qi - kj).astype(F32)
    row = lax.broadcasted_iota(jnp.int32, (c, 1), 0).astype(F32)
    lo = lax.broadcasted_iota(jnp.int32, (c, LANES), 1) < B_QK_DIM
    keep = (jnp.where(lo, 1.0, 0.0).astype(BF16), jnp.where(lo, 0.0, 1.0).astype(BF16))
    log_g = [math.log1p(-2.0 ** (-5.0 - h)) for h in range(B_HEADS)]
    d_in = [jnp.where(diff >= 0, jnp.exp(jnp.maximum(diff, 0.0) * lg), 0.0) * scale for lg in log_g]
    from_start = [jnp.exp((row + 1.0) * lg) for lg in log_g]
    to_end = [jnp.exp((c - 1.0 - row) * lg) * scale for lg in log_g]
    items = [(ci, h) for ci in range(nchunk) for h in range(B_HEADS)]
    rows = lambda ci: slice(ci * c, (ci + 1) * c)
    qsl = lambda h: slice(h // 2 * LANES, (h // 2 + 1) * LANES)
    ksl = lambda h: slice(B_HEADS * B_QK_DIM + h // 2 * LANES, B_HEADS * B_QK_DIM + (h // 2 + 1) * LANES)
    vsl = lambda h: slice(h * B_V_DIM, (h + 1) * B_V_DIM)
    qm = {(ci, h): qk_ref[rows(ci), qsl(h)] * keep[h % 2] for ci, h in items}
    scores = {(ci, h): (_dot_nt(qm[ci, h], qk_ref[rows(ci), ksl(h)]) * d_in[h]).astype(BF16) for ci, h in items}
    o_in = {(ci, h): _dot(scores[ci, h], v_ref[rows(ci), vsl(h)]) for ci, h in items}
    kd = {(ci, h): ((qk_ref[rows(ci), ksl(h)] * keep[h % 2]).astype(F32) * to_end[h]).astype(BF16) for ci, h in items}
    kv = {(ci, h): _dot_tn(kd[ci, h], v_ref[rows(ci), vsl(h)]) for ci, h in items}
    state = [state_ref[h] for h in range(B_HEADS)]
    for ci in range(nchunk):
        o = [o_in[ci, h] + _dot(qm[ci, h], state[h].astype(BF16)) * from_start[h] for h in range(B_HEADS)]
        state = [state[h] * math.exp(c * log_g[h]) + kv[ci, h] for h in range(B_HEADS)]
        for h in range(B_HEADS):
            on = o[h] * lax.rsqrt(jnp.mean(o[h] * o[h], -1, keepdims=True) + NORM_EPS)
            o_ref[rows(ci), vsl(h)] = on.astype(o_ref.dtype)
    for h in range(B_HEADS):
        state_ref[h] = state[h]


def _retention(main, nchunk=4):
    b, s, _ = main.shape
    blk = (None, nchunk * B_CHUNK, 512)
    o = pl.pallas_call(
        functools.partial(_retention_kernel, nchunk=nchunk), grid=(b, s // (nchunk * B_CHUNK)),
        in_specs=[pl.BlockSpec(blk, lambda bi, i: (bi, i, 3)), pl.BlockSpec(blk, lambda bi, i: (bi, i, 4))],
        out_specs=pl.BlockSpec(blk, lambda bi, i: (bi, i, 0)),
        out_shape=jax.ShapeDtypeStruct((b, s, B_HEADS * B_V_DIM), BF16),
        scratch_shapes=[pltpu.VMEM((B_HEADS, LANES, B_V_DIM), F32)],
        compiler_params=_cparams("parallel", "arbitrary"), name="retention",
    )(main, main)
    return o.reshape(b * s, B_HEADS * B_V_DIM)


def _mem_attn_kernel(q_ref, kv_ref, o_ref, s_ref, p_ref):
    tq = q_ref.shape[0]
    lo_q = lax.broadcasted_iota(jnp.int32, (tq, LANES), 1) < HEAD_DIM
    lo_k = lax.broadcasted_iota(jnp.int32, (MEM_TOKENS, LANES), 1) < HEAD_DIM
    keep_q = (jnp.where(lo_q, 1.0, 0.0).astype(BF16), jnp.where(lo_q, 0.0, 1.0).astype(BF16))
    keep_k = (jnp.where(lo_k, 1.0, 0.0).astype(BF16), jnp.where(lo_k, 0.0, 1.0).astype(BF16))
    npair = MEM_HEADS // 2
    for pair in range(npair):
        sl = slice(pair * LANES, (pair + 1) * LANES)
        for half in range(2):
            s_ref[2 * pair + half] = _dot_nt(q_ref[:, sl] * keep_q[half], kv_ref[:, sl])
    for idx in range(MEM_HEADS):
        s = s_ref[idx]
        p_ref[idx] = jnp.exp2(s - jnp.max(s, -1, keepdims=True)).astype(BF16)
    for pair in range(npair):
        sl = slice(pair * LANES, (pair + 1) * LANES)
        v = kv_ref[:, MEM_WIDTH + pair * LANES:MEM_WIDTH + (pair + 1) * LANES]
        o = [_dot(p_ref[2 * pair + half], v * keep_k[half] + keep_k[1 - half]) for half in range(2)]
        den = pltpu.roll(jnp.where(lo_q, o[1], o[0]), HEAD_DIM, 1)
        o_ref[:, sl] = (jnp.where(lo_q, o[0], o[1]) / den).astype(o_ref.dtype)


def _mem_attn(qm, kv, b, s, tq=1024):
    o = pl.pallas_call(
        _mem_attn_kernel, grid=(b, s // tq),
        in_specs=[pl.BlockSpec((None, tq, MEM_WIDTH), lambda bi, i: (bi, i, 0)),
                  pl.BlockSpec((None, MEM_TOKENS, 2 * MEM_WIDTH), lambda bi, i: (bi, 0, 0))],
        out_specs=pl.BlockSpec((None, tq, MEM_WIDTH), lambda bi, i: (bi, i, 0)),
        out_shape=jax.ShapeDtypeStruct((b, s, MEM_WIDTH), BF16),
        scratch_shapes=[pltpu.VMEM((MEM_HEADS, tq, MEM_TOKENS), F32),
                        pltpu.VMEM((MEM_HEADS, tq, MEM_TOKENS), BF16)],
        compiler_params=_cparams("parallel", "parallel"), name="mem_attn",
    )(qm.reshape(b, s, MEM_WIDTH), kv.reshape(b, MEM_TOKENS, 2 * MEM_WIDTH))
    return o.reshape(b * s, MEM_WIDTH)


def _mixture(o_refs, l_refs, e_ref, stage_o, stage_l, dils, tm):
    n_slab = A_WIDTH // LANES
    lses = []
    for pi, d in enumerate(dils):
        if d == 1:
            lses.append(l_refs[pi][0])
        else:
            for r in range(d):
                stage_l[pi, pl.ds(r, tm // d, stride=d), :] = l_refs[pi][r]
                for c in range(n_slab):
                    stage_o[pi, c, pl.ds(r, tm // d, stride=d), :] = (
                        o_refs[pi][r, :, c * LANES:(c + 1) * LANES].astype(F32))
            lses.append(stage_l[pi])
    m = functools.reduce(jnp.maximum, lses)
    es = [jnp.exp(l - m) for l in lses]
    inv = 1.0 / functools.reduce(lambda a, b: a + b, es)
    first = None
    for pi, d in enumerate(dils):
        wgt = es[pi] * inv
        w_hi = wgt.astype(BF16)
        w_lo = (wgt - w_hi.astype(F32)).astype(BF16)
        w_full = _dot(w_hi, e_ref[...]) + _dot(w_lo, e_ref[...])
        if d == 1:
            o = o_refs[pi][0].astype(F32)
        else:
            o = jnp.concatenate([stage_o[pi, c] for c in range(n_slab)], axis=1)
        first = w_full * o if first is None else first + w_full * o
    return first


def _out_proj_kernel(*refs, dils, final, nxt):
    n_mix = len(dils)
    if n_mix:
        o_refs, l_refs, e_ref = refs[:n_mix], refs[n_mix:2 * n_mix], refs[2 * n_mix]
        refs = refs[2 * n_mix + 1:]
    else:
        first = refs[0][...].astype(F32)
        refs = refs[1:]
    second, third, gate_ref, x_ref, w_ref = refs[:5]
    refs = refs[5:]
    if final:
        g_ref = refs[0]
        refs = refs[1:]
    if nxt is not None:
        n_plan, n_tab, n_out, n_dils = nxt
        ng_ref, nw_ref = refs[:2]
        n_tabs = refs[2:2 + 3 * n_tab]
        refs = refs[2 + 3 * n_tab:]
    out_ref = refs[0]
    refs = refs[1:]
    if nxt is not None:
        n_outs, n_subs = refs[:n_out], refs[n_out:n_out + len(n_dils)]
        refs = refs[n_out + len(n_dils):]
    if n_mix:
        stage_o, stage_l = refs[:2]
        refs = refs[2:]
        first = _mixture(o_refs, l_refs, e_ref, stage_o, stage_l, dils, x_ref.shape[0])
    g = gate_ref[...].astype(F32)
    sg = g * (1.0 / (1.0 + jnp.exp(-g)))
    w1 = first.shape[1]
    w2 = w1 + second.shape[1]
    y1 = (first * sg[:, :w1]).astype(BF16)
    y2 = (second[...].astype(F32) * sg[:, w1:w2]).astype(BF16)
    y3 = (third[...].astype(F32) * sg[:, w2:]).astype(BF16)
    xn = x_ref[...] + _dot(y1, w_ref[:w1, :]) + _dot(y2, w_ref[w1:w2, :]) + _dot(y3, w_ref[w2:, :])
    if final:
        xn = xn * lax.rsqrt(jnp.mean(xn * xn, -1, keepdims=True) + NORM_EPS) * g_ref[...]
    out_ref[...] = xn
    if nxt is not None:
        _norm_proj_body(xn, ng_ref, nw_ref, n_tabs, n_outs, n_subs, refs[0] if n_dils else None, n_plan, n_dils)


def _lse_expansion():
    dst = np.arange(A_WIDTH)
    head = dst // HEAD_DIM
    src = (head % 2) * HEAD_DIM + (head // 2) * A_LSE_GROUP
    return jnp.asarray(np.arange(LANES)[:, None] == src[None, :], BF16)


def _out_proj(first, second, third, gate, x2d, w_out, final_gain=None, tm=ROW_TILE, seq=None, next_proj=None):
    t_rows, dm = x2d.shape
    row = lambda a: pl.BlockSpec((tm, a.shape[1]), lambda i: (i, 0))
    full = lambda a: pl.BlockSpec(a.shape, lambda i: (0, 0), pipeline_mode=pl.Buffered(1))
    dils, scratch = (), []
    if isinstance(first, tuple):
        o_list, l_list = first
        dils = tuple(o.shape[1] for o in o_list)
        nt = seq // tm
        sub = lambda a: pl.BlockSpec((None, a.shape[1], tm // a.shape[1], a.shape[3]),
                                     lambda i: (i // nt, 0, i % nt, 0))
        expand = _lse_expansion()
        ins = list(o_list) + list(l_list) + [expand]
        in_specs = [sub(a) for a in ins[:-1]] + [full(expand)]
        scratch = [pltpu.VMEM((len(dils), A_WIDTH // LANES, tm, LANES), F32),
                   pltpu.VMEM((len(dils), tm, LANES), F32)]
    else:
        ins, in_specs = [first], [row(first)]
    for a in (second, third, gate, x2d):
        ins.append(a)
        in_specs.append(row(a))
    ins.append(w_out)
    in_specs.append(full(w_out))
    if final_gain is not None:
        ins.append(final_gain.reshape(1, dm).astype(F32))
        in_specs.append(pl.BlockSpec((1, dm), lambda i: (0, 0)))
    out_specs = [pl.BlockSpec((tm, dm), lambda i: (i, 0))]
    out_shape = [jax.ShapeDtypeStruct((t_rows, dm), F32)]
    nxt = None
    if next_proj is not None:
        n_ins, n_in_specs, n_out_specs, n_out_shape, n_scratch = _proj_operands(next_proj, t_rows, tm, seq)
        ins += n_ins
        in_specs += n_in_specs
        out_specs += n_out_specs
        out_shape += n_out_shape
        scratch = scratch + n_scratch
        nxt = (next_proj['plan'], len(next_proj['tables']), len(next_proj['out_widths']), next_proj['dils'])
    kern = functools.partial(_out_proj_kernel, dils=dils, final=final_gain is not None, nxt=nxt)
    res = pl.pallas_call(
        kern, grid=(t_rows // tm,), in_specs=in_specs, out_specs=out_specs, out_shape=out_shape,
        scratch_shapes=scratch, compiler_params=_cparams("parallel"),
        name="out_proj" if nxt is None else "out_norm_proj",
    )(*ins)
    return res[0] if nxt is None else res


def _mla_prep_kernel(in_ref, qn_ref, kvn_ref, wq_ref, wqs_ref, wk_ref, wv_ref, c_ref, s1_ref, s2_ref,
                     q_out, k_out, v_out):
    c, s = c_ref[...], s1_ref[...] + s2_ref[...]
    k0 = C_Q_RANK + C_KV_RANK
    cq = in_ref[:, :C_Q_RANK].astype(F32)
    ckv = in_ref[:, C_Q_RANK:k0].astype(F32)
    kr = in_ref[:, k0:k0 + LANES].astype(F32)
    kr_sw = in_ref[:, k0 + LANES:].astype(F32)
    cqn = (cq * lax.rsqrt(jnp.mean(cq * cq, -1, keepdims=True) + NORM_EPS) * qn_ref[...]).astype(BF16)
    ckvn = (ckv * lax.rsqrt(jnp.mean(ckv * ckv, -1, keepdims=True) + NORM_EPS) * kvn_ref[...]).astype(BF16)
    q_scale = (C_NOPE + C_ROPE) ** -0.5 * LOG2E
    c_all = jnp.tile(c * q_scale, (1, C_HEADS))
    s_all = jnp.tile(s * q_scale, (1, C_HEADS))
    q_out[...] = (_dot(cqn, wq_ref[...]) * c_all + _dot(cqn, wqs_ref[...]) * s_all).astype(q_out.dtype)
    kpe = kr * c + kr_sw * s
    kn = _dot(ckvn, wk_ref[...])
    for h in range(C_HEADS):
        k_out[:, h * LANES:(h + 1) * LANES] = (kn[:, h * LANES:(h + 1) * LANES] + kpe).astype(k_out.dtype)
    lane = lax.broadcasted_iota(jnp.int32, (1, C_HEADS * LANES), 1)
    ones = jnp.where(lane % LANES >= C_V, 1.0, 0.0)
    v_out[...] = (_dot(ckvn, wv_ref[...]) + ones).astype(v_out.dtype)


def _mla_prep(mla_in, q_norm, kv_norm, wq, wq_sw, wk, wv, tabs, tm=ROW_TILE):
    t_rows = mla_in.shape[0]
    row = lambda w: pl.BlockSpec((tm, w), lambda i: (i, 0))
    full = lambda a: pl.BlockSpec(a.shape, lambda i: (0, 0))
    qn = q_norm.reshape(1, -1).astype(F32)
    kvn = kv_norm.reshape(1, -1).astype(F32)
    return pl.pallas_call(
        _mla_prep_kernel, grid=(t_rows // tm,),
        in_specs=[row(mla_in.shape[1]), full(qn), full(kvn), full(wq), full(wq_sw), full(wk), full(wv),
                  row(LANES), row(LANES), row(LANES)],
        out_specs=[row(C_HEADS * LANES)] * 3,
        out_shape=[jax.ShapeDtypeStruct((t_rows, C_HEADS * LANES), BF16)] * 3,
        compiler_params=_cparams("parallel"), name="mla_prep",
    )(mla_in, qn, kvn, wq, wq_sw, wk, wv, *tabs)


def _mla_attn_kernel(q_ref, k_ref, v_ref, o_ref, m_ref, acc_ref, p_ref, alpha_ref, *, tq):
    i = pl.program_id(1)
    diag_ok = (lax.broadcasted_iota(jnp.int32, (tq, tq), 1) <= lax.broadcasted_iota(jnp.int32, (tq, tq), 0))
    m_ref[...] = jnp.full(m_ref.shape, NEG_INF, F32)
    acc_ref[...] = jnp.zeros(acc_ref.shape, F32)

    def block(j, masked):
        rows = pl.ds(pl.multiple_of(j * tq, tq), tq)
        for h in range(C_HEADS):
            sl = slice(h * LANES, (h + 1) * LANES)
            s = _dot_nt(q_ref[:, sl], k_ref[rows, sl])
            if masked:
                s = jnp.where(diag_ok, s, NEG_INF)
            m_old = m_ref[h]
            m_new = jnp.maximum(m_old, jnp.max(s, -1, keepdims=True))
            alpha_ref[h] = jnp.exp2(m_old - m_new)
            p_ref[h] = jnp.exp2(s - jnp.tile(m_new, (1, tq // LANES))).astype(BF16)
            m_ref[h] = m_new
        for h in range(C_HEADS):
            sl = slice(h * LANES, (h + 1) * LANES)
            acc_ref[h] = alpha_ref[h] * acc_ref[h] + _dot(p_ref[h], v_ref[rows, sl])

    def body(j, carry):
        block(j, False)
        return carry

    lax.fori_loop(0, i, body, 0)
    block(i, True)
    for h in range(C_HEADS):
        acc = acc_ref[h]
        o_ref[:, h * C_V:(h + 1) * C_V] = (acc[:, :C_V] / acc[:, C_V:]).astype(o_ref.dtype)


def _mla_attn(q, k, v, b, s, tq=512):
    qw, vw = C_HEADS * LANES, C_HEADS * C_V
    o = pl.pallas_call(
        functools.partial(_mla_attn_kernel, tq=tq), grid=(b, s // tq),
        in_specs=[pl.BlockSpec((None, tq, qw), lambda bi, i: (bi, i, 0)),
                  pl.BlockSpec((None, s, qw), lambda bi, i: (bi, 0, 0)),
                  pl.BlockSpec((None, s, qw), lambda bi, i: (bi, 0, 0))],
        out_specs=pl.BlockSpec((None, tq, vw), lambda bi, i: (bi, i, 0)),
        out_shape=jax.ShapeDtypeStruct((b, s, vw), BF16),
        scratch_shapes=[pltpu.VMEM((C_HEADS, tq, LANES), F32),
                        pltpu.VMEM((C_HEADS, tq, LANES), F32),
                        pltpu.VMEM((C_HEADS, tq, tq), BF16),
                        pltpu.VMEM((C_HEADS, tq, LANES), F32)],
        compiler_params=_cparams("parallel", "parallel"), name="mla_attn",
    )(q.reshape(b, s, qw), k.reshape(b, s, qw), v.reshape(b, s, qw))
    return o.reshape(b * s, vw)


def _split3(x):
    hi = x.astype(BF16)
    r1 = x - hi.astype(F32)
    mid = r1.astype(BF16)
    lo = (r1 - mid.astype(F32)).astype(BF16)
    return hi, mid, lo


def _rwkv_kernel(*refs, vres, nchunk):
    c = D_CHUNK
    tr = nchunk * c
    if vres:
        (dp_ref, vf_ref, mu_ref, w0_ref, w2_ref, a0_ref, a2_ref, v0_ref, v2_ref,
         kk_ref, ka_ref, rk_ref, lw_ref, lb_ref, ones_ref, out_ref,
         state_ref, carry_ref, y_ref) = refs
    else:
        (dp_ref, mu_ref, w0_ref, w2_ref, a0_ref, a2_ref,
         kk_ref, ka_ref, rk_ref, lw_ref, lb_ref, ones_ref, out_ref, vf_out,
         state_ref, carry_ref, y_ref) = refs

    @pl.when(pl.program_id(1) == 0)
    def _():
        state_ref[...] = jnp.zeros_like(state_ref)
        carry_ref[...] = jnp.zeros_like(carry_ref)

    hcur = dp_ref[...].astype(F32)
    row = lax.broadcasted_iota(jnp.int32, (tr, 1), 0)
    prev = jnp.where(row == 0, carry_ref[...], pltpu.roll(hcur, 1, 0))
    carry_ref[...] = hcur[tr - 1:tr, :]
    sh = hcur + (prev - hcur) * mu_ref[...]

    w = D_WIDTH
    r = sh[:, 0:w]
    k = sh[:, w:2 * w]
    v = sh[:, 2 * w:3 * w]
    wa = sh[:, 3 * w:3 * w + LANES]
    pre_w = w0_ref[...] + _dot(jnp.tanh(wa).astype(BF16), w2_ref[...])
    logd = -math.exp(-0.5) / (1.0 + jnp.exp(-pre_w))
    pre_a = a0_ref[...] + _dot(wa.astype(BF16), a2_ref[...])
    a = 1.0 / (1.0 + jnp.exp(-pre_a))
    if vres:
        vd = sh[:, 3 * w + LANES:3 * w + 2 * LANES]
        pre_v = v0_ref[...] + _dot(vd.astype(BF16), v2_ref[...])
        v = v + (vf_ref[...].astype(F32) - v) * (1.0 / (1.0 + jnp.exp(-pre_v)))
    else:
        vf_out[...] = v.astype(vf_out.dtype)
    def head_sum(t):
        t_hi = t.astype(BF16)
        t_lo = (t - t_hi.astype(F32)).astype(BF16)
        n = t.shape[1] // LANES
        stack = jnp.concatenate([piece[:, i * LANES:(i + 1) * LANES] for i in range(n) for piece in (t_hi, t_lo)], axis=0)
        res = _dot(stack, ones_ref[...])
        rows = t.shape[0]
        return jnp.concatenate([res[2 * i * rows:(2 * i + 1) * rows] + res[(2 * i + 1) * rows:(2 * i + 2) * rows]
                                for i in range(n)], axis=1)

    kk = k * kk_ref[...]
    kk = kk / jnp.maximum(jnp.sqrt(head_sum(kk * kk)), 1e-12)
    k2 = k * (1.0 + (a - 1.0) * ka_ref[...])

    ti = lax.broadcasted_iota(jnp.int32, (tr, tr), 0)
    tj = lax.broadcasted_iota(jnp.int32, (tr, tr), 1)
    tri = jnp.where((tj <= ti) & (tj >= ti // c * c), 1.0, 0.0).astype(BF16)
    hi, mid, lo = _split3(logd)
    cum = _dot(tri, hi) + _dot(tri, mid) + _dot(tri, lo)
    p_inc = jnp.exp(cum)
    p_exc = jnp.where(row % c == 0, 1.0, pltpu.roll(p_inc, 1, 0))
    p_inv = jnp.exp(-cum)
    p_all = [p_inc[(ci + 1) * c - 1:(ci + 1) * c, :] for ci in range(nchunk)]
    p_all_rows = jnp.concatenate([jnp.broadcast_to(pa, (c, D_WIDTH)) for pa in p_all], axis=0)
    kb = kk * a * p_inv
    kt = k2 * p_inv
    at_b = (-kk * p_exc).astype(BF16)
    rt_b = (r * p_inc).astype(BF16)
    bend_b = (kb * p_all_rows).astype(BF16)
    kend_b = (kt * p_all_rows).astype(BF16)
    vb = v.astype(BF16)

    pc = lax.broadcasted_iota(jnp.int32, (c, LANES), 1) % D_HEAD
    pr = lax.broadcasted_iota(jnp.int32, (c, LANES), 0)
    strict = pc < pr
    incl = pc <= pr
    eye = jnp.where(pc == pr, 1.0, 0.0)
    lo_lane = lax.broadcasted_iota(jnp.int32, (c, LANES), 1) < D_HEAD
    sr = lax.broadcasted_iota(jnp.int32, (LANES, LANES), 0) < D_HEAD
    sc = lax.broadcasted_iota(jnp.int32, (LANES, LANES), 1) < D_HEAD
    same_head = sr == sc

    def pair_diag(t):
        return jnp.concatenate([jnp.where(lo_lane, t, 0.0), jnp.where(lo_lane, 0.0, t)], axis=0).astype(BF16)

    items = [(ci, p) for ci in range(nchunk) for p in range(D_HEADS // 2)]
    rsl = lambda ci: slice(ci * c, (ci + 1) * c)
    psl = lambda p: slice(p * LANES, (p + 1) * LANES)
    sub = lambda arr, it: arr[rsl(it[0]), psl(it[1])]
    lhs = {it: jnp.concatenate([sub(at_b, it), sub(rt_b, it)], axis=0) for it in items}
    mats = {it: _dot_nt(lhs[it], jnp.concatenate([pair_diag(sub(kb, it)), pair_diag(sub(kt, it))], axis=0))
            for it in items}
    r_b = {it: jnp.where(incl, mats[it][c:, :LANES], 0.0).astype(BF16) for it in items}
    x0 = {it: jnp.where(strict, mats[it][:c, :LANES], 0.0) for it in items}
    xp = {it: _dot(x0[it].astype(BF16), pair_diag(x0[it])) for it in items}
    tinv = {it: eye + x0[it] for it in items}
    n_levels = int(math.log2(c))
    for lvl in range(1, n_levels - 1):
        prod = {it: _dot(jnp.concatenate([xp[it], tinv[it]], axis=0).astype(BF16), pair_diag(xp[it])) for it in items}
        xp = {it: prod[it][:c] for it in items}
        tinv = {it: tinv[it] + prod[it][c:] for it in items}
    tinv = {it: (tinv[it] + _dot(tinv[it].astype(BF16), pair_diag(xp[it]))).astype(BF16) for it in items}
    m_k = {it: jnp.concatenate([jnp.where(strict, mats[it][:c, LANES:], 0.0),
                                jnp.where(incl, mats[it][c:, LANES:], 0.0)], axis=0).astype(BF16) for it in items}
    from_v = {it: _dot(m_k[it], pair_diag(sub(v, it))) for it in items}
    upd_v = {it: jnp.where(same_head, _dot_tn(sub(vb, it), sub(kend_b, it)), 0.0) for it in items}

    state = [state_ref[p] for p in range(D_HEADS // 2)]
    pairs = range(D_HEADS // 2)
    for ci in range(nchunk):
        from_state = [_dot_nt(lhs[ci, p], state[p].astype(BF16)) for p in pairs]
        u = [_dot(tinv[ci, p], pair_diag(from_state[p][:c] + from_v[ci, p][:c])) for p in pairs]
        upd_u = [_dot_tn(u[p].astype(BF16), sub(bend_b, (ci, p))) for p in pairs]
        state = [state[p] * p_all[ci][:, psl(p)] + jnp.where(same_head, upd_u[p], 0.0) + upd_v[ci, p] for p in pairs]
        for p in pairs:
            y_ref[rsl(ci), psl(p)] = from_state[p][c:] + from_v[ci, p][c:] + _dot(r_b[ci, p], pair_diag(u[p]))
    for p in pairs:
        state_ref[p] = state[p]

    y = y_ref[...]
    yc = y - head_sum(y) * (1.0 / D_HEAD)
    var = head_sum(yc * yc) * (1.0 / D_HEAD)
    yn = yc * lax.rsqrt(var + RWKV_LN_EPS) * lw_ref[...] + lb_ref[...]
    bonus = head_sum(r * k2 * rk_ref[...]) * v
    out_ref[...] = (yn + bonus).astype(out_ref.dtype)


def _rwkv(dproj, v_first, p, b, s):
    tr = D_STEP_CHUNKS * D_CHUNK
    vres = v_first is not None
    row = lambda wd: pl.BlockSpec((None, tr, wd), lambda bi, i: (bi, i, 0))
    full = lambda a: pl.BlockSpec(a.shape, lambda bi, i: (0, 0))
    ins = [dproj.reshape(b, s, D_PROJ_PAD)]
    in_specs = [row(D_PROJ_PAD)]
    if vres:
        ins.append(v_first.reshape(b, s, D_WIDTH))
        in_specs.append(row(D_WIDTH))
    names = ['mu', 'w0', 'w2', 'a0', 'a2'] + (['v0', 'v2'] if vres else []) + ['k_k', 'k_a', 'r_k', 'lnx_w', 'lnx_b']
    for nm in names:
        ins.append(p[nm])
        in_specs.append(full(p[nm]))
    head_id = np.arange(LANES) // D_HEAD
    ones_bd = jnp.asarray(head_id[:, None] == head_id[None, :], BF16)
    ins.append(ones_bd)
    in_specs.append(full(ones_bd))
    out_shape = [jax.ShapeDtypeStruct((b, s, D_WIDTH), BF16)]
    out_specs = [row(D_WIDTH)]
    if not vres:
        out_shape.append(jax.ShapeDtypeStruct((b, s, D_WIDTH), BF16))
        out_specs.append(row(D_WIDTH))
    res = pl.pallas_call(
        functools.partial(_rwkv_kernel, vres=vres, nchunk=D_STEP_CHUNKS), grid=(b, s // tr),
        in_specs=in_specs, out_specs=out_specs, out_shape=out_shape,
        scratch_shapes=[pltpu.VMEM((D_HEADS // 2, LANES, LANES), F32),
                        pltpu.VMEM((1, D_PROJ_PAD), F32),
                        pltpu.VMEM((tr, D_WIDTH), F32)],
        compiler_params=_cparams("parallel", "arbitrary"), name="rwkv7",
    )(*ins)
    d_out = res[0].reshape(b * s, D_WIDTH)
    vf = v_first if vres else res[1].reshape(b * s, D_WIDTH)
    return d_out, vf


def _rope_tables(positions, dim, theta, period, base, half):
    inv = jnp.exp(-math.log(theta) * jnp.arange(0, dim, 2, dtype=F32) / dim)
    n = positions.shape[0] * positions.shape[1]
    lane = np.arange(LANES) % period - base
    lo = (lane >= 0) & (lane < half)
    hi = (lane >= half) & (lane < 2 * half)
    src = np.where(lo, lane, np.where(hi, lane - half, 0))
    ang = positions.astype(F32).reshape(n, 1) * inv[src][None, :]
    c = jnp.where(jnp.asarray(lo | hi), jnp.cos(ang), 1.0)
    s1 = jnp.where(jnp.asarray(hi), jnp.sin(ang), 0.0)
    s2 = jnp.where(jnp.asarray(lo), -jnp.sin(ang), 0.0)
    return c, s1, s2


A_SUB_DILATIONS = tuple(d for d in A_DILATIONS if d > 1)


def _even_proj(norm, w_in, tabs_a, tabs_r):
    q_scale = HEAD_DIM ** -0.5 * LOG2E
    aw = A_WIDTH
    plan = [(0, aw, 0, 0, (0, ROT_DIM // 2, q_scale), 0), (aw, aw, 0, aw, (0, ROT_DIM // 2, 1.0), aw),
            (2 * aw, aw, 0, 2 * aw, None, 2 * aw),
            (1536, 512, 0, 1536, (1, B_QK_DIM // 2, 1.0)), (2048, 512, 0, 2048, None),
            (2560, 256, 1, 0, (None, 0, MEM_Q_SCALE)), (2816, 640, 2, 0, None), (3456, 640, 2, 640, None)]
    return _proj_spec(norm, w_in.astype(BF16), plan, (2560, 256, MIX_WIDTH), (tabs_a, tabs_r),
                      dils=A_SUB_DILATIONS, sub_width=3 * aw)


def _even_mixers(res, b, s, mem_kv):
    main, qm, gate = res[:3]
    main3 = main.reshape(b, s, 2560)
    qkv = {1: main3.reshape(b, 1, s, 2560)}
    qkv.update(zip(A_SUB_DILATIONS, res[3:]))
    outs, lses = [], []
    for dil in A_DILATIONS:
        o, lse = _band_attn(qkv[dil], 0, dil)
        outs.append(o)
        lses.append(lse)
    r_out = _retention(main3)
    m_out = _mem_attn(qm, mem_kv, b, s)
    return (outs, lses), r_out, m_out, gate


def _odd_proj(norm, w_in, d_cols):
    k_dim = w_in.shape[0]
    w_in = w_in.astype(BF16)
    z = lambda n: jnp.zeros((k_dim, n), BF16)
    o = 0
    cq = w_in[:, o:o + C_Q_RANK]; o += C_Q_RANK
    ckv = w_in[:, o:o + C_KV_RANK]; o += C_KV_RANK
    kro = w_in[:, o:o + C_ROPE]; o += C_ROPE
    dp = w_in[:, o:o + d_cols]; o += d_cols
    qm_w = w_in[:, o:o + MEM_WIDTH]; o += MEM_WIDTH
    gate_w = w_in[:, o:]
    dp_pad = jnp.concatenate([dp, z(D_PROJ_PAD - d_cols)], axis=1)
    swap = lambda t: jnp.concatenate([t[..., C_ROPE // 2:], t[..., :C_ROPE // 2]], axis=-1)
    rope_slab = lambda t: jnp.concatenate([z(C_NOPE), t, z(LANES - C_NOPE - C_ROPE)], axis=1)
    w_pad = jnp.concatenate([cq, ckv, rope_slab(kro), rope_slab(swap(kro)), dp_pad, qm_w, gate_w], axis=1)
    m0 = C_Q_RANK + C_KV_RANK + 2 * LANES
    plan = [(0, m0, 0, 0, None), (m0, 896, 1, 0, None), (m0 + 896, 896, 1, 896, None),
            (m0 + 1792, 256, 2, 0, (None, 0, MEM_Q_SCALE)),
            (m0 + 2048, 640, 3, 0, None), (m0 + 2688, 640, 3, 640, None)]
    return _proj_spec(norm, w_pad.astype(BF16), plan, (m0, D_PROJ_PAD, 256, MIX_WIDTH))


def _odd_mixers(res, b, s, mem_kv, tabs_m, v_first, q_norm, w_qb, kv_norm, w_kvb,
                mu_shift, w0, w2, a0, a2, v0, v2, k_k, k_a, r_k, lnx_w, lnx_b):
    mla_in, dproj, qm, gate = res
    vres = v0 is not None
    d_cols = mu_shift.shape[0]
    swap = lambda t: jnp.concatenate([t[..., C_ROPE // 2:], t[..., :C_ROPE // 2]], axis=-1)
    wq = w_qb.reshape(C_Q_RANK, C_HEADS, C_NOPE + C_ROPE)
    slab_pad = ((0, 0), (0, 0), (0, LANES - C_NOPE - C_ROPE))
    wq_sw = jnp.concatenate([wq[..., :C_NOPE], swap(wq[..., C_NOPE:])], axis=-1)
    wq_sw = jnp.pad(wq_sw, slab_pad).reshape(C_Q_RANK, C_HEADS * LANES)
    wq = jnp.pad(wq, slab_pad).reshape(C_Q_RANK, C_HEADS * LANES)
    wkv = w_kvb.reshape(C_KV_RANK, C_HEADS, C_NOPE + C_V)
    wk = jnp.pad(wkv[:, :, :C_NOPE], ((0, 0), (0, 0), (0, LANES - C_NOPE))).reshape(C_KV_RANK, C_HEADS * LANES)
    wv = jnp.pad(wkv[:, :, C_NOPE:], ((0, 0), (0, 0), (0, LANES - C_V))).reshape(C_KV_RANK, C_HEADS * LANES)
    q, k, v = _mla_prep(mla_in, q_norm, kv_norm, wq.astype(BF16), wq_sw.astype(BF16), wk.astype(BF16),
                        wv.astype(BF16), tabs_m)
    c_out = _mla_attn(q, k, v, b, s)

    row = lambda t: t.reshape(1, -1).astype(F32)
    pad_rows = lambda t, top, n: jnp.concatenate(
        [jnp.zeros((top, t.shape[1]), F32), t, jnp.zeros((n - top - t.shape[0], t.shape[1]), F32)], axis=0)
    p = dict(mu=row(jnp.concatenate([mu_shift, jnp.zeros((D_PROJ_PAD - d_cols,), F32)])),
             w0=row(w0), w2=pad_rows(w2, 0, LANES).astype(BF16),
             a0=row(a0), a2=pad_rows(a2, D_DECAY_LORA, LANES).astype(BF16),
             k_k=row(k_k), k_a=row(k_a), r_k=row(r_k), lnx_w=row(lnx_w), lnx_b=row(lnx_b))
    if vres:
        p['v0'] = row(v0)
        p['v2'] = pad_rows(v2, 0, LANES).astype(BF16)
    d_out, v_first = _rwkv(dproj, v_first, p, b, s)
    m_out = _mem_attn(qm, mem_kv, b, s)
    return (c_out, d_out, m_out, gate), v_first


def kernel(x, mem, positions, mem_norm, final_norm, l0_norm, l0_w_in, l0_w_mem_kv, l0_w_out, l1_norm, l1_w_in, l1_q_norm, l1_w_qb, l1_kv_norm, l1_w_kvb, l1_mu_shift, l1_w0, l1_w2, l1_a0, l1_a2, l1_k_k, l1_k_a, l1_r_k, l1_lnx_w, l1_lnx_b, l1_w_mem_kv, l1_w_out, l2_norm, l2_w_in, l2_w_mem_kv, l2_w_out, l3_norm, l3_w_in, l3_q_norm, l3_w_qb, l3_kv_norm, l3_w_kvb, l3_mu_shift, l3_w0, l3_w2, l3_a0, l3_a2, l3_v0, l3_v2, l3_k_k, l3_k_a, l3_r_k, l3_lnx_w, l3_lnx_b, l3_w_mem_kv, l3_w_out):
    b, s, d = x.shape
    x2d = x.reshape(b * s, d)
    mem2d = mem.reshape(b * MEM_TOKENS, d)
    tabs_a = _rope_tables(positions, ROT_DIM, ROPE_THETA, HEAD_DIM, 0, ROT_DIM // 2)
    tabs_r = _rope_tables(positions, B_QK_DIM, RET_THETA, B_QK_DIM, 0, B_QK_DIM // 2)
    tabs_m = _rope_tables(positions, C_ROPE, ROPE_THETA, LANES, C_NOPE, C_ROPE // 2)

    kvw = 2 * MEM_WIDTH
    w_mem = jnp.concatenate([w.astype(BF16) for w in (l0_w_mem_kv, l1_w_mem_kv, l2_w_mem_kv, l3_w_mem_kv)], axis=1)
    mem_kvs = _norm_proj(mem2d, mem_norm, w_mem, [(kvw * l, kvw, l, 0, None) for l in range(4)], (kvw,) * 4,
                         tm=MEM_TOKENS)

    proj = [_even_proj(l0_norm, l0_w_in, tabs_a, tabs_r), _odd_proj(l1_norm, l1_w_in, l1_mu_shift.shape[0]),
            _even_proj(l2_norm, l2_w_in, tabs_a, tabs_r), _odd_proj(l3_norm, l3_w_in, l3_mu_shift.shape[0])]
    w_outs = [w.astype(BF16) for w in (l0_w_out, l1_w_out, l2_w_out, l3_w_out)]
    p0 = proj[0]
    res = _norm_proj(x2d, p0['g'], p0['w'], p0['plan'], p0['out_widths'], p0['tables'], dils=p0['dils'],
                     sub_width=p0['sub_width'], seq=s)
    ops = _even_mixers(res, b, s, mem_kvs[0])
    x2d, *res = _out_proj(*ops, x2d, w_outs[0], seq=s, next_proj=proj[1])
    ops, v_first = _odd_mixers(res, b, s, mem_kvs[1], tabs_m, None, l1_q_norm, l1_w_qb, l1_kv_norm, l1_w_kvb,
                               l1_mu_shift, l1_w0, l1_w2, l1_a0, l1_a2, None, None,
                               l1_k_k, l1_k_a, l1_r_k, l1_lnx_w, l1_lnx_b)
    x2d, *res = _out_proj(*ops, x2d, w_outs[1], seq=s, next_proj=proj[2])
    ops = _even_mixers(res, b, s, mem_kvs[2])
    x2d, *res = _out_proj(*ops, x2d, w_outs[2], seq=s, next_proj=proj[3])
    ops, _ = _odd_mixers(res, b, s, mem_kvs[3], tabs_m, v_first, l3_q_norm, l3_w_qb, l3_kv_norm, l3_w_kvb,
                         l3_mu_shift, l3_w0, l3_w2, l3_a0, l3_a2, l3_v0, l3_v2,
                         l3_k_k, l3_k_a, l3_r_k, l3_lnx_w, l3_lnx_b)
    out = _out_proj(*ops, x2d, w_outs[3], final_gain=final_norm)
    return out.reshape(b, s, d)
```
